```python
import math
import jax, jax.numpy as jnp
from jax import lax
import numpy as np

D_MODEL = 2048
BATCH = 1
SEQ = 8192
DEPTH = 2
DEC_BATCH = 32
DEC_SEQ = 32
PAST_LEN = 4096

CHUNK = 64
Q_BLOCK = 128
EPS = 1e-6
D_FF = 5632
GLA_HEADS = 4
GLA_DK = 128
GLA_DV = 256
GLA_GK_RANK = 16
GLA_GATE_NORMALIZER = 16.0
DIFF_HEADS = 4
DIFF_D = 128
DIFF_DV = 2 * DIFF_D
N_BUCKETS = 32
MAX_DISTANCE = 128

GLA_K_W = GLA_HEADS * GLA_DK
GLA_V_W = GLA_HEADS * GLA_DV
DIFF_QK_W = DIFF_HEADS * 2 * DIFF_D
DIFF_V_W = DIFF_HEADS * DIFF_DV
MIX_WIDTH = GLA_V_W + DIFF_V_W
IN_COLS = 2 * GLA_K_W + 2 * GLA_V_W + GLA_GK_RANK + 2 * DIFF_QK_W + DIFF_V_W
SPLITS = [int(s) for s in np.cumsum([GLA_K_W, GLA_K_W, GLA_V_W, GLA_V_W, GLA_GK_RANK, DIFF_QK_W, DIFF_QK_W])]

kernel_name = "hybrid_gla_diffattn_streaming_step"


def rmsnorm(x, w):
    xf = x.astype(jnp.float32)
    xf = xf * lax.rsqrt(jnp.mean(xf * xf, axis=-1, keepdims=True) + EPS)
    return xf.astype(x.dtype) * w


def swiglu(x, w_gate, w_up, w_down):
    return (jax.nn.silu(x @ w_gate) * (x @ w_up)) @ w_down


def lambda_init(l):
    return 0.8 - 0.6 * math.exp(-0.3 * l)


def t5_bucket(rel):
    half = N_BUCKETS // 2
    max_exact = half // 2
    ret = jnp.where(rel > 0, half, 0)
    n = jnp.abs(rel)
    nf = jnp.maximum(n, 1).astype(jnp.float32)
    large = max_exact + (jnp.log(nf / max_exact) / math.log(MAX_DISTANCE / max_exact)
                         * (half - max_exact)).astype(jnp.int32)
    large = jnp.minimum(large, half - 1)
    return ret + jnp.where(n < max_exact, n, large)


def _gla_chunk(S, qkvg):
    q, k, v, g = qkvg
    b = jnp.cumsum(g, axis=2)
    b_last = b[:, :, -1:, :]
    qe = q * jnp.exp(b)
    ke = k * jnp.exp(-b)
    C = q.shape[2]
    causal = jnp.tril(jnp.ones((C, C), dtype=bool))
    A = jnp.where(causal, jnp.einsum('bhid,bhjd->bhij', qe, ke), 0.0)
    o = jnp.einsum('bhid,bhde->bhie', qe, S) + jnp.einsum('bhij,bhje->bhie', A, v)
    k_dec = k * jnp.exp(b_last - b)
    S_new = jnp.exp(b_last[:, :, 0, :])[..., None] * S + jnp.einsum('bhjd,bhje->bhde', k_dec, v)
    return S_new, o


def gla_scan(q, k, v, g, S0, chunk):
    B, T = q.shape[:2]
    n = T // chunk

    def to_chunks(a):
        return a.astype(jnp.float32).reshape(B, n, chunk, a.shape[2], a.shape[3]).transpose(1, 0, 3, 2, 4)

    S, o = lax.scan(_gla_chunk, S0, (to_chunks(q), to_chunks(k), to_chunks(v), to_chunks(g)))
    o = o.transpose(1, 0, 3, 2, 4).reshape(B, T, GLA_HEADS, GLA_DV)
    return o, S


def diff_logits(q, k, q_pos, k_pos, rel_bias):
    qh = q.reshape(q.shape[:-1] + (2, DIFF_D))
    kh = k.reshape(k.shape[:-1] + (2, DIFF_D))
    s = jnp.einsum('bqhcd,bkhcd->cbhqk', qh, kh).astype(jnp.float32) * (DIFF_D ** -0.5)
    bucket = t5_bucket(k_pos[None, :] - q_pos[:, None])
    bias = rel_bias.astype(jnp.float32)[bucket].transpose(2, 0, 1)
    mask = (k_pos // CHUNK)[None, :] <= (q_pos // CHUNK)[:, None]
    return jnp.where(mask, s + bias, -jnp.inf)


def diff_weights(logits, lam):
    p = jax.nn.softmax(logits, axis=-1)
    return p[0] - lam * p[1]


def diff_attn_prompt(q, k, v, lam, rel_bias):
    B, T = q.shape[:2]
    nb = T // Q_BLOCK
    pos = jnp.arange(T)
    qb = q.reshape(B, nb, Q_BLOCK, DIFF_HEADS, 2 * DIFF_D).transpose(1, 0, 2, 3, 4)
    pb = pos.reshape(nb, Q_BLOCK)

    def block(args):
        q_blk, q_pos = args
        w = diff_weights(diff_logits(q_blk, k, q_pos, pos, rel_bias), lam)
        return jnp.einsum('bhqk,bkhe->bqhe', w.astype(v.dtype), v)

    o = lax.map(block, (qb, pb))
    return o.transpose(1, 0, 2, 3, 4).reshape(B, T, DIFF_HEADS, DIFF_DV)


def diff_attn_sample(q, k_new, v_new, k_cache, v_cache, lam, rel_bias):
    T = q.shape[1]
    P = k_cache.shape[1]
    q_pos = P + jnp.arange(T)
    lc = diff_logits(q, k_cache, q_pos, jnp.arange(P), rel_bias)
    ln = diff_logits(q, k_new, q_pos, q_pos, rel_bias)
    w = diff_weights(jnp.concatenate([lc, ln], axis=-1), lam).astype(v_new.dtype)
    return (jnp.einsum('bhqk,bkhe->bqhe', w[..., :P], v_cache)
            + jnp.einsum('bhqk,bkhe->bqhe', w[..., P:], v_new))


def mixer(h, w_in, gla_w_gk, gla_b_gk, gla_norm_w, diff_lam, diff_norm_w, w_out, l, S0, gla_chunk, attn_fn):
    B, T, _ = h.shape
    gq, gk, gv, gr, gz, dq, dk, dv = jnp.split(h @ w_in, SPLITS, axis=-1)

    def heads(a, H):
        return a.reshape(B, T, H, -1)

    log_a = jax.nn.log_sigmoid((gz @ gla_w_gk + gla_b_gk).astype(jnp.float32)) / GLA_GATE_NORMALIZER
    o_gla, S = gla_scan(heads(gq, GLA_HEADS) * (GLA_DK ** -0.5), heads(gk, GLA_HEADS),
                        heads(gv, GLA_HEADS), heads(log_a, GLA_HEADS), S0, gla_chunk)
    o_gla = rmsnorm(o_gla.astype(h.dtype), gla_norm_w) * jax.nn.silu(heads(gr, GLA_HEADS))

    lam_i = lambda_init(l)
    lamf = diff_lam.astype(jnp.float32)
    lam = jnp.exp(jnp.dot(lamf[0], lamf[1])) - jnp.exp(jnp.dot(lamf[2], lamf[3])) + lam_i
    k_rows = heads(dk, DIFF_HEADS)
    v_rows = heads(dv, DIFF_HEADS)
    o_diff = attn_fn(heads(dq, DIFF_HEADS), k_rows, v_rows, lam)
    o_diff = rmsnorm(o_diff, diff_norm_w) * (1.0 - lam_i)

    o = jnp.concatenate([o_gla.reshape(B, T, GLA_V_W), o_diff.reshape(B, T, DIFF_V_W)], axis=-1)
    return o @ w_out, S, k_rows, v_rows


def setup_inputs(seed: int = 0) -> dict:
    key = jax.random.key(seed)
    ks = jax.random.split(key, 24)
    f = jnp.float32

    def nrm(k, shape, scale=1.0):
        return jax.random.normal(k, shape, f) * scale

    def gain(k, shape):
        return 1.0 + 0.01 * jax.random.normal(k, shape, f)

    return {
        "x_prompt": nrm(ks[0], (BATCH, SEQ, D_MODEL)),
        "x_sample": nrm(ks[1], (DEC_BATCH, DEC_SEQ, D_MODEL)),
        "cache_k": nrm(ks[2], (DEPTH, DEC_BATCH, PAST_LEN, DIFF_HEADS, 2 * DIFF_D)),
        "cache_v": nrm(ks[3], (DEPTH, DEC_BATCH, PAST_LEN, DIFF_HEADS, DIFF_DV)),
        "state_gla": nrm(ks[4], (DEPTH, DEC_BATCH, GLA_HEADS, GLA_DK, GLA_DV), 0.5),
        "ffn1_norm": gain(ks[5], (DEPTH, D_MODEL)),
        "ffn1_w_gate": nrm(ks[6], (DEPTH, D_MODEL, D_FF), D_MODEL ** -0.5),
        "ffn1_w_up": nrm(ks[7], (DEPTH, D_MODEL, D_FF), D_MODEL ** -0.5),
        "ffn1_w_down": nrm(ks[8], (DEPTH, D_FF, D_MODEL), D_FF ** -0.5),
        "mix_norm": gain(ks[9], (DEPTH, D_MODEL)),
        "w_in": nrm(ks[10], (DEPTH, D_MODEL, IN_COLS), D_MODEL ** -0.5),
        "gla_w_gk": nrm(ks[11], (DEPTH, GLA_GK_RANK, GLA_K_W), GLA_GK_RANK ** -0.5),
        "gla_b_gk": nrm(ks[12], (DEPTH, GLA_K_W), 0.1),
        "gla_norm": gain(ks[13], (DEPTH, GLA_DV)),
        "diff_lambda": nrm(ks[14], (DEPTH, 4, DIFF_D), 0.1),
        "diff_norm": gain(ks[15], (DEPTH, DIFF_DV)),
        "w_out": nrm(ks[16], (DEPTH, MIX_WIDTH, D_MODEL), MIX_WIDTH ** -0.5),
        "ffn2_norm": gain(ks[17], (DEPTH, D_MODEL)),
        "ffn2_w_gate": nrm(ks[18], (DEPTH, D_MODEL, D_FF), D_MODEL ** -0.5),
        "ffn2_w_up": nrm(ks[19], (DEPTH, D_MODEL, D_FF), D_MODEL ** -0.5),
        "ffn2_w_down": nrm(ks[20], (DEPTH, D_FF, D_MODEL), D_FF ** -0.5),
        "rel_bias": nrm(ks[21], (N_BUCKETS, DIFF_HEADS), 0.5),
        "final_norm": gain(ks[22], (D_MODEL,)),
    }


def reference(x_prompt, x_sample, cache_k, cache_v, state_gla,
              ffn1_norm, ffn1_w_gate, ffn1_w_up, ffn1_w_down,
              mix_norm, w_in, gla_w_gk, gla_b_gk, gla_norm, diff_lambda, diff_norm, w_out,
              ffn2_norm, ffn2_w_gate, ffn2_w_up, ffn2_w_down, rel_bias, final_norm):

    def run(x, S_init, attn_for_layer, gla_chunk):
        ks, vs, Ss = [], [], []
        for l in range(DEPTH):
            x = x + 0.5 * swiglu(rmsnorm(x, ffn1_norm[l]), ffn1_w_gate[l], ffn1_w_up[l], ffn1_w_down[l])
            y, S, kr, vr = mixer(rmsnorm(x, mix_norm[l]), w_in[l], gla_w_gk[l], gla_b_gk[l], gla_norm[l],
                                 diff_lambda[l], diff_norm[l], w_out[l], l, S_init(l), gla_chunk,
                                 attn_for_layer(l))
            x = x + y
            x = x + 0.5 * swiglu(rmsnorm(x, ffn2_norm[l]), ffn2_w_gate[l], ffn2_w_up[l], ffn2_w_down[l])
            ks.append(kr)
            vs.append(vr)
            Ss.append(S.astype(x.dtype))
        return rmsnorm(x, final_norm), jnp.stack(ks), jnp.stack(vs), jnp.stack(Ss)

    y_prompt, prompt_k, prompt_v, prompt_gla = run(
        x_prompt,
        lambda l: jnp.zeros((x_prompt.shape[0], GLA_HEADS, GLA_DK, GLA_DV), jnp.float32),
        lambda l: (lambda q, k, v, lam: diff_attn_prompt(q, k, v, lam, rel_bias)),
        CHUNK)

    y_sample, sample_k, sample_v, sample_gla = run(
        x_sample,
        lambda l: state_gla[l].astype(jnp.float32),
        lambda l: (lambda q, k, v, lam: diff_attn_sample(q, k, v, cache_k[l], cache_v[l], lam, rel_bias)),
        x_sample.shape[1])

    return (y_prompt, y_sample, prompt_k, prompt_v, prompt_gla, sample_k, sample_v, sample_gla)
```

```python
import functools
import math

import jax
import jax.numpy as jnp
from jax import lax
from jax.experimental import pallas as pl
from jax.experimental.pallas import tpu as pltpu

F32 = jnp.float32
BF16 = jnp.bfloat16

EPS = 1e-6
CHUNK = 64
GLA_HEADS = 4
GLA_DK = 128
GLA_DV = 256
GLA_GK_RANK = 16
GLA_GATE_NORMALIZER = 16.0
DIFF_HEADS = 4
DIFF_D = 128
DIFF_DV = 2 * DIFF_D
N_BUCKETS = 32
MAX_DISTANCE = 128

GLA_K_W = GLA_HEADS * GLA_DK
GLA_V_W = GLA_HEADS * GLA_DV
DIFF_QK_W = DIFF_HEADS * 2 * DIFF_D
DIFF_V_W = DIFF_HEADS * DIFF_DV
MIX_WIDTH = GLA_V_W + DIFF_V_W
GZ_OFF = 2 * GLA_K_W + 2 * GLA_V_W
MAIN_W = GZ_OFF + DIFF_QK_W
DQ_OFF = GZ_OFF

LANE = 128
VMEM_LIMIT = 56 * 1024 * 1024

ROW_TILE = 512
FF_TILE = 512
IN_TILE = 512
ATT_BLOCK = 512
GLA_ROWS = 256


def _dot(a, b):
    return jnp.dot(a, b, preferred_element_type=F32)


def _dot_nt(a, b):
    return lax.dot_general(a, b, (((1,), (1,)), ((), ())), preferred_element_type=F32)


def _dot_tn(a, b):
    return lax.dot_general(a, b, (((0,), (0,)), ((), ())), preferred_element_type=F32)


def _params(*sem):
    return pltpu.CompilerParams(dimension_semantics=sem, vmem_limit_bytes=VMEM_LIMIT)


def _row_tile(n):
    t = ROW_TILE
    while n % t:
        t //= 2
    return t


def _rms_scale(x):
    return lax.rsqrt(jnp.mean(x * x, axis=-1, keepdims=True) + EPS)


def _ffn_body(x_ref, nw_ref, wg_ref, wu_ref, wd_ref, fw_ref, o_ref, xn_ref, acc_ref, *, final):
    j = pl.program_id(1)

    @pl.when(j == 0)
    def _():
        x = x_ref[...]
        xn_ref[...] = ((x * _rms_scale(x)) * nw_ref[...]).astype(BF16)
        acc_ref[...] = jnp.zeros_like(acc_ref)

    xn = xn_ref[...]
    h = _dot(xn, wg_ref[...])
    u = _dot(xn, wu_ref[...])
    a = (h * jax.nn.sigmoid(h) * u).astype(BF16)
    acc_ref[...] += _dot(a, wd_ref[...])

    @pl.when(j == pl.num_programs(1) - 1)
    def _():
        y = x_ref[...] + 0.5 * acc_ref[...]
        if final:
            y = (y * _rms_scale(y)) * fw_ref[...]
        o_ref[...] = y


def _ffn(x, nw, wg, wu, wd, fw, final):
    n, d = x.shape
    f = wg.shape[1]
    tm = _row_tile(n)
    return pl.pallas_call(
        functools.partial(_ffn_body, final=final),
        grid=(n // tm, f // FF_TILE),
        in_specs=[
            pl.BlockSpec((tm, d), lambda i, j: (i, 0)),
            pl.BlockSpec((1, d), lambda i, j: (0, 0)),
            pl.BlockSpec((d, FF_TILE), lambda i, j: (0, j)),
            pl.BlockSpec((d, FF_TILE), lambda i, j: (0, j)),
            pl.BlockSpec((FF_TILE, d), lambda i, j: (j, 0)),
            pl.BlockSpec((1, d), lambda i, j: (0, 0)),
        ],
        out_specs=pl.BlockSpec((tm, d), lambda i, j: (i, 0)),
        out_shape=jax.ShapeDtypeStruct((n, d), F32),
        scratch_shapes=[pltpu.VMEM((tm, d), BF16), pltpu.VMEM((tm, d), F32)],
        compiler_params=_params("parallel", "arbitrary"),
        name="ffn",
    )(x, nw, wg, wu, wd, fw)


N_MAIN_BLK = MAIN_W // IN_TILE
N_K_BLK = DIFF_QK_W // IN_TILE
N_V_BLK = DIFF_V_W // IN_TILE


def _inproj_body(x_ref, nw_ref, w_ref, wgz_ref, wgk_ref, bgk_ref,
                 main_ref, k_ref, v_ref, kvb_ref, g_ref, xn_ref):
    j = pl.program_id(1)

    @pl.when(j == 0)
    def _():
        x = x_ref[...]
        xn = ((x * _rms_scale(x)) * nw_ref[...]).astype(BF16)
        xn_ref[...] = xn
        gz = _dot(xn, wgz_ref[...])
        z = _dot(gz.astype(BF16), wgk_ref[...]) + bgk_ref[...]
        g_ref[...] = (jnp.minimum(z, 0.0) - jnp.log1p(jnp.exp(-jnp.abs(z)))) * (1.0 / GLA_GATE_NORMALIZER)

    y = _dot(xn_ref[...], w_ref[...])

    @pl.when(j < N_MAIN_BLK)
    def _():
        main_ref[...] = y

    @pl.when(jnp.logical_and(j >= N_MAIN_BLK, j < N_MAIN_BLK + N_K_BLK))
    def _():
        k_ref[...] = y
        kvb_ref[...] = y.astype(BF16)

    @pl.when(j >= N_MAIN_BLK + N_K_BLK)
    def _():
        v_ref[...] = y
        kvb_ref[...] = y.astype(BF16)


def _inproj(x, nw, w_main, w_gz, w_gk, b_gk):
    n, d = x.shape
    tm = _row_tile(n)
    nj = N_MAIN_BLK + N_K_BLK + N_V_BLK
    k0 = N_MAIN_BLK
    v0 = N_MAIN_BLK + N_K_BLK
    return pl.pallas_call(
        _inproj_body,
        grid=(n // tm, nj),
        in_specs=[
            pl.BlockSpec((tm, d), lambda i, j: (i, 0)),
            pl.BlockSpec((1, d), lambda i, j: (0, 0)),
            pl.BlockSpec((d, IN_TILE), lambda i, j: (0, j)),
            pl.BlockSpec((d, LANE), lambda i, j: (0, 0)),
            pl.BlockSpec((LANE, GLA_K_W), lambda i, j: (0, 0)),
            pl.BlockSpec((1, GLA_K_W), lambda i, j: (0, 0)),
        ],
        out_specs=[
            pl.BlockSpec((tm, IN_TILE), lambda i, j: (i, jnp.minimum(j, k0 - 1))),
            pl.BlockSpec((tm, IN_TILE), lambda i, j: (i, jnp.clip(j - k0, 0, N_K_BLK - 1))),
            pl.BlockSpec((tm, IN_TILE), lambda i, j: (i, jnp.clip(j - v0, 0, N_V_BLK - 1))),
            pl.BlockSpec((tm, IN_TILE), lambda i, j: (i, jnp.clip(j - k0, 0, N_K_BLK + N_V_BLK - 1))),
            pl.BlockSpec((tm, GLA_K_W), lambda i, j: (i, 0)),
        ],
        out_shape=[
            jax.ShapeDtypeStruct((n, MAIN_W), F32),
            jax.ShapeDtypeStruct((n, DIFF_QK_W), F32),
            jax.ShapeDtypeStruct((n, DIFF_V_W), F32),
            jax.ShapeDtypeStruct((n, DIFF_QK_W + DIFF_V_W), BF16),
            jax.ShapeDtypeStruct((n, GLA_K_W), F32),
        ],
        scratch_shapes=[pltpu.VMEM((tm, d), BF16)],
        compiler_params=_params("parallel", "arbitrary"),
        name="inproj",
    )(x, nw, w_main, w_gz, w_gk, b_gk)


def _split3(a):
    hi = a.astype(BF16)
    r = a - hi.astype(F32)
    mid = r.astype(BF16)
    lo = (r - mid.astype(F32)).astype(BF16)
    return hi, mid, lo


def _gla_body(q_ref, k_ref, v_ref, r_ref, g_ref, s0_ref, nw_ref, mix_in_ref, o_ref, s_ref, st_ref, *, chunk):
    del mix_in_ref
    t = pl.program_id(2)
    rows = q_ref.shape[0]

    @pl.when(t == 0)
    def _():
        st_ref[...] = s0_ref[0, 0].T

    ri = lax.broadcasted_iota(jnp.int32, (chunk, chunk), 0)
    ci = lax.broadcasted_iota(jnp.int32, (chunk, chunk), 1)
    causal = ci <= ri
    tri = causal.astype(BF16)

    for c in range(rows // chunk):
        sl = pl.ds(c * chunk, chunk)
        q = q_ref[sl, :] * (GLA_DK ** -0.5)
        k = k_ref[sl, :]
        v = v_ref[sl, :].astype(BF16)
        g = g_ref[sl, :]
        g_hi, g_mid, g_lo = _split3(g)
        b = _dot(tri, g_hi) + _dot(tri, g_mid) + _dot(tri, g_lo)
        b_last = b[chunk - 1:chunk, :]
        qe = (q * jnp.exp(b)).astype(BF16)
        ke = (k * jnp.exp(-b)).astype(BF16)
        kd = (k * jnp.exp(b_last - b)).astype(BF16)
        a = jnp.where(causal, _dot_nt(qe, ke), 0.0).astype(BF16)
        st = st_ref[...]
        o = _dot_nt(qe, st.astype(BF16)) + _dot(a, v)
        st_ref[...] = st * jnp.exp(b_last) + _dot_tn(v, kd)
        o = (o * _rms_scale(o)) * nw_ref[...]
        r = r_ref[sl, :]
        o_ref[sl, :] = (o * (r * jax.nn.sigmoid(r))).astype(BF16)

    @pl.when(t == pl.num_programs(2) - 1)
    def _():
        s_ref[0, 0] = st_ref[...].T


def _gla(main, g, s0, nw, mix, *, row_off, batch, seq, chunk, rows):
    assert seq % rows == 0 and rows % chunk == 0 and row_off % rows == 0
    nt = seq // rows
    rb0 = row_off // rows
    kq = GLA_DK // GLA_DK
    del kq

    def rowblk(b, h, t):
        return rb0 + b * nt + t

    v_col0 = 2 * GLA_K_W // GLA_DV
    r_col0 = (2 * GLA_K_W + GLA_V_W) // GLA_DV
    out, s = pl.pallas_call(
        functools.partial(_gla_body, chunk=chunk),
        grid=(batch, GLA_HEADS, nt),
        in_specs=[
            pl.BlockSpec((rows, GLA_DK), lambda b, h, t: (rowblk(b, h, t), h)),
            pl.BlockSpec((rows, GLA_DK), lambda b, h, t: (rowblk(b, h, t), GLA_HEADS + h)),
            pl.BlockSpec((rows, GLA_DV), lambda b, h, t: (rowblk(b, h, t), v_col0 + h)),
            pl.BlockSpec((rows, GLA_DV), lambda b, h, t: (rowblk(b, h, t), r_col0 + h)),
            pl.BlockSpec((rows, GLA_DK), lambda b, h, t: (rowblk(b, h, t), h)),
            pl.BlockSpec((1, 1, GLA_DK, GLA_DV), lambda b, h, t: (b, h, 0, 0)),
            pl.BlockSpec((1, GLA_DV), lambda b, h, t: (0, 0)),
            pl.BlockSpec(memory_space=pl.ANY),
        ],
        out_specs=[
            pl.BlockSpec((rows, GLA_DV), lambda b, h, t: (rowblk(b, h, t), h)),
            pl.BlockSpec((1, 1, GLA_DK, GLA_DV), lambda b, h, t: (b, h, 0, 0)),
        ],
        out_shape=[
            jax.ShapeDtypeStruct(mix.shape, mix.dtype),
            jax.ShapeDtypeStruct((batch, GLA_HEADS, GLA_DK, GLA_DV), F32),
        ],
        scratch_shapes=[pltpu.VMEM((GLA_DV, GLA_DK), F32)],
        input_output_aliases={7: 0},
        compiler_params=_params("parallel", "parallel", "arbitrary"),
        name="gla",
    )(main, main, main, main, g, s0, nw, mix)
    return out, s


def _bucket_thresholds():
    half = N_BUCKETS // 2
    m = half // 2
    e = half - m
    thr = []
    for kk in range(1, e):
        n = m
        while n ** e * m ** kk < m ** e * MAX_DISTANCE ** kk:
            n += 1
        thr.append(n)
    return tuple(thr)


_BUCKET_THR = _bucket_thresholds()


def _t5_bucket(rel):
    half = N_BUCKETS // 2
    max_exact = half // 2
    n = jnp.abs(rel)
    large = jnp.full(rel.shape, max_exact, jnp.int32)
    for thr in _BUCKET_THR:
        large = large + (n >= thr).astype(jnp.int32)
    return jnp.where(rel > 0, half, 0) + jnp.where(n < max_exact, n, large)


def _bias_body(rb_ref, o_ref, *, q_off, k_off, k_step):
    h = pl.program_id(0)
    t = pl.program_id(1)
    shape = o_ref.shape[2:]
    qpos = q_off + lax.broadcasted_iota(jnp.int32, shape, 0)
    kpos = k_off + t * k_step + lax.broadcasted_iota(jnp.int32, shape, 1)
    bucket = _t5_bucket(kpos - qpos)
    acc = jnp.zeros(shape, F32)
    for bkt in range(N_BUCKETS):
        acc = jnp.where(bucket == bkt, rb_ref[bkt, h], acc)
    visible = (kpos >> 6) <= (qpos >> 6)
    o_ref[0, 0] = jnp.where(visible, acc, -jnp.inf)


assert CHUNK == 64


def _bias_table(rel_bias, *, tiles, rows, cols, q_off, k_off, k_step):
    return pl.pallas_call(
        functools.partial(_bias_body, q_off=q_off, k_off=k_off, k_step=k_step),
        grid=(DIFF_HEADS, tiles),
        in_specs=[pl.BlockSpec(memory_space=pltpu.SMEM)],
        out_specs=pl.BlockSpec((1, 1, rows, cols), lambda h, t: (h, t, 0, 0)),
        out_shape=jax.ShapeDtypeStruct((DIFF_HEADS, tiles, rows, cols), F32),
        compiler_params=_params("parallel", "parallel"),
        name="bias_table",
    )(rel_bias)


def _lambda(lam_ref, layer):
    lam = lam_ref[...]
    a = jnp.sum(lam[0:1, :] * lam[1:2, :], axis=-1, keepdims=True)
    b = jnp.sum(lam[2:3, :] * lam[3:4, :], axis=-1, keepdims=True)
    lam_init = 0.8 - 0.6 * math.exp(-0.3 * layer)
    return jnp.exp(a) - jnp.exp(b) + lam_init, lam_init


def _diff_finish(o1, l1, o2, l2, lam, lam_init, nw):
    o = o1 / l1 - lam * (o2 / l2)
    return ((o * _rms_scale(o)) * nw * (1.0 - lam_init)).astype(BF16)


def _attn_prompt_body(q_ref, k_ref, v_ref, bias_ref, rb_ref, lam_ref, nw_ref, mix_in_ref,
                      o_ref, acc_ref, *, layer, blk):
    del mix_in_ref
    h = pl.program_id(0)
    qi = pl.program_id(1)
    qs = (q_ref[...] * (DIFF_D ** -0.5)).astype(BF16)
    qh = (qs[:, :DIFF_D], qs[:, DIFF_D:])
    far_bias = rb_ref[N_BUCKETS // 2 - 1, h]
    acc_ref[...] = jnp.zeros_like(acc_ref)

    def step(kj, bias, carry):
        rows = pl.ds(pl.multiple_of(kj * blk, blk), blk)
        kb = k_ref[rows, :]
        vb = v_ref[rows, :]
        out = []
        for half in range(2):
            m, l = carry[half]
            s = _dot_nt(qh[half], kb[:, half * DIFF_D:(half + 1) * DIFF_D]) + bias
            m_new = jnp.maximum(m, jnp.max(s, axis=-1, keepdims=True))
            alpha = jnp.exp(m - m_new)
            p = jnp.exp(s - m_new)
            l = alpha * l + jnp.sum(p, axis=-1, keepdims=True)
            acc_ref[half] = alpha * acc_ref[half] + _dot(p.astype(BF16), vb)
            out.append((m_new, l))
        return tuple(out)

    init = tuple((jnp.full((blk, 1), -jnp.inf, F32), jnp.zeros((blk, 1), F32)) for _ in range(2))
    carry = lax.fori_loop(0, qi - 1, lambda kj, c: step(kj, far_bias, c), init)
    carry = lax.cond(qi > 0, lambda c: step(qi - 1, bias_ref[0, 1], c), lambda c: c, carry)
    carry = step(qi, bias_ref[0, 0], carry)

    lam, lam_init = _lambda(lam_ref, layer)
    o_ref[...] = _diff_finish(acc_ref[0], carry[0][1], acc_ref[1], carry[1][1], lam, lam_init, nw_ref[...])


def _attn_prompt(main, kvb, bias, rel_bias, lam, nw, mix, *, layer, seq):
    blk = min(ATT_BLOCK, seq)
    assert seq % blk == 0 and blk >= MAX_DISTANCE and blk % CHUNK == 0
    q_col0 = DQ_OFF // (2 * DIFF_D)
    v_col0 = DIFF_QK_W // DIFF_DV
    o_col0 = GLA_V_W // DIFF_DV
    return pl.pallas_call(
        functools.partial(_attn_prompt_body, layer=layer, blk=blk),
        grid=(DIFF_HEADS, seq // blk),
        in_specs=[
            pl.BlockSpec((blk, 2 * DIFF_D), lambda h, i: (i, q_col0 + h)),
            pl.BlockSpec((seq, 2 * DIFF_D), lambda h, i: (0, h)),
            pl.BlockSpec((seq, DIFF_DV), lambda h, i: (0, v_col0 + h)),
            pl.BlockSpec((1, 2, blk, blk), lambda h, i: (h, 0, 0, 0)),
            pl.BlockSpec(memory_space=pltpu.SMEM),
            pl.BlockSpec((4, DIFF_D), lambda h, i: (0, 0)),
            pl.BlockSpec((1, DIFF_DV), lambda h, i: (0, 0)),
            pl.BlockSpec(memory_space=pl.ANY),
        ],
        out_specs=pl.BlockSpec((blk, DIFF_DV), lambda h, i: (i, o_col0 + h)),
        out_shape=jax.ShapeDtypeStruct(mix.shape, mix.dtype),
        scratch_shapes=[pltpu.VMEM((2, blk, DIFF_DV), F32)],
        input_output_aliases={7: 0},
        compiler_params=_params("parallel", "arbitrary"),
        name="attn_prompt",
    )(main, kvb, kvb, bias, rel_bias, lam, nw, mix)


def _attn_sample_body(q_ref, kn_ref, vn_ref, kc_ref, vc_ref, bias_ref, lam_ref, nw_ref, mix_in_ref,
                      o_ref, *, layer, past):
    del mix_in_ref
    qs = (q_ref[...] * (DIFF_D ** -0.5)).astype(BF16)
    kn = kn_ref[...]
    bias_c = bias_ref[0, 0, :, :past]
    bias_n = bias_ref[0, 0, :, past:]
    ps = []
    for half in range(2):
        cols = slice(half * DIFF_D, (half + 1) * DIFF_D)
        sc = _dot_nt(qs[:, cols], kc_ref[:, cols].astype(BF16)) + bias_c
        sn = _dot_nt(qs[:, cols], kn[:, cols]) + bias_n
        m = jnp.maximum(jnp.max(sc, axis=-1, keepdims=True), jnp.max(sn, axis=-1, keepdims=True))
        pc = jnp.exp(sc - m)
        pn = jnp.exp(sn - m)
        inv = 1.0 / (jnp.sum(pc, axis=-1, keepdims=True) + jnp.sum(pn, axis=-1, keepdims=True))
        ps.append((pc * inv, pn * inv))
    lam, lam_init = _lambda(lam_ref, layer)
    wc = (ps[0][0] - lam * ps[1][0]).astype(BF16)
    wn = (ps[0][1] - lam * ps[1][1]).astype(BF16)
    o = _dot(wc, vc_ref[...].astype(BF16)) + _dot(wn, vn_ref[...])
    o_ref[...] = ((o * _rms_scale(o)) * nw_ref[...] * (1.0 - lam_init)).astype(BF16)


def _attn_sample(main, kvb, cache_k, cache_v, bias, lam, nw, mix, *, layer, row_off, batch, seq):
    past = cache_k.shape[2]
    assert row_off % seq == 0
    rb0 = row_off // seq
    q_col0 = DQ_OFF // (2 * DIFF_D)
    v_col0 = DIFF_QK_W // DIFF_DV
    o_col0 = GLA_V_W // DIFF_DV
    return pl.pallas_call(
        functools.partial(_attn_sample_body, layer=layer, past=past),
        grid=(batch, DIFF_HEADS),
        in_specs=[
            pl.BlockSpec((seq, 2 * DIFF_D), lambda b, h: (rb0 + b, q_col0 + h)),
            pl.BlockSpec((seq, 2 * DIFF_D), lambda b, h: (rb0 + b, h)),
            pl.BlockSpec((seq, DIFF_DV), lambda b, h: (rb0 + b, v_col0 + h)),
            pl.BlockSpec((None, None, past, 2 * DIFF_D), lambda b, h: (layer, b, 0, h)),
            pl.BlockSpec((None, None, past, DIFF_DV), lambda b, h: (layer, b, 0, h)),
            pl.BlockSpec((1, 1, seq, past + seq), lambda b, h: (h, 0, 0, 0)),
            pl.BlockSpec((4, DIFF_D), lambda b, h: (0, 0)),
            pl.BlockSpec((1, DIFF_DV), lambda b, h: (0, 0)),
            pl.BlockSpec(memory_space=pl.ANY),
        ],
        out_specs=pl.BlockSpec((seq, DIFF_DV), lambda b, h: (rb0 + b, o_col0 + h)),
        out_shape=jax.ShapeDtypeStruct(mix.shape, mix.dtype),
        input_output_aliases={8: 0},
        compiler_params=_params("parallel", "arbitrary"),
        name="attn_sample",
    )(main, kvb, kvb, cache_k, cache_v, bias, lam, nw, mix)


def _outproj_body(o_ref, w_ref, x_ref, y_ref):
    y_ref[...] = x_ref[...] + _dot(o_ref[...], w_ref[...])


def _outproj(mix, w, x):
    n, d = x.shape
    tm = _row_tile(n)
    return pl.pallas_call(
        _outproj_body,
        grid=(n // tm,),
        in_specs=[
            pl.BlockSpec((tm, MIX_WIDTH), lambda i: (i, 0)),
            pl.BlockSpec((MIX_WIDTH, d), lambda i: (0, 0)),
            pl.BlockSpec((tm, d), lambda i: (i, 0)),
        ],
        out_specs=pl.BlockSpec((tm, d), lambda i: (i, 0)),
        out_shape=jax.ShapeDtypeStruct((n, d), F32),
        compiler_params=_params("parallel"),
        name="outproj",
    )(mix, w, x)


def kernel(x_prompt, x_sample, cache_k, cache_v, state_gla, ffn1_norm, ffn1_w_gate, ffn1_w_up, ffn1_w_down,
           mix_norm, w_in, gla_w_gk, gla_b_gk, gla_norm, diff_lambda, diff_norm, w_out,
           ffn2_norm, ffn2_w_gate, ffn2_w_up, ffn2_w_down, rel_bias, final_norm):
    pb, ps, d = x_prompt.shape
    sb, ss, _ = x_sample.shape
    depth = w_in.shape[0]
    past = cache_k.shape[2]
    assert pb == 1
    n_p, n_s = pb * ps, sb * ss
    x = jnp.concatenate([x_prompt.reshape(n_p, d), x_sample.reshape(n_s, d)], axis=0)

    ck = cache_k.reshape(depth, sb, past, DIFF_QK_W)
    cv = cache_v.reshape(depth, sb, past, DIFF_V_W)
    blk = min(ATT_BLOCK, ps)
    bias_p = _bias_table(rel_bias, tiles=2, rows=blk, cols=blk, q_off=0, k_off=0, k_step=-blk)
    bias_s = _bias_table(rel_bias, tiles=1, rows=ss, cols=past + ss, q_off=past, k_off=0, k_step=0)
    zero_state = jnp.zeros((pb, GLA_HEADS, GLA_DK, GLA_DV), F32)
    row = lambda a: a.reshape(1, -1)

    k_rows, v_rows, p_states, s_states = [], [], [], []
    for l in range(depth):
        w_main = jnp.concatenate([w_in[l, :, :GZ_OFF], w_in[l, :, GZ_OFF + GLA_GK_RANK:]], axis=1).astype(BF16)
        w_gz = jnp.pad(w_in[l, :, GZ_OFF:GZ_OFF + GLA_GK_RANK], ((0, 0), (0, LANE - GLA_GK_RANK))).astype(BF16)
        w_gk = jnp.pad(gla_w_gk[l], ((0, LANE - GLA_GK_RANK), (0, 0))).astype(BF16)

        x = _ffn(x, row(ffn1_norm[l]), ffn1_w_gate[l].astype(BF16), ffn1_w_up[l].astype(BF16),
                 ffn1_w_down[l].astype(BF16), row(final_norm), False)
        main, kf, vf, kvb, g = _inproj(x, row(mix_norm[l]), w_main, w_gz, w_gk, row(gla_b_gk[l]))

        mix = jnp.zeros((n_p + n_s, MIX_WIDTH), BF16)
        mix, sp = _gla(main, g, zero_state, row(gla_norm[l]), mix,
                       row_off=0, batch=pb, seq=ps, chunk=CHUNK, rows=min(GLA_ROWS, ps))
        mix, s_s = _gla(main, g, state_gla[l], row(gla_norm[l]), mix,
                        row_off=n_p, batch=sb, seq=ss, chunk=ss, rows=ss)
        mix = _attn_prompt(main, kvb, bias_p, rel_bias, diff_lambda[l], row(diff_norm[l]), mix,
                           layer=l, seq=ps)
        mix = _attn_sample(main, kvb, ck, cv, bias_s, diff_lambda[l], row(diff_norm[l]), mix,
                           layer=l, row_off=n_p, batch=sb, seq=ss)

        x = _outproj(mix, w_out[l].astype(BF16), x)
        x = _ffn(x, row(ffn2_norm[l]), ffn2_w_gate[l].astype(BF16), ffn2_w_up[l].astype(BF16),
                 ffn2_w_down[l].astype(BF16), row(final_norm), l == depth - 1)
        k_rows.append(kf)
        v_rows.append(vf)
        p_states.append(sp)
        s_states.append(s_s)

    def rows_of(parts, lo, hi, shape):
        return jnp.stack([p[lo:hi].reshape(shape) for p in parts])

    return (x[:n_p].reshape(pb, ps, d),
            x[n_p:].reshape(sb, ss, d),
            rows_of(k_rows, 0, n_p, (pb, ps, DIFF_HEADS, 2 * DIFF_D)),
            rows_of(v_rows, 0, n_p, (pb, ps, DIFF_HEADS, DIFF_DV)),
            jnp.stack(p_states),
            rows_of(k_rows, n_p, n_p + n_s, (sb, ss, DIFF_HEADS, 2 * DIFF_D)),
            rows_of(v_rows, n_p, n_p + n_s, (sb, ss, DIFF_HEADS, DIFF_DV)),
            jnp.stack(s_states))
```

```python
import functools
import math

import jax
import jax.numpy as jnp
from jax import lax
from jax.experimental import pallas as pl
from jax.experimental.pallas import tpu as pltpu

F32 = jnp.float32
BF16 = jnp.bfloat16

EPS = 1e-6
CHUNK = 64
GLA_HEADS = 4
GLA_DK = 128
GLA_DV = 256
GLA_GK_RANK = 16
GLA_GATE_NORMALIZER = 16.0
DIFF_HEADS = 4
DIFF_D = 128
DIFF_DV = 2 * DIFF_D
N_BUCKETS = 32
MAX_DISTANCE = 128

GLA_K_W = GLA_HEADS * GLA_DK
GLA_V_W = GLA_HEADS * GLA_DV
DIFF_QK_W = DIFF_HEADS * 2 * DIFF_D
DIFF_V_W = DIFF_HEADS * DIFF_DV
MIX_WIDTH = GLA_V_W + DIFF_V_W
GZ_OFF = 2 * GLA_K_W + 2 * GLA_V_W
MAIN_W = GZ_OFF + DIFF_QK_W
DQ_OFF = GZ_OFF

LANE = 128
VMEM_LIMIT = 56 * 1024 * 1024

ROW_TILE = 512
FF_TILE = 512
IN_TILE = 512
ATT_BLOCK = 512
GLA_ROWS = 256


def _dot(a, b):
    return jnp.dot(a, b, preferred_element_type=F32)


def _dot_nt(a, b):
    return lax.dot_general(a, b, (((1,), (1,)), ((), ())), preferred_element_type=F32)


def _dot_tn(a, b):
    return lax.dot_general(a, b, (((0,), (0,)), ((), ())), preferred_element_type=F32)


def _params(*sem):
    return pltpu.CompilerParams(dimension_semantics=sem, vmem_limit_bytes=VMEM_LIMIT)


def _row_tile(n):
    t = ROW_TILE
    while n % t:
        t //= 2
    return t


def _rms_scale(x):
    return lax.rsqrt(jnp.mean(x * x, axis=-1, keepdims=True) + EPS)


def _ffn_body(x_ref, nw_ref, wg_ref, wu_ref, wd_ref, fw_ref, o_ref, xn_ref, acc_ref, *, final):
    j = pl.program_id(1)

    @pl.when(j == 0)
    def _():
        x = x_ref[...]
        xn_ref[...] = ((x * _rms_scale(x)) * nw_ref[...]).astype(BF16)
        acc_ref[...] = jnp.zeros_like(acc_ref)

    xn = xn_ref[...]
    h = _dot(xn, wg_ref[...])
    u = _dot(xn, wu_ref[...])
    a = (h * jax.nn.sigmoid(h) * u).astype(BF16)
    acc_ref[...] += _dot(a, wd_ref[...])

    @pl.when(j == pl.num_programs(1) - 1)
    def _():
        y = x_ref[...] + 0.5 * acc_ref[...]
        if final:
            y = (y * _rms_scale(y)) * fw_ref[...]
        o_ref[...] = y


def _ffn(x, nw, wg, wu, wd, fw, final):
    n, d = x.shape
    f = wg.shape[1]
    tm = _row_tile(n)
    return pl.pallas_call(
        functools.partial(_ffn_body, final=final),
        grid=(n // tm, f // FF_TILE),
        in_specs=[
            pl.BlockSpec((tm, d), lambda i, j: (i, 0)),
            pl.BlockSpec((1, d), lambda i, j: (0, 0)),
            pl.BlockSpec((d, FF_TILE), lambda i, j: (0, j)),
            pl.BlockSpec((d, FF_TILE), lambda i, j: (0, j)),
            pl.BlockSpec((FF_TILE, d), lambda i, j: (j, 0)),
            pl.BlockSpec((1, d), lambda i, j: (0, 0)),
        ],
        out_specs=pl.BlockSpec((tm, d), lambda i, j: (i, 0)),
        out_shape=jax.ShapeDtypeStruct((n, d), F32),
        scratch_shapes=[pltpu.VMEM((tm, d), BF16), pltpu.VMEM((tm, d), F32)],
        compiler_params=_params("parallel", "arbitrary"),
        name="ffn",
    )(x, nw, wg, wu, wd, fw)


N_MAIN_BLK = MAIN_W // IN_TILE
N_K_BLK = DIFF_QK_W // IN_TILE
N_V_BLK = DIFF_V_W // IN_TILE


def _inproj_body(x_ref, nw_ref, w_ref, wgz_ref, wgk_ref, bgk_ref,
                 main_ref, k_ref, v_ref, kvb_ref, g_ref, xn_ref):
    j = pl.program_id(1)

    @pl.when(j == 0)
    def _():
        x = x_ref[...]
        xn = ((x * _rms_scale(x)) * nw_ref[...]).astype(BF16)
        xn_ref[...] = xn
        gz = _dot(xn, wgz_ref[...])
        z = _dot(gz.astype(BF16), wgk_ref[...]) + bgk_ref[...]
        g_ref[...] = (jnp.minimum(z, 0.0) - jnp.log1p(jnp.exp(-jnp.abs(z)))) * (1.0 / GLA_GATE_NORMALIZER)

    y = _dot(xn_ref[...], w_ref[...])

    @pl.when(j < N_MAIN_BLK)
    def _():
        main_ref[...] = y

    @pl.when(jnp.logical_and(j >= N_MAIN_BLK, j < N_MAIN_BLK + N_K_BLK))
    def _():
        k_ref[...] = y
        kvb_ref[...] = y.astype(BF16)

    @pl.when(j >= N_MAIN_BLK + N_K_BLK)
    def _():
        v_ref[...] = y
        kvb_ref[...] = y.astype(BF16)


def _inproj(x, nw, w_main, w_gz, w_gk, b_gk):
    n, d = x.shape
    tm = _row_tile(n)
    nj = N_MAIN_BLK + N_K_BLK + N_V_BLK
    k0 = N_MAIN_BLK
    v0 = N_MAIN_BLK + N_K_BLK
    return pl.pallas_call(
        _inproj_body,
        grid=(n // tm, nj),
        in_specs=[
            pl.BlockSpec((tm, d), lambda i, j: (i, 0)),
            pl.BlockSpec((1, d), lambda i, j: (0, 0)),
            pl.BlockSpec((d, IN_TILE), lambda i, j: (0, j)),
            pl.BlockSpec((d, LANE), lambda i, j: (0, 0)),
            pl.BlockSpec((LANE, GLA_K_W), lambda i, j: (0, 0)),
            pl.BlockSpec((1, GLA_K_W), lambda i, j: (0, 0)),
        ],
        out_specs=[
            pl.BlockSpec((tm, IN_TILE), lambda i, j: (i, jnp.minimum(j, k0 - 1))),
            pl.BlockSpec((tm, IN_TILE), lambda i, j: (i, jnp.clip(j - k0, 0, N_K_BLK - 1))),
            pl.BlockSpec((tm, IN_TILE), lambda i, j: (i, jnp.clip(j - v0, 0, N_V_BLK - 1))),
            pl.BlockSpec((tm, IN_TILE), lambda i, j: (i, jnp.clip(j - k0, 0, N_K_BLK + N_V_BLK - 1))),
            pl.BlockSpec((tm, GLA_K_W), lambda i, j: (i, 0)),
        ],
        out_shape=[
            jax.ShapeDtypeStruct((n, MAIN_W), F32),
            jax.ShapeDtypeStruct((n, DIFF_QK_W), F32),
            jax.ShapeDtypeStruct((n, DIFF_V_W), F32),
            jax.ShapeDtypeStruct((n, DIFF_QK_W + DIFF_V_W), BF16),
            jax.ShapeDtypeStruct((n, GLA_K_W), F32),
        ],
        scratch_shapes=[pltpu.VMEM((tm, d), BF16)],
        compiler_params=_params("parallel", "arbitrary"),
        name="inproj",
    )(x, nw, w_main, w_gz, w_gk, b_gk)


def _split3(a):
    hi = a.astype(BF16)
    r = a - hi.astype(F32)
    mid = r.astype(BF16)
    lo = (r - mid.astype(F32)).astype(BF16)
    return hi, mid, lo


def _gla_body(q_ref, k_ref, v_ref, r_ref, g_ref, s0_ref, nw_ref, mix_in_ref, o_ref, s_ref, st_ref, *, chunk):
    del mix_in_ref
    t = pl.program_id(2)
    rows = q_ref.shape[0]

    @pl.when(t == 0)
    def _():
        st_ref[...] = s0_ref[0, 0].T

    ri = lax.broadcasted_iota(jnp.int32, (chunk, chunk), 0)
    ci = lax.broadcasted_iota(jnp.int32, (chunk, chunk), 1)
    causal = ci <= ri
    tri = causal.astype(BF16)

    for c in range(rows // chunk):
        sl = pl.ds(c * chunk, chunk)
        q = q_ref[sl, :] * (GLA_DK ** -0.5)
        k = k_ref[sl, :]
        v = v_ref[sl, :].astype(BF16)
        g = g_ref[sl, :]
        g_hi, g_mid, g_lo = _split3(g)
        b = _dot(tri, g_hi) + _dot(tri, g_mid) + _dot(tri, g_lo)
        b_last = b[chunk - 1:chunk, :]
        qe = (q * jnp.exp(b)).astype(BF16)
        ke = (k * jnp.exp(-b)).astype(BF16)
        kd = (k * jnp.exp(b_last - b)).astype(BF16)
        a = jnp.where(causal, _dot_nt(qe, ke), 0.0).astype(BF16)
        st = st_ref[...]
        o = _dot_nt(qe, st.astype(BF16)) + _dot(a, v)
        st_ref[...] = st * jnp.exp(b_last) + _dot_tn(v, kd)
        o = (o * _rms_scale(o)) * nw_ref[...]
        r = r_ref[sl, :]
        o_ref[sl, :] = (o * (r * jax.nn.sigmoid(r))).astype(BF16)

    @pl.when(t == pl.num_programs(2) - 1)
    def _():
        s_ref[0, 0] = st_ref[...].T


def _gla(main, g, s0, nw, mix, *, row_off, batch, seq, chunk, rows):
    assert seq % rows == 0 and rows % chunk == 0 and row_off % rows == 0
    nt = seq // rows
    rb0 = row_off // rows
    kq = GLA_DK // GLA_DK
    del kq

    def rowblk(b, h, t):
        return rb0 + b * nt + t

    v_col0 = 2 * GLA_K_W // GLA_DV
    r_col0 = (2 * GLA_K_W + GLA_V_W) // GLA_DV
    out, s = pl.pallas_call(
        functools.partial(_gla_body, chunk=chunk),
        grid=(batch, GLA_HEADS, nt),
        in_specs=[
            pl.BlockSpec((rows, GLA_DK), lambda b, h, t: (rowblk(b, h, t), h)),
            pl.BlockSpec((rows, GLA_DK), lambda b, h, t: (rowblk(b, h, t), GLA_HEADS + h)),
            pl.BlockSpec((rows, GLA_DV), lambda b, h, t: (rowblk(b, h, t), v_col0 + h)),
            pl.BlockSpec((rows, GLA_DV), lambda b, h, t: (rowblk(b, h, t), r_col0 + h)),
            pl.BlockSpec((rows, GLA_DK), lambda b, h, t: (rowblk(b, h, t), h)),
            pl.BlockSpec((1, 1, GLA_DK, GLA_DV), lambda b, h, t: (b, h, 0, 0)),
            pl.BlockSpec((1, GLA_DV), lambda b, h, t: (0, 0)),
            pl.BlockSpec(memory_space=pl.ANY),
        ],
        out_specs=[
            pl.BlockSpec((rows, GLA_DV), lambda b, h, t: (rowblk(b, h, t), h)),
            pl.BlockSpec((1, 1, GLA_DK, GLA_DV), lambda b, h, t: (b, h, 0, 0)),
        ],
        out_shape=[
            jax.ShapeDtypeStruct(mix.shape, mix.dtype),
            jax.ShapeDtypeStruct((batch, GLA_HEADS, GLA_DK, GLA_DV), F32),
        ],
        scratch_shapes=[pltpu.VMEM((GLA_DV, GLA_DK), F32)],
        input_output_aliases={7: 0},
        compiler_params=_params("parallel", "parallel", "arbitrary"),
        name="gla",
    )(main, main, main, main, g, s0, nw, mix)
    return out, s


def _bucket_thresholds():
    half = N_BUCKETS // 2
    m = half // 2
    e = half - m
    thr = []
    for kk in range(1, e):
        n = m
        while n ** e * m ** kk < m ** e * MAX_DISTANCE ** kk:
            n += 1
        thr.append(n)
    return tuple(thr)


_BUCKET_THR = _bucket_thresholds()


def _t5_bucket(rel):
    half = N_BUCKETS // 2
    max_exact = half // 2
    n = jnp.abs(rel)
    large = jnp.full(rel.shape, max_exact, jnp.int32)
    for thr in _BUCKET_THR:
        large = large + (n >= thr).astype(jnp.int32)
    return jnp.where(rel > 0, half, 0) + jnp.where(n < max_exact, n, large)


def _bias_body(rb_ref, o_ref, *, q_off, k_off, k_step):
    h = pl.program_id(0)
    t = pl.program_id(1)
    shape = o_ref.shape[2:]
    qpos = q_off + lax.broadcasted_iota(jnp.int32, shape, 0)
    kpos = k_off + t * k_step + lax.broadcasted_iota(jnp.int32, shape, 1)
    bucket = _t5_bucket(kpos - qpos)
    acc = jnp.zeros(shape, F32)
    for bkt in range(N_BUCKETS):
        acc = jnp.where(bucket == bkt, rb_ref[bkt, h], acc)
    visible = (kpos >> 6) <= (qpos >> 6)
    o_ref[0, 0] = jnp.where(visible, acc, -jnp.inf)


assert CHUNK == 64


def _bias_table(rel_bias, *, tiles, rows, cols, q_off, k_off, k_step):
    return pl.pallas_call(
        functools.partial(_bias_body, q_off=q_off, k_off=k_off, k_step=k_step),
        grid=(DIFF_HEADS, tiles),
        in_specs=[pl.BlockSpec(memory_space=pltpu.SMEM)],
        out_specs=pl.BlockSpec((1, 1, rows, cols), lambda h, t: (h, t, 0, 0)),
        out_shape=jax.ShapeDtypeStruct((DIFF_HEADS, tiles, rows, cols), F32),
        compiler_params=_params("parallel", "parallel"),
        name="bias_table",
    )(rel_bias)


def _lambda(lam_ref, layer):
    lam = lam_ref[...]
    a = jnp.sum(lam[0:1, :] * lam[1:2, :], axis=-1, keepdims=True)
    b = jnp.sum(lam[2:3, :] * lam[3:4, :], axis=-1, keepdims=True)
    lam_init = 0.8 - 0.6 * math.exp(-0.3 * layer)
    return jnp.exp(a) - jnp.exp(b) + lam_init, lam_init


def _diff_finish(o1, l1, o2, l2, lam, lam_init, nw):
    o = o1 / l1 - lam * (o2 / l2)
    return ((o * _rms_scale(o)) * nw * (1.0 - lam_init)).astype(BF16)


def _attn_prompt_body(q_ref, k_ref, v_ref, bias_ref, rb_ref, lam_ref, nw_ref, mix_in_ref,
                      o_ref, acc_ref, *, layer, blk):
    del mix_in_ref
    h = pl.program_id(0)
    qi = pl.program_id(1)
    qs = (q_ref[...] * (DIFF_D ** -0.5)).astype(BF16)
    qh = (qs[:, :DIFF_D], qs[:, DIFF_D:])
    far_bias = rb_ref[N_BUCKETS // 2 - 1, h]
    acc_ref[...] = jnp.zeros_like(acc_ref)

    def step(kj, bias, carry):
        rows = pl.ds(pl.multiple_of(kj * blk, blk), blk)
        kb = k_ref[rows, :]
        vb = v_ref[rows, :]
        out = []
        for half in range(2):
            m, l = carry[half]
            s = _dot_nt(qh[half], kb[:, half * DIFF_D:(half + 1) * DIFF_D]) + bias
            m_new = jnp.maximum(m, jnp.max(s, axis=-1, keepdims=True))
            alpha = jnp.exp(m - m_new)
            p = jnp.exp(s - m_new)
            l = alpha * l + jnp.sum(p, axis=-1, keepdims=True)
            acc_ref[half] = alpha * acc_ref[half] + _dot(p.astype(BF16), vb)
            out.append((m_new, l))
        return tuple(out)

    init = tuple((jnp.full((blk, 1), -jnp.inf, F32), jnp.zeros((blk, 1), F32)) for _ in range(2))
    carry = lax.fori_loop(0, qi - 1, lambda kj, c: step(kj, far_bias, c), init)
    carry = lax.cond(qi > 0, lambda c: step(qi - 1, bias_ref[0, 1], c), lambda c: c, carry)
    carry = step(qi, bias_ref[0, 0], carry)

    lam, lam_init = _lambda(lam_ref, layer)
    o_ref[...] = _diff_finish(acc_ref[0], carry[0][1], acc_ref[1], carry[1][1], lam, lam_init, nw_ref[...])


def _attn_prompt(main, kvb, bias, rel_bias, lam, nw, mix, *, layer, seq):
    blk = min(ATT_BLOCK, seq)
    assert seq % blk == 0 and blk >= MAX_DISTANCE and blk % CHUNK == 0
    q_col0 = DQ_OFF // (2 * DIFF_D)
    v_col0 = DIFF_QK_W // DIFF_DV
    o_col0 = GLA_V_W // DIFF_DV
    return pl.pallas_call(
        functools.partial(_attn_prompt_body, layer=layer, blk=blk),
        grid=(DIFF_HEADS, seq // blk),
        in_specs=[
            pl.BlockSpec((blk, 2 * DIFF_D), lambda h, i: (i, q_col0 + h)),
            pl.BlockSpec((seq, 2 * DIFF_D), lambda h, i: (0, h)),
            pl.BlockSpec((seq, DIFF_DV), lambda h, i: (0, v_col0 + h)),
            pl.BlockSpec((1, 2, blk, blk), lambda h, i: (h, 0, 0, 0)),
            pl.BlockSpec(memory_space=pltpu.SMEM),
            pl.BlockSpec((4, DIFF_D), lambda h, i: (0, 0)),
            pl.BlockSpec((1, DIFF_DV), lambda h, i: (0, 0)),
            pl.BlockSpec(memory_space=pl.ANY),
        ],
        out_specs=pl.BlockSpec((blk, DIFF_DV), lambda h, i: (i, o_col0 + h)),
        out_shape=jax.ShapeDtypeStruct(mix.shape, mix.dtype),
        scratch_shapes=[pltpu.VMEM((2, blk, DIFF_DV), F32)],
        input_output_aliases={7: 0},
        compiler_params=_params("parallel", "arbitrary"),
        name="attn_prompt",
    )(main, kvb, kvb, bias, rel_bias, lam, nw, mix)


def _attn_sample_body(q_ref, kn_ref, vn_ref, kc_hbm, vc_hbm, bias_ref, lam_ref, nw_ref, mix_in_ref,
                      o_ref, kbuf, vbuf, sem, *, layer, past):
    del mix_in_ref
    b = pl.program_id(0)
    h = pl.program_id(1)
    n_h = pl.num_programs(1)
    step = b * n_h + h
    slot = step % 2

    def copies(cb, ch, cslot):
        return (pltpu.make_async_copy(kc_hbm.at[layer, cb, :, ch, :], kbuf.at[cslot], sem.at[0, cslot]),
                pltpu.make_async_copy(vc_hbm.at[layer, cb, :, ch, :], vbuf.at[cslot], sem.at[1, cslot]))

    @pl.when(step == 0)
    def _():
        for c in copies(b, h, slot):
            c.start()

    @pl.when(step + 1 < pl.num_programs(0) * n_h)
    def _():
        wrap = h + 1 == n_h
        for c in copies(jnp.where(wrap, b + 1, b), jnp.where(wrap, 0, h + 1), 1 - slot):
            c.start()

    qs = (q_ref[...] * (DIFF_D ** -0.5)).astype(BF16)
    kn = kn_ref[...]
    bias_c = bias_ref[0, 0, :, :past]
    bias_n = bias_ref[0, 0, :, past:]
    for c in copies(b, h, slot):
        c.wait()
    ps = []
    for half in range(2):
        cols = slice(half * DIFF_D, (half + 1) * DIFF_D)
        sc = _dot_nt(qs[:, cols], kbuf[slot, :, cols].astype(BF16)) + bias_c
        sn = _dot_nt(qs[:, cols], kn[:, cols]) + bias_n
        m = jnp.maximum(jnp.max(sc, axis=-1, keepdims=True), jnp.max(sn, axis=-1, keepdims=True))
        pc = jnp.exp(sc - m)
        pn = jnp.exp(sn - m)
        inv = 1.0 / (jnp.sum(pc, axis=-1, keepdims=True) + jnp.sum(pn, axis=-1, keepdims=True))
        ps.append((pc * inv, pn * inv))
    lam, lam_init = _lambda(lam_ref, layer)
    wc = (ps[0][0] - lam * ps[1][0]).astype(BF16)
    wn = (ps[0][1] - lam * ps[1][1]).astype(BF16)
    o = _dot(wc, vbuf[slot].astype(BF16)) + _dot(wn, vn_ref[...])
    o_ref[...] = ((o * _rms_scale(o)) * nw_ref[...] * (1.0 - lam_init)).astype(BF16)


def _attn_sample(main, kvb, cache_k, cache_v, bias, lam, nw, mix, *, layer, row_off, batch, seq):
    past = cache_k.shape[2]
    assert row_off % seq == 0
    rb0 = row_off // seq
    q_col0 = DQ_OFF // (2 * DIFF_D)
    v_col0 = DIFF_QK_W // DIFF_DV
    o_col0 = GLA_V_W // DIFF_DV
    return pl.pallas_call(
        functools.partial(_attn_sample_body, layer=layer, past=past),
        grid=(batch, DIFF_HEADS),
        in_specs=[
            pl.BlockSpec((seq, 2 * DIFF_D), lambda b, h: (rb0 + b, q_col0 + h)),
            pl.BlockSpec((seq, 2 * DIFF_D), lambda b, h: (rb0 + b, h)),
            pl.BlockSpec((seq, DIFF_DV), lambda b, h: (rb0 + b, v_col0 + h)),
            pl.BlockSpec(memory_space=pl.ANY),
            pl.BlockSpec(memory_space=pl.ANY),
            pl.BlockSpec((1, 1, seq, past + seq), lambda b, h: (h, 0, 0, 0)),
            pl.BlockSpec((4, DIFF_D), lambda b, h: (0, 0)),
            pl.BlockSpec((1, DIFF_DV), lambda b, h: (0, 0)),
            pl.BlockSpec(memory_space=pl.ANY),
        ],
        out_specs=pl.BlockSpec((seq, DIFF_DV), lambda b, h: (rb0 + b, o_col0 + h)),
        out_shape=jax.ShapeDtypeStruct(mix.shape, mix.dtype),
        scratch_shapes=[
            pltpu.VMEM((2, past, 2 * DIFF_D), F32),
            pltpu.VMEM((2, past, DIFF_DV), F32),
            pltpu.SemaphoreType.DMA((2, 2)),
        ],
        input_output_aliases={8: 0},
        compiler_params=_params("arbitrary", "arbitrary"),
        name="attn_sample",
    )(main, kvb, kvb, cache_k, cache_v, bias, lam, nw, mix)


def _outproj_body(o_ref, w_ref, x_ref, y_ref):
    y_ref[...] = x_ref[...] + _dot(o_ref[...], w_ref[...])


def _outproj(mix, w, x):
    n, d = x.shape
    tm = _row_tile(n)
    return pl.pallas_call(
        _outproj_body,
        grid=(n // tm,),
        in_specs=[
            pl.BlockSpec((tm, MIX_WIDTH), lambda i: (i, 0)),
            pl.BlockSpec((MIX_WIDTH, d), lambda i: (0, 0)),
            pl.BlockSpec((tm, d), lambda i: (i, 0)),
        ],
        out_specs=pl.BlockSpec((tm, d), lambda i: (i, 0)),
        out_shape=jax.ShapeDtypeStruct((n, d), F32),
        compiler_params=_params("parallel"),
        name="outproj",
    )(mix, w, x)


def kernel(x_prompt, x_sample, cache_k, cache_v, state_gla, ffn1_norm, ffn1_w_gate, ffn1_w_up, ffn1_w_down,
           mix_norm, w_in, gla_w_gk, gla_b_gk, gla_norm, diff_lambda, diff_norm, w_out,
           ffn2_norm, ffn2_w_gate, ffn2_w_up, ffn2_w_down, rel_bias, final_norm):
    pb, ps, d = x_prompt.shape
    sb, ss, _ = x_sample.shape
    depth = w_in.shape[0]
    past = cache_k.shape[2]
    assert pb == 1
    n_p, n_s = pb * ps, sb * ss
    x = jnp.concatenate([x_prompt.reshape(n_p, d), x_sample.reshape(n_s, d)], axis=0)

    blk = min(ATT_BLOCK, ps)
    bias_p = _bias_table(rel_bias, tiles=2, rows=blk, cols=blk, q_off=0, k_off=0, k_step=-blk)
    bias_s = _bias_table(rel_bias, tiles=1, rows=ss, cols=past + ss, q_off=past, k_off=0, k_step=0)
    zero_state = jnp.zeros((pb, GLA_HEADS, GLA_DK, GLA_DV), F32)
    row = lambda a: a.reshape(1, -1)

    k_rows, v_rows, p_states, s_states = [], [], [], []
    for l in range(depth):
        w_main = jnp.concatenate([w_in[l, :, :GZ_OFF], w_in[l, :, GZ_OFF + GLA_GK_RANK:]], axis=1).astype(BF16)
        w_gz = jnp.pad(w_in[l, :, GZ_OFF:GZ_OFF + GLA_GK_RANK], ((0, 0), (0, LANE - GLA_GK_RANK))).astype(BF16)
        w_gk = jnp.pad(gla_w_gk[l], ((0, LANE - GLA_GK_RANK), (0, 0))).astype(BF16)

        x = _ffn(x, row(ffn1_norm[l]), ffn1_w_gate[l].astype(BF16), ffn1_w_up[l].astype(BF16),
                 ffn1_w_down[l].astype(BF16), row(final_norm), False)
        main, kf, vf, kvb, g = _inproj(x, row(mix_norm[l]), w_main, w_gz, w_gk, row(gla_b_gk[l]))

        mix = jnp.zeros((n_p + n_s, MIX_WIDTH), BF16)
        mix, sp = _gla(main, g, zero_state, row(gla_norm[l]), mix,
                       row_off=0, batch=pb, seq=ps, chunk=CHUNK, rows=min(GLA_ROWS, ps))
        mix, s_s = _gla(main, g, state_gla[l], row(gla_norm[l]), mix,
                        row_off=n_p, batch=sb, seq=ss, chunk=ss, rows=ss)
        mix = _attn_prompt(main, kvb, bias_p, rel_bias, diff_lambda[l], row(diff_norm[l]), mix,
                           layer=l, seq=ps)
        mix = _attn_sample(main, kvb, cache_k, cache_v, bias_s, diff_lambda[l], row(diff_norm[l]), mix,
                           layer=l, row_off=n_p, batch=sb, seq=ss)

        x = _outproj(mix, w_out[l].astype(BF16), x)
        x = _ffn(x, row(ffn2_norm[l]), ffn2_w_gate[l].astype(BF16), ffn2_w_up[l].astype(BF16),
                 ffn2_w_down[l].astype(BF16), row(final_norm), l == depth - 1)
        k_rows.append(kf)
        v_rows.append(vf)
        p_states.append(sp)
        s_states.append(s_s)

    def rows_of(parts, lo, hi, shape):
        return jnp.stack([p[lo:hi].reshape(shape) for p in parts])

    return (x[:n_p].reshape(pb, ps, d),
            x[n_p:].reshape(sb, ss, d),
            rows_of(k_rows, 0, n_p, (pb, ps, DIFF_HEADS, 2 * DIFF_D)),
            rows_of(v_rows, 0, n_p, (pb, ps, DIFF_HEADS, DIFF_DV)),
            jnp.stack(p_states),
            rows_of(k_rows, n_p, n_p + n_s, (sb, ss, DIFF_HEADS, 2 * DIFF_D)),
            rows_of(v_rows, n_p, n_p + n_s, (sb, ss, DIFF_HEADS, DIFF_DV)),
            jnp.stack(s_states))
```

```python
import functools
import math

import jax
import jax.numpy as jnp
from jax import lax
from jax.experimental import pallas as pl
from jax.experimental.pallas import tpu as pltpu

F32 = jnp.float32
BF16 = jnp.bfloat16

EPS = 1e-6
CHUNK = 64
GLA_HEADS = 4
GLA_DK = 128
GLA_DV = 256
GLA_GK_RANK = 16
GLA_GATE_NORMALIZER = 16.0
DIFF_HEADS = 4
DIFF_D = 128
DIFF_DV = 2 * DIFF_D
N_BUCKETS = 32
MAX_DISTANCE = 128

GLA_K_W = GLA_HEADS * GLA_DK
GLA_V_W = GLA_HEADS * GLA_DV
DIFF_QK_W = DIFF_HEADS * 2 * DIFF_D
DIFF_V_W = DIFF_HEADS * DIFF_DV
MIX_WIDTH = GLA_V_W + DIFF_V_W
GZ_OFF = 2 * GLA_K_W + 2 * GLA_V_W
MAIN_W = GZ_OFF + DIFF_QK_W
DQ_OFF = GZ_OFF

LANE = 128
LOG2E = math.log2(math.e)
SUB = 64
VMEM_LIMIT = 56 * 1024 * 1024

ROW_TILE = 512
FF_TILE = 512
IN_TILE = 512
GLA_ROWS = 256


def _dot(a, b):
    return jnp.dot(a, b, preferred_element_type=F32)


def _dot_nt(a, b):
    return lax.dot_general(a, b, (((1,), (1,)), ((), ())), preferred_element_type=F32)


def _dot_tn(a, b):
    return lax.dot_general(a, b, (((0,), (0,)), ((), ())), preferred_element_type=F32)


def _params(*sem):
    return pltpu.CompilerParams(dimension_semantics=sem, vmem_limit_bytes=VMEM_LIMIT)


def _row_tile(n):
    t = ROW_TILE
    while n % t:
        t //= 2
    return t


def _rms_scale(x):
    return lax.rsqrt(jnp.mean(x * x, axis=-1, keepdims=True) + EPS)


CAST_ROWS = 256


def _cast_body(w_ref, o_ref):
    o_ref[...] = w_ref[...].astype(BF16)


def _cast_bf16(w):
    depth, r, c = w.shape
    spec = pl.BlockSpec((1, CAST_ROWS, c), lambda l, i: (l, i, 0))
    return pl.pallas_call(
        _cast_body,
        grid=(depth, r // CAST_ROWS),
        in_specs=[spec],
        out_specs=spec,
        out_shape=jax.ShapeDtypeStruct(w.shape, BF16),
        compiler_params=_params("parallel", "parallel"),
        name="cast_bf16",
    )(w)


def _cast_w_in_body(w_ref, main_ref, gz_ref):
    w = w_ref[0]
    main_ref[0, :, :GZ_OFF] = w[:, :GZ_OFF].astype(BF16)
    main_ref[0, :, GZ_OFF:] = w[:, GZ_OFF + GLA_GK_RANK:].astype(BF16)
    gz = w[:, GZ_OFF:GZ_OFF + GLA_GK_RANK]
    gz_ref[0] = jnp.concatenate([gz, jnp.zeros((gz.shape[0], LANE - GLA_GK_RANK), F32)], axis=1).astype(BF16)


def _cast_w_in(w_in):
    depth, r, c = w_in.shape
    wide = c - GLA_GK_RANK
    return pl.pallas_call(
        _cast_w_in_body,
        grid=(depth, r // CAST_ROWS),
        in_specs=[pl.BlockSpec((1, CAST_ROWS, c), lambda l, i: (l, i, 0))],
        out_specs=[pl.BlockSpec((1, CAST_ROWS, wide), lambda l, i: (l, i, 0)),
                   pl.BlockSpec((1, CAST_ROWS, LANE), lambda l, i: (l, i, 0))],
        out_shape=[jax.ShapeDtypeStruct((depth, r, wide), BF16),
                   jax.ShapeDtypeStruct((depth, r, LANE), BF16)],
        compiler_params=_params("parallel", "parallel"),
        name="cast_w_in",
    )(w_in)


def _ffn_body(x_ref, nw_ref, wg_ref, wu_ref, wd_ref, fw_ref, o_ref, xn_ref, acc_ref, *, final):
    j = pl.program_id(1)

    @pl.when(j == 0)
    def _():
        x = x_ref[...]
        xn_ref[...] = ((x * _rms_scale(x)) * nw_ref[...]).astype(BF16)
        acc_ref[...] = jnp.zeros_like(acc_ref)

    xn = xn_ref[...]
    h = _dot(xn, wg_ref[...])
    u = _dot(xn, wu_ref[...])
    a = (h * jax.nn.sigmoid(h) * u).astype(BF16)
    acc_ref[...] += _dot(a, wd_ref[...])

    @pl.when(j == pl.num_programs(1) - 1)
    def _():
        y = x_ref[...] + 0.5 * acc_ref[...]
        if final:
            y = (y * _rms_scale(y)) * fw_ref[...]
        o_ref[...] = y


def _ffn(x, nw, wg, wu, wd, fw, layer, final):
    n, d = x.shape
    f = wg.shape[2]
    tm = _row_tile(n)
    return pl.pallas_call(
        functools.partial(_ffn_body, final=final),
        grid=(n // tm, f // FF_TILE),
        in_specs=[
            pl.BlockSpec((tm, d), lambda i, j: (i, 0)),
            pl.BlockSpec((1, d), lambda i, j: (0, 0)),
            pl.BlockSpec((None, d, FF_TILE), lambda i, j: (layer, 0, j)),
            pl.BlockSpec((None, d, FF_TILE), lambda i, j: (layer, 0, j)),
            pl.BlockSpec((None, FF_TILE, d), lambda i, j: (layer, j, 0)),
            pl.BlockSpec((1, d), lambda i, j: (0, 0)),
        ],
        out_specs=pl.BlockSpec((tm, d), lambda i, j: (i, 0)),
        out_shape=jax.ShapeDtypeStruct((n, d), F32),
        scratch_shapes=[pltpu.VMEM((tm, d), BF16), pltpu.VMEM((tm, d), F32)],
        compiler_params=_params("parallel", "arbitrary"),
        name="ffn",
    )(x, nw, wg, wu, wd, fw)


N_MAIN_BLK = MAIN_W // IN_TILE
N_K_BLK = DIFF_QK_W // IN_TILE
N_V_BLK = DIFF_V_W // IN_TILE


def _inproj_body(x_ref, nw_ref, w_ref, wgz_ref, wgk_ref, bgk_ref,
                 main_ref, k_ref, v_ref, kvb_ref, vt_ref, g_ref, xn_ref):
    j = pl.program_id(1)

    @pl.when(j == 0)
    def _():
        x = x_ref[...]
        xn = ((x * _rms_scale(x)) * nw_ref[...]).astype(BF16)
        xn_ref[...] = xn
        gz = _dot(xn, wgz_ref[...])
        z = _dot(gz.astype(BF16), wgk_ref[...]) + bgk_ref[...]
        g_ref[...] = (jnp.minimum(z, 0.0) - jnp.log1p(jnp.exp(-jnp.abs(z)))) * (1.0 / GLA_GATE_NORMALIZER)

    y = _dot(xn_ref[...], w_ref[...])

    @pl.when(j < N_MAIN_BLK)
    def _():
        main_ref[...] = y

    @pl.when(jnp.logical_and(j >= N_MAIN_BLK, j < N_MAIN_BLK + N_K_BLK))
    def _():
        k_ref[...] = y
        kvb_ref[...] = y.astype(BF16)

    @pl.when(j >= N_MAIN_BLK + N_K_BLK)
    def _():
        v_ref[...] = y
        kvb_ref[...] = y.astype(BF16)
        vt_ref[0] = y.T.astype(BF16)


def _inproj(x, nw, w_main, w_gz, w_gk, b_gk, layer):
    n, d = x.shape
    tm = _row_tile(n)
    nj = N_MAIN_BLK + N_K_BLK + N_V_BLK
    k0 = N_MAIN_BLK
    v0 = N_MAIN_BLK + N_K_BLK
    return pl.pallas_call(
        _inproj_body,
        grid=(n // tm, nj),
        in_specs=[
            pl.BlockSpec((tm, d), lambda i, j: (i, 0)),
            pl.BlockSpec((1, d), lambda i, j: (0, 0)),
            pl.BlockSpec((None, d, IN_TILE), lambda i, j: (layer, 0, j)),
            pl.BlockSpec((None, d, LANE), lambda i, j: (layer, 0, 0)),
            pl.BlockSpec((LANE, GLA_K_W), lambda i, j: (0, 0)),
            pl.BlockSpec((1, GLA_K_W), lambda i, j: (0, 0)),
        ],
        out_specs=[
            pl.BlockSpec((tm, IN_TILE), lambda i, j: (i, jnp.minimum(j, k0 - 1))),
            pl.BlockSpec((tm, IN_TILE), lambda i, j: (i, jnp.clip(j - k0, 0, N_K_BLK - 1))),
            pl.BlockSpec((tm, IN_TILE), lambda i, j: (i, jnp.clip(j - v0, 0, N_V_BLK - 1))),
            pl.BlockSpec((tm, IN_TILE), lambda i, j: (i, jnp.clip(j - k0, 0, N_K_BLK + N_V_BLK - 1))),
            pl.BlockSpec((1, IN_TILE, tm), lambda i, j: (i, jnp.clip(j - v0, 0, N_V_BLK - 1), 0)),
            pl.BlockSpec((tm, GLA_K_W), lambda i, j: (i, 0)),
        ],
        out_shape=[
            jax.ShapeDtypeStruct((n, MAIN_W), F32),
            jax.ShapeDtypeStruct((n, DIFF_QK_W), F32),
            jax.ShapeDtypeStruct((n, DIFF_V_W), F32),
            jax.ShapeDtypeStruct((n, DIFF_QK_W + DIFF_V_W), BF16),
            jax.ShapeDtypeStruct((n // tm, DIFF_V_W, tm), BF16),
            jax.ShapeDtypeStruct((n, GLA_K_W), F32),
        ],
        scratch_shapes=[pltpu.VMEM((tm, d), BF16)],
        compiler_params=_params("parallel", "arbitrary"),
        name="inproj",
    )(x, nw, w_main, w_gz, w_gk, b_gk)


def _split3(a):
    hi = a.astype(BF16)
    r = a - hi.astype(F32)
    mid = r.astype(BF16)
    lo = (r - mid.astype(F32)).astype(BF16)
    return hi, mid, lo


def _gla_body(q_ref, k_ref, v_ref, r_ref, g_ref, s0_ref, nw_ref, mix_in_ref, o_ref, s_ref, st_ref, *, chunk):
    del mix_in_ref
    t = pl.program_id(2)
    rows = q_ref.shape[0]

    @pl.when(t == 0)
    def _():
        st_ref[...] = s0_ref[0, 0].T

    ri = lax.broadcasted_iota(jnp.int32, (chunk, chunk), 0)
    ci = lax.broadcasted_iota(jnp.int32, (chunk, chunk), 1)
    causal = ci <= ri
    tri = causal.astype(BF16)

    for c in range(rows // chunk):
        sl = pl.ds(c * chunk, chunk)
        q = q_ref[sl, :] * (GLA_DK ** -0.5)
        k = k_ref[sl, :]
        v = v_ref[sl, :].astype(BF16)
        g = g_ref[sl, :]
        g_hi, g_mid, g_lo = _split3(g)
        b = _dot(tri, g_hi) + _dot(tri, g_mid) + _dot(tri, g_lo)
        b_last = b[chunk - 1:chunk, :]
        qe = (q * jnp.exp(b)).astype(BF16)
        ke = (k * jnp.exp(-b)).astype(BF16)
        kd = (k * jnp.exp(b_last - b)).astype(BF16)
        a = jnp.where(causal, _dot_nt(qe, ke), 0.0).astype(BF16)
        st = st_ref[...]
        o = _dot_nt(qe, st.astype(BF16)) + _dot(a, v)
        st_ref[...] = st * jnp.exp(b_last) + _dot_tn(v, kd)
        o = (o * _rms_scale(o)) * nw_ref[...]
        r = r_ref[sl, :]
        o_ref[sl, :] = (o * (r * jax.nn.sigmoid(r))).astype(BF16)

    @pl.when(t == pl.num_programs(2) - 1)
    def _():
        s_ref[0, 0] = st_ref[...].T


def _gla(main, g, s0, nw, mix, *, row_off, batch, seq, chunk, rows):
    assert seq % rows == 0 and rows % chunk == 0 and row_off % rows == 0
    nt = seq // rows
    rb0 = row_off // rows
    kq = GLA_DK // GLA_DK
    del kq

    def rowblk(b, h, t):
        return rb0 + b * nt + t

    v_col0 = 2 * GLA_K_W // GLA_DV
    r_col0 = (2 * GLA_K_W + GLA_V_W) // GLA_DV
    out, s = pl.pallas_call(
        functools.partial(_gla_body, chunk=chunk),
        grid=(batch, GLA_HEADS, nt),
        in_specs=[
            pl.BlockSpec((rows, GLA_DK), lambda b, h, t: (rowblk(b, h, t), h)),
            pl.BlockSpec((rows, GLA_DK), lambda b, h, t: (rowblk(b, h, t), GLA_HEADS + h)),
            pl.BlockSpec((rows, GLA_DV), lambda b, h, t: (rowblk(b, h, t), v_col0 + h)),
            pl.BlockSpec((rows, GLA_DV), lambda b, h, t: (rowblk(b, h, t), r_col0 + h)),
            pl.BlockSpec((rows, GLA_DK), lambda b, h, t: (rowblk(b, h, t), h)),
            pl.BlockSpec((1, 1, GLA_DK, GLA_DV), lambda b, h, t: (b, h, 0, 0)),
            pl.BlockSpec((1, GLA_DV), lambda b, h, t: (0, 0)),
            pl.BlockSpec(memory_space=pl.ANY),
        ],
        out_specs=[
            pl.BlockSpec((rows, GLA_DV), lambda b, h, t: (rowblk(b, h, t), h)),
            pl.BlockSpec((1, 1, GLA_DK, GLA_DV), lambda b, h, t: (b, h, 0, 0)),
        ],
        out_shape=[
            jax.ShapeDtypeStruct(mix.shape, mix.dtype),
            jax.ShapeDtypeStruct((batch, GLA_HEADS, GLA_DK, GLA_DV), F32),
        ],
        scratch_shapes=[pltpu.VMEM((GLA_DV, GLA_DK), F32)],
        input_output_aliases={7: 0},
        compiler_params=_params("parallel", "parallel", "arbitrary"),
        name="gla",
    )(main, main, main, main, g, s0, nw, mix)
    return out, s


def _bucket_thresholds():
    half = N_BUCKETS // 2
    m = half // 2
    e = half - m
    thr = []
    for kk in range(1, e):
        n = m
        while n ** e * m ** kk < m ** e * MAX_DISTANCE ** kk:
            n += 1
        thr.append(n)
    return tuple(thr)


_BUCKET_THR = _bucket_thresholds()


def _t5_bucket(rel):
    half = N_BUCKETS // 2
    max_exact = half // 2
    n = jnp.abs(rel)
    large = jnp.full(rel.shape, max_exact, jnp.int32)
    for thr in _BUCKET_THR:
        large = large + (n >= thr).astype(jnp.int32)
    return jnp.where(rel > 0, half, 0) + jnp.where(n < max_exact, n, large)


def _bias_body(rb_ref, o_ref, *, q_off, k_off, k_step, key_major, scale):
    h = pl.program_id(0)
    t = pl.program_id(1)
    shape = o_ref.shape[2:]
    qpos = q_off + lax.broadcasted_iota(jnp.int32, shape, 1 if key_major else 0)
    kpos = k_off + t * k_step + lax.broadcasted_iota(jnp.int32, shape, 0 if key_major else 1)
    bucket = _t5_bucket(kpos - qpos)
    acc = jnp.zeros(shape, F32)
    for bkt in range(N_BUCKETS):
        acc = jnp.where(bucket == bkt, rb_ref[bkt, h], acc)
    visible = (kpos >> 6) <= (qpos >> 6)
    o_ref[0, 0] = jnp.where(visible, acc * scale, -jnp.inf)


assert CHUNK == 64


def _bias_table(rel_bias, *, tiles, rows, cols, q_off, k_off, k_step, key_major=False, scale=1.0):
    return pl.pallas_call(
        functools.partial(_bias_body, q_off=q_off, k_off=k_off, k_step=k_step, key_major=key_major,
                          scale=scale),
        grid=(DIFF_HEADS, tiles),
        in_specs=[pl.BlockSpec(memory_space=pltpu.SMEM)],
        out_specs=pl.BlockSpec((1, 1, rows, cols), lambda h, t: (h, t, 0, 0)),
        out_shape=jax.ShapeDtypeStruct((DIFF_HEADS, tiles, rows, cols), F32),
        compiler_params=_params("parallel", "parallel"),
        name="bias_table",
    )(rel_bias)


def _lambda(lam_ref, layer):
    lam = lam_ref[...]
    a = jnp.sum(lam[0:1, :] * lam[1:2, :], axis=-1, keepdims=True)
    b = jnp.sum(lam[2:3, :] * lam[3:4, :], axis=-1, keepdims=True)
    lam_init = 0.8 - 0.6 * math.exp(-0.3 * layer)
    return jnp.exp(a) - jnp.exp(b) + lam_init, lam_init


def _diff_finish(o1, l1, o2, l2, lam, lam_init, nw):
    o = o1 / l1 - lam * (o2 / l2)
    return ((o * _rms_scale(o)) * nw * (1.0 - lam_init)).astype(BF16)


def _attn_prompt_body(q_ref, k_ref, vt_ref, bias_ref, lam_ref, nw_ref, mix_in_ref,
                      o_ref, acc_ref, s_ref, p_ref, *, layer, blk):
    del mix_in_ref
    qi = pl.program_id(1)
    qt = (q_ref[...] * (DIFF_D ** -0.5 * LOG2E)).T.astype(BF16)
    qth = (qt[:DIFF_D], qt[DIFF_D:])
    acc_ref[...] = jnp.zeros_like(acc_ref)
    strips = [slice(c * SUB, (c + 1) * SUB) for c in range(blk // SUB)]

    def fold(a):
        return a.reshape(SUB // 8, 8, blk)

    def step(kj, stats):
        kb = k_ref[pl.ds(pl.multiple_of(kj * blk, blk), blk), :]
        vt = vt_ref[kj]
        t = jnp.minimum(qi - kj, 2)
        for half in range(2):
            s_ref[half] = _dot(kb[:, half * DIFF_D:(half + 1) * DIFF_D], qth[half]) + bias_ref[0, t]
        out = []
        for half in range(2):
            m, l = stats[half]
            bm = jnp.max(fold(s_ref[half, strips[0], :]), axis=0)
            for c in strips[1:]:
                bm = jnp.maximum(bm, jnp.max(fold(s_ref[half, c, :]), axis=0))
            m_new = jnp.maximum(m, jnp.max(bm, axis=0, keepdims=True))
            alpha = jnp.exp2(m - m_new)
            ls = jnp.zeros((8, blk), F32)
            for c in strips:
                p = jnp.exp2(s_ref[half, c, :] - m_new)
                ls = ls + jnp.sum(fold(p), axis=0)
                p_ref[half, c, :] = p.astype(BF16)
            l = alpha * l + jnp.sum(ls, axis=0, keepdims=True)
            acc_ref[half] = alpha * acc_ref[half] + _dot(vt, p_ref[half])
            out.append((m_new, l))
        return tuple(out)

    init = tuple((jnp.full((1, blk), -jnp.inf, F32), jnp.zeros((1, blk), F32)) for _ in range(2))
    stats = lax.fori_loop(0, qi + 1, step, init)

    lam, lam_init = _lambda(lam_ref, layer)
    ot = acc_ref[0] / stats[0][1] - lam * (acc_ref[1] / stats[1][1])
    scale = lax.rsqrt(jnp.mean(ot * ot, axis=0, keepdims=True) + EPS)
    ot = (ot * scale) * nw_ref[...] * (1.0 - lam_init)
    o_ref[...] = ot.T.astype(BF16)


def _attn_prompt(main, kvb, vt, bias, lam, nw_col, mix, *, layer, seq):
    blk = vt.shape[2]
    assert seq % blk == 0 and blk >= MAX_DISTANCE and blk % CHUNK == 0
    q_col0 = DQ_OFF // (2 * DIFF_D)
    o_col0 = GLA_V_W // DIFF_DV
    return pl.pallas_call(
        functools.partial(_attn_prompt_body, layer=layer, blk=blk),
        grid=(DIFF_HEADS, seq // blk),
        in_specs=[
            pl.BlockSpec((blk, 2 * DIFF_D), lambda h, i: (i, q_col0 + h)),
            pl.BlockSpec((seq, 2 * DIFF_D), lambda h, i: (0, h)),
            pl.BlockSpec((seq // blk, DIFF_DV, blk), lambda h, i: (0, h, 0)),
            pl.BlockSpec((1, 3, blk, blk), lambda h, i: (h, 0, 0, 0)),
            pl.BlockSpec((4, DIFF_D), lambda h, i: (0, 0)),
            pl.BlockSpec((DIFF_DV, 1), lambda h, i: (0, 0)),
            pl.BlockSpec(memory_space=pl.ANY),
        ],
        out_specs=pl.BlockSpec((blk, DIFF_DV), lambda h, i: (i, o_col0 + h)),
        out_shape=jax.ShapeDtypeStruct(mix.shape, mix.dtype),
        scratch_shapes=[pltpu.VMEM((2, DIFF_DV, blk), F32), pltpu.VMEM((2, blk, blk), F32),
                        pltpu.VMEM((2, blk, blk), BF16)],
        input_output_aliases={6: 0},
        compiler_params=_params("parallel", "arbitrary"),
        name="attn_prompt",
    )(main, kvb, vt, bias, lam, nw_col, mix)


def _attn_sample_body(q_ref, kn_ref, vn_ref, kc_hbm, vc_hbm, bias_ref, lam_ref, nw_ref, mix_in_ref,
                      o_ref, kbuf, vbuf, sem, *, layer, past):
    del mix_in_ref
    b = pl.program_id(0)
    h = pl.program_id(1)
    n_h = pl.num_programs(1)
    step = b * n_h + h
    slot = step % 2

    def copies(cb, ch, cslot):
        return (pltpu.make_async_copy(kc_hbm.at[layer, cb, :, ch, :], kbuf.at[cslot], sem.at[0, cslot]),
                pltpu.make_async_copy(vc_hbm.at[layer, cb, :, ch, :], vbuf.at[cslot], sem.at[1, cslot]))

    @pl.when(step == 0)
    def _():
        for c in copies(b, h, slot):
            c.start()

    @pl.when(step + 1 < pl.num_programs(0) * n_h)
    def _():
        wrap = h + 1 == n_h
        for c in copies(jnp.where(wrap, b + 1, b), jnp.where(wrap, 0, h + 1), 1 - slot):
            c.start()

    qs = (q_ref[...] * (DIFF_D ** -0.5)).astype(BF16)
    kn = kn_ref[...]
    bias_c = bias_ref[0, 0, :, :past]
    bias_n = bias_ref[0, 0, :, past:]
    for c in copies(b, h, slot):
        c.wait()
    ps = []
    for half in range(2):
        cols = slice(half * DIFF_D, (half + 1) * DIFF_D)
        sc = _dot_nt(qs[:, cols], kbuf[slot, :, cols].astype(BF16)) + bias_c
        sn = _dot_nt(qs[:, cols], kn[:, cols]) + bias_n
        m = jnp.maximum(jnp.max(sc, axis=-1, keepdims=True), jnp.max(sn, axis=-1, keepdims=True))
        pc = jnp.exp(sc - m)
        pn = jnp.exp(sn - m)
        inv = 1.0 / (jnp.sum(pc, axis=-1, keepdims=True) + jnp.sum(pn, axis=-1, keepdims=True))
        ps.append((pc * inv, pn * inv))
    lam, lam_init = _lambda(lam_ref, layer)
    wc = (ps[0][0] - lam * ps[1][0]).astype(BF16)
    wn = (ps[0][1] - lam * ps[1][1]).astype(BF16)
    o = _dot(wc, vbuf[slot].astype(BF16)) + _dot(wn, vn_ref[...])
    o_ref[...] = ((o * _rms_scale(o)) * nw_ref[...] * (1.0 - lam_init)).astype(BF16)


def _attn_sample(main, kvb, cache_k, cache_v, bias, lam, nw, mix, *, layer, row_off, batch, seq):
    past = cache_k.shape[2]
    assert row_off % seq == 0
    rb0 = row_off // seq
    q_col0 = DQ_OFF // (2 * DIFF_D)
    v_col0 = DIFF_QK_W // DIFF_DV
    o_col0 = GLA_V_W // DIFF_DV
    return pl.pallas_call(
        functools.partial(_attn_sample_body, layer=layer, past=past),
        grid=(batch, DIFF_HEADS),
        in_specs=[
            pl.BlockSpec((seq, 2 * DIFF_D), lambda b, h: (rb0 + b, q_col0 + h)),
            pl.BlockSpec((seq, 2 * DIFF_D), lambda b, h: (rb0 + b, h)),
            pl.BlockSpec((seq, DIFF_DV), lambda b, h: (rb0 + b, v_col0 + h)),
            pl.BlockSpec(memory_space=pl.ANY),
            pl.BlockSpec(memory_space=pl.ANY),
            pl.BlockSpec((1, 1, seq, past + seq), lambda b, h: (h, 0, 0, 0)),
            pl.BlockSpec((4, DIFF_D), lambda b, h: (0, 0)),
            pl.BlockSpec((1, DIFF_DV), lambda b, h: (0, 0)),
            pl.BlockSpec(memory_space=pl.ANY),
        ],
        out_specs=pl.BlockSpec((seq, DIFF_DV), lambda b, h: (rb0 + b, o_col0 + h)),
        out_shape=jax.ShapeDtypeStruct(mix.shape, mix.dtype),
        scratch_shapes=[
            pltpu.VMEM((2, past, 2 * DIFF_D), F32),
            pltpu.VMEM((2, past, DIFF_DV), F32),
            pltpu.SemaphoreType.DMA((2, 2)),
        ],
        input_output_aliases={8: 0},
        compiler_params=_params("arbitrary", "arbitrary"),
        name="attn_sample",
    )(main, kvb, kvb, cache_k, cache_v, bias, lam, nw, mix)


def _outproj_body(o_ref, w_ref, x_ref, y_ref):
    y_ref[...] = x_ref[...] + _dot(o_ref[...], w_ref[...])


def _outproj(mix, w, x, layer):
    n, d = x.shape
    tm = _row_tile(n)
    return pl.pallas_call(
        _outproj_body,
        grid=(n // tm,),
        in_specs=[
            pl.BlockSpec((tm, MIX_WIDTH), lambda i: (i, 0)),
            pl.BlockSpec((None, MIX_WIDTH, d), lambda i: (layer, 0, 0)),
            pl.BlockSpec((tm, d), lambda i: (i, 0)),
        ],
        out_specs=pl.BlockSpec((tm, d), lambda i: (i, 0)),
        out_shape=jax.ShapeDtypeStruct((n, d), F32),
        compiler_params=_params("parallel"),
        name="outproj",
    )(mix, w, x)


def kernel(x_prompt, x_sample, cache_k, cache_v, state_gla, ffn1_norm, ffn1_w_gate, ffn1_w_up, ffn1_w_down,
           mix_norm, w_in, gla_w_gk, gla_b_gk, gla_norm, diff_lambda, diff_norm, w_out,
           ffn2_norm, ffn2_w_gate, ffn2_w_up, ffn2_w_down, rel_bias, final_norm):
    pb, ps, d = x_prompt.shape
    sb, ss, _ = x_sample.shape
    depth = w_in.shape[0]
    past = cache_k.shape[2]
    assert pb == 1
    n_p, n_s = pb * ps, sb * ss
    x = jnp.concatenate([x_prompt.reshape(n_p, d), x_sample.reshape(n_s, d)], axis=0)

    blk = _row_tile(n_p + n_s)
    bias_p = _bias_table(rel_bias, tiles=3, rows=blk, cols=blk, q_off=0, k_off=0, k_step=-blk,
                         key_major=True, scale=LOG2E)
    bias_s = _bias_table(rel_bias, tiles=1, rows=ss, cols=past + ss, q_off=past, k_off=0, k_step=0)
    zero_state = jnp.zeros((pb, GLA_HEADS, GLA_DK, GLA_DV), F32)
    row = lambda a: a.reshape(1, -1)

    f1g, f1u, f1d = _cast_bf16(ffn1_w_gate), _cast_bf16(ffn1_w_up), _cast_bf16(ffn1_w_down)
    f2g, f2u, f2d = _cast_bf16(ffn2_w_gate), _cast_bf16(ffn2_w_up), _cast_bf16(ffn2_w_down)
    w_main, w_gz = _cast_w_in(w_in)
    w_o = _cast_bf16(w_out)

    k_rows, v_rows, p_states, s_states = [], [], [], []
    for l in range(depth):
        w_gk = jnp.pad(gla_w_gk[l], ((0, LANE - GLA_GK_RANK), (0, 0))).astype(BF16)

        x = _ffn(x, row(ffn1_norm[l]), f1g, f1u, f1d, row(final_norm), l, False)
        main, kf, vf, kvb, vt, g = _inproj(x, row(mix_norm[l]), w_main, w_gz, w_gk, row(gla_b_gk[l]), l)

        mix = jnp.zeros((n_p + n_s, MIX_WIDTH), BF16)
        mix, sp = _gla(main, g, zero_state, row(gla_norm[l]), mix,
                       row_off=0, batch=pb, seq=ps, chunk=CHUNK, rows=min(GLA_ROWS, ps))
        mix, s_s = _gla(main, g, state_gla[l], row(gla_norm[l]), mix,
                        row_off=n_p, batch=sb, seq=ss, chunk=ss, rows=ss)
        mix = _attn_prompt(main, kvb, vt, bias_p, diff_lambda[l], diff_norm[l].reshape(-1, 1), mix,
                           layer=l, seq=ps)
        mix = _attn_sample(main, kvb, cache_k, cache_v, bias_s, diff_lambda[l], row(diff_norm[l]), mix,
                           layer=l, row_off=n_p, batch=sb, seq=ss)

        x = _outproj(mix, w_o, x, l)
        x = _ffn(x, row(ffn2_norm[l]), f2g, f2u, f2d, row(final_norm), l, l == depth - 1)
        k_rows.append(kf)
        v_rows.append(vf)
        p_states.append(sp)
        s_states.append(s_s)

    def rows_of(parts, lo, hi, shape):
        return jnp.stack([p[lo:hi].reshape(shape) for p in parts])

    return (x[:n_p].reshape(pb, ps, d),
            x[n_p:].reshape(sb, ss, d),
            rows_of(k_rows, 0, n_p, (pb, ps, DIFF_HEADS, 2 * DIFF_D)),
            rows_of(v_rows, 0, n_p, (pb, ps, DIFF_HEADS, DIFF_DV)),
            jnp.stack(p_states),
            rows_of(k_rows, n_p, n_p + n_s, (sb, ss, DIFF_HEADS, 2 * DIFF_D)),
            rows_of(v_rows, n_p, n_p + n_s, (sb, ss, DIFF_HEADS, DIFF_DV)),
            jnp.stack(s_states))
```

```python
import functools
import math

import jax
import jax.numpy as jnp
from jax import lax
from jax.experimental import pallas as pl
from jax.experimental.pallas import tpu as pltpu

F32 = jnp.float32
BF16 = jnp.bfloat16

EPS = 1e-6
CHUNK = 64
GLA_HEADS = 4
GLA_DK = 128
GLA_DV = 256
GLA_GK_RANK = 16
GLA_GATE_NORMALIZER = 16.0
DIFF_HEADS = 4
DIFF_D = 128
DIFF_DV = 2 * DIFF_D
N_BUCKETS = 32
MAX_DISTANCE = 128

GLA_K_W = GLA_HEADS * GLA_DK
GLA_V_W = GLA_HEADS * GLA_DV
DIFF_QK_W = DIFF_HEADS * 2 * DIFF_D
DIFF_V_W = DIFF_HEADS * DIFF_DV
MIX_WIDTH = GLA_V_W + DIFF_V_W
GZ_OFF = 2 * GLA_K_W + 2 * GLA_V_W
MAIN_W = GZ_OFF + DIFF_QK_W
DQ_OFF = GZ_OFF

LANE = 128
LOG2E = math.log2(math.e)
SUB = 64
VMEM_LIMIT = 56 * 1024 * 1024

ROW_TILE = 512
FF_TILE = 512
IN_TILE = 1024
GLA_ROWS = 256


def _dot(a, b):
    return jnp.dot(a, b, preferred_element_type=F32)


def _dot_nt(a, b):
    return lax.dot_general(a, b, (((1,), (1,)), ((), ())), preferred_element_type=F32)


def _dot_tn(a, b):
    return lax.dot_general(a, b, (((0,), (0,)), ((), ())), preferred_element_type=F32)


def _params(*sem):
    return pltpu.CompilerParams(dimension_semantics=sem, vmem_limit_bytes=VMEM_LIMIT)


def _row_tile(n):
    t = ROW_TILE
    while n % t:
        t //= 2
    return t


def _rms_scale(x):
    return lax.rsqrt(jnp.mean(x * x, axis=-1, keepdims=True) + EPS)


CAST_ROWS = 256


def _cast_body(w_ref, o_ref):
    o_ref[...] = w_ref[...].astype(BF16)


def _cast_bf16(w):
    depth, r, c = w.shape
    spec = pl.BlockSpec((1, CAST_ROWS, c), lambda l, i: (l, i, 0))
    return pl.pallas_call(
        _cast_body,
        grid=(depth, r // CAST_ROWS),
        in_specs=[spec],
        out_specs=spec,
        out_shape=jax.ShapeDtypeStruct(w.shape, BF16),
        compiler_params=_params("parallel", "parallel"),
        name="cast_bf16",
    )(w)


def _cast_w_in_body(w_ref, main_ref, gz_ref):
    w = w_ref[0]
    main_ref[0, :, :GZ_OFF] = w[:, :GZ_OFF].astype(BF16)
    main_ref[0, :, GZ_OFF:] = w[:, GZ_OFF + GLA_GK_RANK:].astype(BF16)
    gz = w[:, GZ_OFF:GZ_OFF + GLA_GK_RANK]
    gz_ref[0] = jnp.concatenate([gz, jnp.zeros((gz.shape[0], LANE - GLA_GK_RANK), F32)], axis=1).astype(BF16)


def _cast_w_in(w_in):
    depth, r, c = w_in.shape
    wide = c - GLA_GK_RANK
    return pl.pallas_call(
        _cast_w_in_body,
        grid=(depth, r // CAST_ROWS),
        in_specs=[pl.BlockSpec((1, CAST_ROWS, c), lambda l, i: (l, i, 0))],
        out_specs=[pl.BlockSpec((1, CAST_ROWS, wide), lambda l, i: (l, i, 0)),
                   pl.BlockSpec((1, CAST_ROWS, LANE), lambda l, i: (l, i, 0))],
        out_shape=[jax.ShapeDtypeStruct((depth, r, wide), BF16),
                   jax.ShapeDtypeStruct((depth, r, LANE), BF16)],
        compiler_params=_params("parallel", "parallel"),
        name="cast_w_in",
    )(w_in)


def _ffn_body(*refs, final, n_in, n_out, split):
    x_refs = refs[:n_in]
    nw_ref, wg_ref, wu_ref, wd_ref, fw_ref = refs[n_in:n_in + 5]
    o_refs = refs[n_in + 5:n_in + 5 + n_out]
    xn_ref, acc_ref = refs[n_in + 5 + n_out:]
    i = pl.program_id(0)
    j = pl.program_id(1)

    def load_x():
        if n_in == 1:
            return x_refs[0][...]
        return jnp.where(i < split, x_refs[0][...], x_refs[1][...])

    @pl.when(j == 0)
    def _():
        x = load_x()
        xn_ref[...] = ((x * _rms_scale(x)) * nw_ref[...]).astype(BF16)
        acc_ref[...] = jnp.zeros_like(acc_ref)

    xn = xn_ref[...]
    h = _dot(xn, wg_ref[...])
    u = _dot(xn, wu_ref[...])
    a = (h * jax.nn.sigmoid(h) * u).astype(BF16)
    acc_ref[...] += _dot(a, wd_ref[...])

    @pl.when(j == pl.num_programs(1) - 1)
    def _():
        y = load_x() + 0.5 * acc_ref[...]
        if final:
            y = (y * _rms_scale(y)) * fw_ref[...]
        if n_out == 1:
            o_refs[0][...] = y
        else:
            @pl.when(i < split)
            def _():
                o_refs[0][...] = y

            @pl.when(i >= split)
            def _():
                o_refs[1][...] = y


def _ffn(xs, nw, wg, wu, wd, fw, layer, final, out_rows):
    d = xs[0].shape[1]
    n = sum(x.shape[0] for x in xs)
    assert n == sum(out_rows)
    f = wg.shape[2]
    tm = _row_tile(n)
    first = xs[0].shape[0] if len(xs) == 2 else out_rows[0]
    assert first % tm == 0
    split = first // tm
    parts = [lambda i, j: (jnp.minimum(i, split - 1), 0), lambda i, j: (jnp.maximum(i - split, 0), 0)]
    whole = [lambda i, j: (i, 0)]
    outs = pl.pallas_call(
        functools.partial(_ffn_body, final=final, n_in=len(xs), n_out=len(out_rows), split=split),
        grid=(n // tm, f // FF_TILE),
        in_specs=[pl.BlockSpec((tm, d), m) for m in (parts if len(xs) == 2 else whole)] + [
            pl.BlockSpec((1, d), lambda i, j: (0, 0)),
            pl.BlockSpec((None, d, FF_TILE), lambda i, j: (layer, 0, j)),
            pl.BlockSpec((None, d, FF_TILE), lambda i, j: (layer, 0, j)),
            pl.BlockSpec((None, FF_TILE, d), lambda i, j: (layer, j, 0)),
            pl.BlockSpec((1, d), lambda i, j: (0, 0)),
        ],
        out_specs=[pl.BlockSpec((tm, d), m) for m in (parts if len(out_rows) == 2 else whole)],
        out_shape=[jax.ShapeDtypeStruct((r, d), F32) for r in out_rows],
        scratch_shapes=[pltpu.VMEM((tm, d), BF16), pltpu.VMEM((tm, d), F32)],
        compiler_params=_params("arbitrary" if len(out_rows) == 2 else "parallel", "arbitrary"),
        name="ffn",
    )(*xs, nw, wg, wu, wd, fw)
    return outs


N_MAIN_BLK = MAIN_W // IN_TILE
N_K_BLK = DIFF_QK_W // IN_TILE
N_V_BLK = DIFF_V_W // IN_TILE


def _inproj_body(x_ref, nw_ref, w_ref, wgz_ref, wgk_ref, bgk_ref,
                 main_ref, kvb_ref, vt_ref, g_ref, xn_ref):
    j = pl.program_id(1)

    @pl.when(j == 0)
    def _():
        x = x_ref[...]
        xn = ((x * _rms_scale(x)) * nw_ref[...]).astype(BF16)
        xn_ref[...] = xn
        gz = _dot(xn, wgz_ref[...])
        z = _dot(gz.astype(BF16), wgk_ref[...]) + bgk_ref[...]
        g_ref[...] = (jnp.minimum(z, 0.0) - jnp.log1p(jnp.exp(-jnp.abs(z)))) * (1.0 / GLA_GATE_NORMALIZER)

    main_ref[...] = _dot(xn_ref[...], w_ref[...])

    @pl.when(j >= N_MAIN_BLK)
    def _():
        kvb_ref[...] = main_ref[...].astype(BF16)

    @pl.when(j >= N_MAIN_BLK + N_K_BLK)
    def _():
        vt_ref[0] = main_ref[...].T.astype(BF16)


def _inproj(x, nw, w_main, w_gz, w_gk, b_gk, layer):
    n, d = x.shape
    tm = _row_tile(n)
    nj = N_MAIN_BLK + N_K_BLK + N_V_BLK
    k0 = N_MAIN_BLK
    v0 = N_MAIN_BLK + N_K_BLK
    return pl.pallas_call(
        _inproj_body,
        grid=(n // tm, nj),
        in_specs=[
            pl.BlockSpec((tm, d), lambda i, j: (i, 0)),
            pl.BlockSpec((1, d), lambda i, j: (0, 0)),
            pl.BlockSpec((None, d, IN_TILE), lambda i, j: (layer, 0, j)),
            pl.BlockSpec((None, d, LANE), lambda i, j: (layer, 0, 0)),
            pl.BlockSpec((LANE, GLA_K_W), lambda i, j: (0, 0)),
            pl.BlockSpec((1, GLA_K_W), lambda i, j: (0, 0)),
        ],
        out_specs=[
            pl.BlockSpec((tm, IN_TILE), lambda i, j: (i, j)),
            pl.BlockSpec((tm, IN_TILE), lambda i, j: (i, jnp.clip(j - k0, 0, N_K_BLK + N_V_BLK - 1))),
            pl.BlockSpec((1, IN_TILE, tm), lambda i, j: (i, jnp.clip(j - v0, 0, N_V_BLK - 1), 0)),
            pl.BlockSpec((tm, GLA_K_W), lambda i, j: (i, 0)),
        ],
        out_shape=[
            jax.ShapeDtypeStruct((n, MAIN_W + DIFF_QK_W + DIFF_V_W), F32),
            jax.ShapeDtypeStruct((n, DIFF_QK_W + DIFF_V_W), BF16),
            jax.ShapeDtypeStruct((n // tm, DIFF_V_W, tm), BF16),
            jax.ShapeDtypeStruct((n, GLA_K_W), F32),
        ],
        scratch_shapes=[pltpu.VMEM((tm, d), BF16)],
        compiler_params=_params("parallel", "arbitrary"),
        name="inproj",
    )(x, nw, w_main, w_gz, w_gk, b_gk)


def _split3(a):
    hi = a.astype(BF16)
    r = a - hi.astype(F32)
    mid = r.astype(BF16)
    lo = (r - mid.astype(F32)).astype(BF16)
    return hi, mid, lo


def _gla_body(q_ref, k_ref, v_ref, r_ref, g_ref, s0_ref, nw_ref, mix_in_ref, o_ref, s_ref, st_ref, *, chunk):
    del mix_in_ref
    t = pl.program_id(1)
    rows = q_ref.shape[0]

    @pl.when(t == 0)
    def _():
        for h in range(GLA_HEADS):
            st_ref[h] = s0_ref[0, h].T

    ri = lax.broadcasted_iota(jnp.int32, (chunk, chunk), 0)
    ci = lax.broadcasted_iota(jnp.int32, (chunk, chunk), 1)
    causal = ci <= ri
    tri = causal.astype(BF16)

    for c in range(rows // chunk):
        sl = pl.ds(c * chunk, chunk)
        for h in range(GLA_HEADS):
            kc = slice(h * GLA_DK, (h + 1) * GLA_DK)
            vc = slice(h * GLA_DV, (h + 1) * GLA_DV)
            q = q_ref[sl, kc] * (GLA_DK ** -0.5)
            k = k_ref[sl, kc]
            v = v_ref[sl, vc].astype(BF16)
            g_hi, g_mid, g_lo = _split3(g_ref[sl, kc])
            b = _dot(tri, g_hi) + _dot(tri, g_mid) + _dot(tri, g_lo)
            b_last = b[chunk - 1:chunk, :]
            qe = (q * jnp.exp(b)).astype(BF16)
            ke = (k * jnp.exp(-b)).astype(BF16)
            kd = (k * jnp.exp(b_last - b)).astype(BF16)
            a = jnp.where(causal, _dot_nt(qe, ke), 0.0).astype(BF16)
            st = st_ref[h]
            o = _dot_nt(qe, st.astype(BF16)) + _dot(a, v)
            st_ref[h] = st * jnp.exp(b_last) + _dot_tn(v, kd)
            o = (o * _rms_scale(o)) * nw_ref[...]
            r = r_ref[sl, vc]
            o_ref[sl, vc] = (o * (r * jax.nn.sigmoid(r))).astype(BF16)

    @pl.when(t == pl.num_programs(1) - 1)
    def _():
        for h in range(GLA_HEADS):
            s_ref[0, h] = st_ref[h].T


def _gla(main, g, s0, nw, mix, *, row_off, batch, seq, chunk, rows):
    assert seq % rows == 0 and rows % chunk == 0 and row_off % rows == 0
    nt = seq // rows
    rb0 = row_off // rows

    def rowblk(b, t):
        return rb0 + b * nt + t

    state_spec = pl.BlockSpec((1, GLA_HEADS, GLA_DK, GLA_DV), lambda b, t: (b, 0, 0, 0))
    out, s = pl.pallas_call(
        functools.partial(_gla_body, chunk=chunk),
        grid=(batch, nt),
        in_specs=[
            pl.BlockSpec((rows, GLA_K_W), lambda b, t: (rowblk(b, t), 0)),
            pl.BlockSpec((rows, GLA_K_W), lambda b, t: (rowblk(b, t), 1)),
            pl.BlockSpec((rows, GLA_V_W), lambda b, t: (rowblk(b, t), 2 * GLA_K_W // GLA_V_W)),
            pl.BlockSpec((rows, GLA_V_W), lambda b, t: (rowblk(b, t), 2 * GLA_K_W // GLA_V_W + 1)),
            pl.BlockSpec((rows, GLA_K_W), lambda b, t: (rowblk(b, t), 0)),
            state_spec,
            pl.BlockSpec((1, GLA_DV), lambda b, t: (0, 0)),
            pl.BlockSpec(memory_space=pl.ANY),
        ],
        out_specs=[
            pl.BlockSpec((rows, GLA_V_W), lambda b, t: (rowblk(b, t), 0)),
            state_spec,
        ],
        out_shape=[
            jax.ShapeDtypeStruct(mix.shape, mix.dtype),
            jax.ShapeDtypeStruct((batch, GLA_HEADS, GLA_DK, GLA_DV), F32),
        ],
        scratch_shapes=[pltpu.VMEM((GLA_HEADS, GLA_DV, GLA_DK), F32)],
        input_output_aliases={7: 0},
        compiler_params=_params("parallel", "arbitrary"),
        name="gla",
    )(main, main, main, main, g, s0, nw, mix)
    return out, s


def _bucket_thresholds():
    half = N_BUCKETS // 2
    m = half // 2
    e = half - m
    thr = []
    for kk in range(1, e):
        n = m
        while n ** e * m ** kk < m ** e * MAX_DISTANCE ** kk:
            n += 1
        thr.append(n)
    return tuple(thr)


_BUCKET_THR = _bucket_thresholds()


def _t5_bucket(rel):
    half = N_BUCKETS // 2
    max_exact = half // 2
    n = jnp.abs(rel)
    large = jnp.full(rel.shape, max_exact, jnp.int32)
    for thr in _BUCKET_THR:
        large = large + (n >= thr).astype(jnp.int32)
    return jnp.where(rel > 0, half, 0) + jnp.where(n < max_exact, n, large)


def _bias_body(rb_ref, o_ref, *, q_off, k_off, k_step, key_major, scale):
    h = pl.program_id(0)
    t = pl.program_id(1)
    shape = o_ref.shape[2:]
    qpos = q_off + lax.broadcasted_iota(jnp.int32, shape, 1 if key_major else 0)
    kpos = k_off + t * k_step + lax.broadcasted_iota(jnp.int32, shape, 0 if key_major else 1)
    bucket = _t5_bucket(kpos - qpos)
    acc = jnp.zeros(shape, F32)
    for bkt in range(N_BUCKETS):
        acc = jnp.where(bucket == bkt, rb_ref[bkt, h], acc)
    visible = (kpos >> 6) <= (qpos >> 6)
    o_ref[0, 0] = jnp.where(visible, acc * scale, -jnp.inf)


assert CHUNK == 64


def _bias_table(rel_bias, *, tiles, rows, cols, q_off, k_off, k_step, key_major=False, scale=1.0):
    return pl.pallas_call(
        functools.partial(_bias_body, q_off=q_off, k_off=k_off, k_step=k_step, key_major=key_major,
                          scale=scale),
        grid=(DIFF_HEADS, tiles),
        in_specs=[pl.BlockSpec(memory_space=pltpu.SMEM)],
        out_specs=pl.BlockSpec((1, 1, rows, cols), lambda h, t: (h, t, 0, 0)),
        out_shape=jax.ShapeDtypeStruct((DIFF_HEADS, tiles, rows, cols), F32),
        compiler_params=_params("parallel", "parallel"),
        name="bias_table",
    )(rel_bias)


def _lambda(lam_ref, layer):
    lam = lam_ref[...]
    a = jnp.sum(lam[0:1, :] * lam[1:2, :], axis=-1, keepdims=True)
    b = jnp.sum(lam[2:3, :] * lam[3:4, :], axis=-1, keepdims=True)
    lam_init = 0.8 - 0.6 * math.exp(-0.3 * layer)
    return jnp.exp(a) - jnp.exp(b) + lam_init, lam_init


def _diff_finish(o1, l1, o2, l2, lam, lam_init, nw):
    o = o1 / l1 - lam * (o2 / l2)
    return ((o * _rms_scale(o)) * nw * (1.0 - lam_init)).astype(BF16)


def _attn_prompt_body(q_ref, k_ref, vt_ref, bias_ref, lam_ref, nw_ref, mix_in_ref,
                      o_ref, acc_ref, s_ref, p_ref, *, layer, blk):
    del mix_in_ref
    qi = pl.program_id(1)
    qt = (q_ref[...] * (DIFF_D ** -0.5 * LOG2E)).T.astype(BF16)
    qth = (qt[:DIFF_D], qt[DIFF_D:])
    acc_ref[...] = jnp.zeros_like(acc_ref)
    strips = [slice(c * SUB, (c + 1) * SUB) for c in range(blk // SUB)]

    def fold(a):
        return a.reshape(SUB // 8, 8, blk)

    def step(kj, stats):
        kb = k_ref[pl.ds(pl.multiple_of(kj * blk, blk), blk), :]
        vt = vt_ref[kj]
        t = jnp.minimum(qi - kj, 2)
        for half in range(2):
            s_ref[half] = _dot(kb[:, half * DIFF_D:(half + 1) * DIFF_D], qth[half]) + bias_ref[0, t]
        out = []
        for half in range(2):
            m, l = stats[half]
            bm = jnp.max(fold(s_ref[half, strips[0], :]), axis=0)
            for c in strips[1:]:
                bm = jnp.maximum(bm, jnp.max(fold(s_ref[half, c, :]), axis=0))
            m_new = jnp.maximum(m, jnp.max(bm, axis=0, keepdims=True))
            alpha = jnp.exp2(m - m_new)
            ls = jnp.zeros((8, blk), F32)
            for c in strips:
                p = jnp.exp2(s_ref[half, c, :] - m_new)
                ls = ls + jnp.sum(fold(p), axis=0)
                p_ref[half, c, :] = p.astype(BF16)
            l = alpha * l + jnp.sum(ls, axis=0, keepdims=True)
            acc_ref[half] = alpha * acc_ref[half] + _dot(vt, p_ref[half])
            out.append((m_new, l))
        return tuple(out)

    init = tuple((jnp.full((1, blk), -jnp.inf, F32), jnp.zeros((1, blk), F32)) for _ in range(2))
    stats = lax.fori_loop(0, qi + 1, step, init)

    lam, lam_init = _lambda(lam_ref, layer)
    ot = acc_ref[0] / stats[0][1] - lam * (acc_ref[1] / stats[1][1])
    scale = lax.rsqrt(jnp.mean(ot * ot, axis=0, keepdims=True) + EPS)
    ot = (ot * scale) * nw_ref[...] * (1.0 - lam_init)
    o_ref[...] = ot.T.astype(BF16)


def _attn_prompt(main, kvb, vt, bias, lam, nw_col, mix, *, layer, seq):
    blk = vt.shape[2]
    assert seq % blk == 0 and blk >= MAX_DISTANCE and blk % CHUNK == 0
    q_col0 = DQ_OFF // (2 * DIFF_D)
    o_col0 = GLA_V_W // DIFF_DV
    return pl.pallas_call(
        functools.partial(_attn_prompt_body, layer=layer, blk=blk),
        grid=(DIFF_HEADS, seq // blk),
        in_specs=[
            pl.BlockSpec((blk, 2 * DIFF_D), lambda h, i: (i, q_col0 + h)),
            pl.BlockSpec((seq, 2 * DIFF_D), lambda h, i: (0, h)),
            pl.BlockSpec((seq // blk, DIFF_DV, blk), lambda h, i: (0, h, 0)),
            pl.BlockSpec((1, 3, blk, blk), lambda h, i: (h, 0, 0, 0)),
            pl.BlockSpec((4, DIFF_D), lambda h, i: (0, 0)),
            pl.BlockSpec((DIFF_DV, 1), lambda h, i: (0, 0)),
            pl.BlockSpec(memory_space=pl.ANY),
        ],
        out_specs=pl.BlockSpec((blk, DIFF_DV), lambda h, i: (i, o_col0 + h)),
        out_shape=jax.ShapeDtypeStruct(mix.shape, mix.dtype),
        scratch_shapes=[pltpu.VMEM((2, DIFF_DV, blk), F32), pltpu.VMEM((2, blk, blk), F32),
                        pltpu.VMEM((2, blk, blk), BF16)],
        input_output_aliases={6: 0},
        compiler_params=_params("parallel", "arbitrary"),
        name="attn_prompt",
    )(main, kvb, vt, bias, lam, nw_col, mix)


def _attn_sample_body(q_ref, kn_ref, vn_ref, kc_hbm, vc_hbm, bias_ref, lam_ref, nw_ref, mix_in_ref,
                      o_ref, kbuf, vbuf, sem, *, layer, past):
    del mix_in_ref
    b = pl.program_id(0)
    h = pl.program_id(1)
    n_h = pl.num_programs(1)
    step = b * n_h + h
    slot = step % 2

    def copies(cb, ch, cslot):
        return (pltpu.make_async_copy(kc_hbm.at[layer, cb, :, ch, :], kbuf.at[cslot], sem.at[0, cslot]),
                pltpu.make_async_copy(vc_hbm.at[layer, cb, :, ch, :], vbuf.at[cslot], sem.at[1, cslot]))

    @pl.when(step == 0)
    def _():
        for c in copies(b, h, slot):
            c.start()

    @pl.when(step + 1 < pl.num_programs(0) * n_h)
    def _():
        wrap = h + 1 == n_h
        for c in copies(jnp.where(wrap, b + 1, b), jnp.where(wrap, 0, h + 1), 1 - slot):
            c.start()

    qs = (q_ref[...] * (DIFF_D ** -0.5)).astype(BF16)
    kn = kn_ref[...]
    bias_c = bias_ref[0, 0, :, :past]
    bias_n = bias_ref[0, 0, :, past:]
    for c in copies(b, h, slot):
        c.wait()
    ps = []
    for half in range(2):
        cols = slice(half * DIFF_D, (half + 1) * DIFF_D)
        sc = _dot_nt(qs[:, cols], kbuf[slot, :, cols].astype(BF16)) + bias_c
        sn = _dot_nt(qs[:, cols], kn[:, cols]) + bias_n
        m = jnp.maximum(jnp.max(sc, axis=-1, keepdims=True), jnp.max(sn, axis=-1, keepdims=True))
        pc = jnp.exp(sc - m)
        pn = jnp.exp(sn - m)
        inv = 1.0 / (jnp.sum(pc, axis=-1, keepdims=True) + jnp.sum(pn, axis=-1, keepdims=True))
        ps.append((pc * inv, pn * inv))
    lam, lam_init = _lambda(lam_ref, layer)
    wc = (ps[0][0] - lam * ps[1][0]).astype(BF16)
    wn = (ps[0][1] - lam * ps[1][1]).astype(BF16)
    o = _dot(wc, vbuf[slot].astype(BF16)) + _dot(wn, vn_ref[...])
    o_ref[...] = ((o * _rms_scale(o)) * nw_ref[...] * (1.0 - lam_init)).astype(BF16)


def _attn_sample(main, kvb, cache_k, cache_v, bias, lam, nw, mix, *, layer, row_off, batch, seq):
    past = cache_k.shape[2]
    assert row_off % seq == 0
    rb0 = row_off // seq
    q_col0 = DQ_OFF // (2 * DIFF_D)
    v_col0 = DIFF_QK_W // DIFF_DV
    o_col0 = GLA_V_W // DIFF_DV
    return pl.pallas_call(
        functools.partial(_attn_sample_body, layer=layer, past=past),
        grid=(batch, DIFF_HEADS),
        in_specs=[
            pl.BlockSpec((seq, 2 * DIFF_D), lambda b, h: (rb0 + b, q_col0 + h)),
            pl.BlockSpec((seq, 2 * DIFF_D), lambda b, h: (rb0 + b, h)),
            pl.BlockSpec((seq, DIFF_DV), lambda b, h: (rb0 + b, v_col0 + h)),
            pl.BlockSpec(memory_space=pl.ANY),
            pl.BlockSpec(memory_space=pl.ANY),
            pl.BlockSpec((1, 1, seq, past + seq), lambda b, h: (h, 0, 0, 0)),
            pl.BlockSpec((4, DIFF_D), lambda b, h: (0, 0)),
            pl.BlockSpec((1, DIFF_DV), lambda b, h: (0, 0)),
            pl.BlockSpec(memory_space=pl.ANY),
        ],
        out_specs=pl.BlockSpec((seq, DIFF_DV), lambda b, h: (rb0 + b, o_col0 + h)),
        out_shape=jax.ShapeDtypeStruct(mix.shape, mix.dtype),
        scratch_shapes=[
            pltpu.VMEM((2, past, 2 * DIFF_D), F32),
            pltpu.VMEM((2, past, DIFF_DV), F32),
            pltpu.SemaphoreType.DMA((2, 2)),
        ],
        input_output_aliases={8: 0},
        compiler_params=_params("arbitrary", "arbitrary"),
        name="attn_sample",
    )(main, kvb, kvb, cache_k, cache_v, bias, lam, nw, mix)


def _outproj_body(o_ref, w_ref, x_ref, y_ref):
    y_ref[...] = x_ref[...] + _dot(o_ref[...], w_ref[...])


def _outproj(mix, w, x, layer):
    n, d = x.shape
    tm = _row_tile(n)
    return pl.pallas_call(
        _outproj_body,
        grid=(n // tm,),
        in_specs=[
            pl.BlockSpec((tm, MIX_WIDTH), lambda i: (i, 0)),
            pl.BlockSpec((None, MIX_WIDTH, d), lambda i: (layer, 0, 0)),
            pl.BlockSpec((tm, d), lambda i: (i, 0)),
        ],
        out_specs=pl.BlockSpec((tm, d), lambda i: (i, 0)),
        out_shape=jax.ShapeDtypeStruct((n, d), F32),
        compiler_params=_params("parallel"),
        name="outproj",
    )(mix, w, x)


def kernel(x_prompt, x_sample, cache_k, cache_v, state_gla, ffn1_norm, ffn1_w_gate, ffn1_w_up, ffn1_w_down,
           mix_norm, w_in, gla_w_gk, gla_b_gk, gla_norm, diff_lambda, diff_norm, w_out,
           ffn2_norm, ffn2_w_gate, ffn2_w_up, ffn2_w_down, rel_bias, final_norm):
    pb, ps, d = x_prompt.shape
    sb, ss, _ = x_sample.shape
    depth = w_in.shape[0]
    past = cache_k.shape[2]
    assert pb == 1
    n_p, n_s = pb * ps, sb * ss
    xs = [x_prompt.reshape(n_p, d), x_sample.reshape(n_s, d)]

    blk = _row_tile(n_p + n_s)
    bias_p = _bias_table(rel_bias, tiles=3, rows=blk, cols=blk, q_off=0, k_off=0, k_step=-blk,
                         key_major=True, scale=LOG2E)
    bias_s = _bias_table(rel_bias, tiles=1, rows=ss, cols=past + ss, q_off=past, k_off=0, k_step=0)
    zero_state = jnp.zeros((pb, GLA_HEADS, GLA_DK, GLA_DV), F32)
    row = lambda a: a.reshape(1, -1)

    f1g, f1u, f1d = _cast_bf16(ffn1_w_gate), _cast_bf16(ffn1_w_up), _cast_bf16(ffn1_w_down)
    f2g, f2u, f2d = _cast_bf16(ffn2_w_gate), _cast_bf16(ffn2_w_up), _cast_bf16(ffn2_w_down)
    w_main, w_gz = _cast_w_in(w_in)
    w_o = _cast_bf16(w_out)

    k_rows, v_rows, p_states, s_states = [], [], [], []
    for l in range(depth):
        w_gk = jnp.pad(gla_w_gk[l], ((0, LANE - GLA_GK_RANK), (0, 0))).astype(BF16)

        x, = _ffn(xs, row(ffn1_norm[l]), f1g, f1u, f1d, row(final_norm), l, False, [n_p + n_s])
        main, kvb, vt, g = _inproj(x, row(mix_norm[l]), w_main, w_gz, w_gk, row(gla_b_gk[l]), l)
        kf = main[:, MAIN_W:MAIN_W + DIFF_QK_W]
        vf = main[:, MAIN_W + DIFF_QK_W:]

        mix = jnp.zeros((n_p + n_s, MIX_WIDTH), BF16)
        mix, sp = _gla(main, g, zero_state, row(gla_norm[l]), mix,
                       row_off=0, batch=pb, seq=ps, chunk=CHUNK, rows=min(GLA_ROWS, ps))
        mix, s_s = _gla(main, g, state_gla[l], row(gla_norm[l]), mix,
                        row_off=n_p, batch=sb, seq=ss, chunk=ss, rows=ss)
        mix = _attn_prompt(main, kvb, vt, bias_p, diff_lambda[l], diff_norm[l].reshape(-1, 1), mix,
                           layer=l, seq=ps)
        mix = _attn_sample(main, kvb, cache_k, cache_v, bias_s, diff_lambda[l], row(diff_norm[l]), mix,
                           layer=l, row_off=n_p, batch=sb, seq=ss)

        x = _outproj(mix, w_o, x, l)
        last = l == depth - 1
        xs = _ffn([x], row(ffn2_norm[l]), f2g, f2u, f2d, row(final_norm), l, last,
                  [n_p, n_s] if last else [n_p + n_s])
        k_rows.append(kf)
        v_rows.append(vf)
        p_states.append(sp)
        s_states.append(s_s)

    def rows_of(parts, lo, hi, shape):
        return jnp.stack([p[lo:hi].reshape(shape) for p in parts])

    return (xs[0].reshape(pb, ps, d),
            xs[1].reshape(sb, ss, d),
            rows_of(k_rows, 0, n_p, (pb, ps, DIFF_HEADS, 2 * DIFF_D)),
            rows_of(v_rows, 0, n_p, (pb, ps, DIFF_HEADS, DIFF_DV)),
            jnp.stack(p_states),
            rows_of(k_rows, n_p, n_p + n_s, (sb, ss, DIFF_HEADS, 2 * DIFF_D)),
            rows_of(v_rows, n_p, n_p + n_s, (sb, ss, DIFF_HEADS, DIFF_DV)),
            jnp.stack(s_states))
```

```python
import functools
import math

import jax
import jax.numpy as jnp
from jax import lax
from jax.experimental import pallas as pl
from jax.experimental.pallas import tpu as pltpu

F32 = jnp.float32
BF16 = jnp.bfloat16

EPS = 1e-6
CHUNK = 64
GLA_HEADS = 4
GLA_DK = 128
GLA_DV = 256
GLA_GK_RANK = 16
GLA_GATE_NORMALIZER = 16.0
DIFF_HEADS = 4
DIFF_D = 128
DIFF_DV = 2 * DIFF_D
N_BUCKETS = 32
MAX_DISTANCE = 128

GLA_K_W = GLA_HEADS * GLA_DK
GLA_V_W = GLA_HEADS * GLA_DV
DIFF_QK_W = DIFF_HEADS * 2 * DIFF_D
DIFF_V_W = DIFF_HEADS * DIFF_DV
MIX_WIDTH = GLA_V_W + DIFF_V_W
GZ_OFF = 2 * GLA_K_W + 2 * GLA_V_W
MAIN_W = GZ_OFF + DIFF_QK_W
DQ_OFF = GZ_OFF

LANE = 128
LOG2E = math.log2(math.e)
SUB = 64
VMEM_LIMIT = 56 * 1024 * 1024

ROW_TILE = 512
FF_TILE = 512
IN_TILE = 1024
GLA_ROWS = 256


def _dot(a, b):
    return jnp.dot(a, b, preferred_element_type=F32)


def _dot_nt(a, b):
    return lax.dot_general(a, b, (((1,), (1,)), ((), ())), preferred_element_type=F32)


def _dot_tn(a, b):
    return lax.dot_general(a, b, (((0,), (0,)), ((), ())), preferred_element_type=F32)


def _params(*sem):
    return pltpu.CompilerParams(dimension_semantics=sem, vmem_limit_bytes=VMEM_LIMIT)


def _row_tile(n):
    t = ROW_TILE
    while n % t:
        t //= 2
    return t


def _rms_scale(x):
    return lax.rsqrt(jnp.mean(x * x, axis=-1, keepdims=True) + EPS)


CAST_ROWS = 256


def _cast_body(w_ref, o_ref):
    o_ref[...] = w_ref[...].astype(BF16)


def _cast_bf16(w):
    depth, r, c = w.shape
    spec = pl.BlockSpec((1, CAST_ROWS, c), lambda l, i: (l, i, 0))
    return pl.pallas_call(
        _cast_body,
        grid=(depth, r // CAST_ROWS),
        in_specs=[spec],
        out_specs=spec,
        out_shape=jax.ShapeDtypeStruct(w.shape, BF16),
        compiler_params=_params("parallel", "parallel"),
        name="cast_bf16",
    )(w)


def _cast_w_in_body(w_ref, main_ref, gz_ref):
    w = w_ref[0]
    main_ref[0, :, :GZ_OFF] = w[:, :GZ_OFF].astype(BF16)
    main_ref[0, :, GZ_OFF:] = w[:, GZ_OFF + GLA_GK_RANK:].astype(BF16)
    gz = w[:, GZ_OFF:GZ_OFF + GLA_GK_RANK]
    gz_ref[0] = jnp.concatenate([gz, jnp.zeros((gz.shape[0], LANE - GLA_GK_RANK), F32)], axis=1).astype(BF16)


def _cast_w_in(w_in):
    depth, r, c = w_in.shape
    wide = c - GLA_GK_RANK
    return pl.pallas_call(
        _cast_w_in_body,
        grid=(depth, r // CAST_ROWS),
        in_specs=[pl.BlockSpec((1, CAST_ROWS, c), lambda l, i: (l, i, 0))],
        out_specs=[pl.BlockSpec((1, CAST_ROWS, wide), lambda l, i: (l, i, 0)),
                   pl.BlockSpec((1, CAST_ROWS, LANE), lambda l, i: (l, i, 0))],
        out_shape=[jax.ShapeDtypeStruct((depth, r, wide), BF16),
                   jax.ShapeDtypeStruct((depth, r, LANE), BF16)],
        compiler_params=_params("parallel", "parallel"),
        name="cast_w_in",
    )(w_in)


def _ffn_body(*refs, final, n_in, n_out, split):
    x_refs = refs[:n_in]
    nw_ref, wg_ref, wu_ref, wd_ref, fw_ref = refs[n_in:n_in + 5]
    o_refs = refs[n_in + 5:n_in + 5 + n_out]
    xn_ref, acc_ref = refs[n_in + 5 + n_out:]
    i = pl.program_id(0)
    j = pl.program_id(1)

    def load_x():
        if n_in == 1:
            return x_refs[0][...]
        return jnp.where(i < split, x_refs[0][...], x_refs[1][...])

    @pl.when(j == 0)
    def _():
        x = load_x()
        xn_ref[...] = ((x * _rms_scale(x)) * nw_ref[...]).astype(BF16)
        acc_ref[...] = jnp.zeros_like(acc_ref)

    xn = xn_ref[...]
    h = _dot(xn, wg_ref[...])
    u = _dot(xn, wu_ref[...])
    a = (h * jax.nn.sigmoid(h) * u).astype(BF16)
    acc_ref[...] += _dot(a, wd_ref[...])

    @pl.when(j == pl.num_programs(1) - 1)
    def _():
        y = load_x() + 0.5 * acc_ref[...]
        if final:
            y = (y * _rms_scale(y)) * fw_ref[...]
        if n_out == 1:
            o_refs[0][...] = y
        else:
            @pl.when(i < split)
            def _():
                o_refs[0][...] = y

            @pl.when(i >= split)
            def _():
                o_refs[1][...] = y


def _ffn(xs, nw, wg, wu, wd, fw, layer, final, out_rows):
    d = xs[0].shape[1]
    n = sum(x.shape[0] for x in xs)
    assert n == sum(out_rows)
    f = wg.shape[2]
    tm = _row_tile(n)
    first = xs[0].shape[0] if len(xs) == 2 else out_rows[0]
    assert first % tm == 0
    split = first // tm
    parts = [lambda i, j: (jnp.minimum(i, split - 1), 0), lambda i, j: (jnp.maximum(i - split, 0), 0)]
    whole = [lambda i, j: (i, 0)]
    outs = pl.pallas_call(
        functools.partial(_ffn_body, final=final, n_in=len(xs), n_out=len(out_rows), split=split),
        grid=(n // tm, f // FF_TILE),
        in_specs=[pl.BlockSpec((tm, d), m) for m in (parts if len(xs) == 2 else whole)] + [
            pl.BlockSpec((1, d), lambda i, j: (0, 0)),
            pl.BlockSpec((None, d, FF_TILE), lambda i, j: (layer, 0, j)),
            pl.BlockSpec((None, d, FF_TILE), lambda i, j: (layer, 0, j)),
            pl.BlockSpec((None, FF_TILE, d), lambda i, j: (layer, j, 0)),
            pl.BlockSpec((1, d), lambda i, j: (0, 0)),
        ],
        out_specs=[pl.BlockSpec((tm, d), m) for m in (parts if len(out_rows) == 2 else whole)],
        out_shape=[jax.ShapeDtypeStruct((r, d), F32) for r in out_rows],
        scratch_shapes=[pltpu.VMEM((tm, d), BF16), pltpu.VMEM((tm, d), F32)],
        compiler_params=_params("arbitrary" if len(out_rows) == 2 else "parallel", "arbitrary"),
        name="ffn",
    )(*xs, nw, wg, wu, wd, fw)
    return outs


N_MAIN_BLK = MAIN_W // IN_TILE
N_K_BLK = DIFF_QK_W // IN_TILE
N_V_BLK = DIFF_V_W // IN_TILE


assert N_K_BLK == 1 and N_V_BLK == 1


def _inproj_body(x_ref, nw_ref, w_ref, wgz_ref, wgk_ref, bgk_ref, pk_in, pv_in, sk_in, sv_in,
                 main_ref, kvb_ref, vt_ref, g_ref, pk_ref, pv_ref, sk_ref, sv_ref,
                 xn_ref, y_ref, sem, *, layer, split, seq_s):
    del pk_in, pv_in, sk_in, sv_in
    i = pl.program_id(0)
    j = pl.program_id(1)
    n_i = pl.num_programs(0)
    tm = x_ref.shape[0]

    @pl.when(j == 0)
    def _():
        x = x_ref[...]
        xn = ((x * _rms_scale(x)) * nw_ref[...]).astype(BF16)
        xn_ref[...] = xn
        gz = _dot(xn, wgz_ref[...])
        z = _dot(gz.astype(BF16), wgk_ref[...]) + bgk_ref[...]
        g_ref[...] = (jnp.minimum(z, 0.0) - jnp.log1p(jnp.exp(-jnp.abs(z)))) * (1.0 / GLA_GATE_NORMALIZER)

    y = _dot(xn_ref[...], w_ref[...])

    @pl.when(j < N_MAIN_BLK)
    def _():
        main_ref[...] = y

    def row_copies(slot, tile, dst_p, dst_s, act):
        heads = [slice(h * DIFF_DV, (h + 1) * DIFF_DV) for h in range(DIFF_HEADS)]

        @pl.when(tile < split)
        def _():
            for h, cols in enumerate(heads):
                act(pltpu.make_async_copy(y_ref.at[slot, :, cols],
                                          dst_p.at[layer, 0, pl.ds(tile * tm, tm), h, :], sem.at[slot]))

        @pl.when(tile >= split)
        def _():
            for b in range(tm // seq_s):
                for h, cols in enumerate(heads):
                    act(pltpu.make_async_copy(y_ref.at[slot, b * seq_s:(b + 1) * seq_s, cols],
                                              dst_s.at[layer, (tile - split) * (tm // seq_s) + b, :, h, :],
                                              sem.at[slot]))

    def kv_step(slot, dst_p, dst_s):
        @pl.when(i > 0)
        def _():
            row_copies(slot, i - 1, dst_p, dst_s, lambda c: c.wait())

        y_ref[slot] = y
        kvb_ref[...] = y.astype(BF16)
        if slot == 1:
            vt_ref[0] = y.T.astype(BF16)
        row_copies(slot, i, dst_p, dst_s, lambda c: c.start())

        @pl.when(i == n_i - 1)
        def _():
            row_copies(slot, i, dst_p, dst_s, lambda c: c.wait())

    @pl.when(j == N_MAIN_BLK)
    def _():
        kv_step(0, pk_ref, sk_ref)

    @pl.when(j == N_MAIN_BLK + 1)
    def _():
        kv_step(1, pv_ref, sv_ref)


def _inproj(x, nw, w_main, w_gz, w_gk, b_gk, kv_out, *, layer, split, seq_s):
    n, d = x.shape
    tm = _row_tile(n)
    assert tm % seq_s == 0 and kv_out[0].shape[1] == 1
    nj = N_MAIN_BLK + 2
    k0 = N_MAIN_BLK
    anyspace = pl.BlockSpec(memory_space=pl.ANY)
    outs = pl.pallas_call(
        functools.partial(_inproj_body, layer=layer, split=split, seq_s=seq_s),
        grid=(n // tm, nj),
        in_specs=[
            pl.BlockSpec((tm, d), lambda i, j: (i, 0)),
            pl.BlockSpec((1, d), lambda i, j: (0, 0)),
            pl.BlockSpec((None, d, IN_TILE), lambda i, j: (layer, 0, j)),
            pl.BlockSpec((None, d, LANE), lambda i, j: (layer, 0, 0)),
            pl.BlockSpec((LANE, GLA_K_W), lambda i, j: (0, 0)),
            pl.BlockSpec((1, GLA_K_W), lambda i, j: (0, 0)),
            anyspace, anyspace, anyspace, anyspace,
        ],
        out_specs=[
            pl.BlockSpec((tm, IN_TILE), lambda i, j: (i, jnp.minimum(j, k0 - 1))),
            pl.BlockSpec((tm, IN_TILE), lambda i, j: (i, jnp.clip(j - k0, 0, 1))),
            pl.BlockSpec((1, IN_TILE, tm), lambda i, j: (i, 0, 0)),
            pl.BlockSpec((tm, GLA_K_W), lambda i, j: (i, 0)),
            anyspace, anyspace, anyspace, anyspace,
        ],
        out_shape=[
            jax.ShapeDtypeStruct((n, MAIN_W), F32),
            jax.ShapeDtypeStruct((n, DIFF_QK_W + DIFF_V_W), BF16),
            jax.ShapeDtypeStruct((n // tm, DIFF_V_W, tm), BF16),
            jax.ShapeDtypeStruct((n, GLA_K_W), F32),
        ] + [jax.ShapeDtypeStruct(a.shape, a.dtype) for a in kv_out],
        scratch_shapes=[pltpu.VMEM((tm, d), BF16), pltpu.VMEM((2, tm, IN_TILE), F32),
                        pltpu.SemaphoreType.DMA((2,))],
        input_output_aliases={6: 4, 7: 5, 8: 6, 9: 7},
        compiler_params=_params("arbitrary", "arbitrary"),
        name="inproj",
    )(x, nw, w_main, w_gz, w_gk, b_gk, *kv_out)
    return outs[:4], outs[4:]


def _split3(a):
    hi = a.astype(BF16)
    r = a - hi.astype(F32)
    mid = r.astype(BF16)
    lo = (r - mid.astype(F32)).astype(BF16)
    return hi, mid, lo


def _gla_body(q_ref, k_ref, v_ref, r_ref, g_ref, s0_ref, nw_ref, mix_in_ref, o_ref, s_ref, st_ref, *, chunk):
    del mix_in_ref
    t = pl.program_id(1)
    rows = q_ref.shape[0]

    @pl.when(t == 0)
    def _():
        for h in range(GLA_HEADS):
            st_ref[h] = s0_ref[0, h].T

    ri = lax.broadcasted_iota(jnp.int32, (chunk, chunk), 0)
    ci = lax.broadcasted_iota(jnp.int32, (chunk, chunk), 1)
    causal = ci <= ri
    tri = causal.astype(BF16)

    for c in range(rows // chunk):
        sl = pl.ds(c * chunk, chunk)
        for h in range(GLA_HEADS):
            kc = slice(h * GLA_DK, (h + 1) * GLA_DK)
            vc = slice(h * GLA_DV, (h + 1) * GLA_DV)
            q = q_ref[sl, kc] * (GLA_DK ** -0.5)
            k = k_ref[sl, kc]
            v = v_ref[sl, vc].astype(BF16)
            g_hi, g_mid, g_lo = _split3(g_ref[sl, kc])
            b = _dot(tri, g_hi) + _dot(tri, g_mid) + _dot(tri, g_lo)
            b_last = b[chunk - 1:chunk, :]
            qe = (q * jnp.exp(b)).astype(BF16)
            ke = (k * jnp.exp(-b)).astype(BF16)
            kd = (k * jnp.exp(b_last - b)).astype(BF16)
            a = jnp.where(causal, _dot_nt(qe, ke), 0.0).astype(BF16)
            st = st_ref[h]
            o = _dot_nt(qe, st.astype(BF16)) + _dot(a, v)
            st_ref[h] = st * jnp.exp(b_last) + _dot_tn(v, kd)
            o = (o * _rms_scale(o)) * nw_ref[...]
            r = r_ref[sl, vc]
            o_ref[sl, vc] = (o * (r * jax.nn.sigmoid(r))).astype(BF16)

    @pl.when(t == pl.num_programs(1) - 1)
    def _():
        for h in range(GLA_HEADS):
            s_ref[0, h] = st_ref[h].T


def _gla(main, g, s0, nw, mix, *, row_off, batch, seq, chunk, rows):
    assert seq % rows == 0 and rows % chunk == 0 and row_off % rows == 0
    nt = seq // rows
    rb0 = row_off // rows

    def rowblk(b, t):
        return rb0 + b * nt + t

    state_spec = pl.BlockSpec((1, GLA_HEADS, GLA_DK, GLA_DV), lambda b, t: (b, 0, 0, 0))
    out, s = pl.pallas_call(
        functools.partial(_gla_body, chunk=chunk),
        grid=(batch, nt),
        in_specs=[
            pl.BlockSpec((rows, GLA_K_W), lambda b, t: (rowblk(b, t), 0)),
            pl.BlockSpec((rows, GLA_K_W), lambda b, t: (rowblk(b, t), 1)),
            pl.BlockSpec((rows, GLA_V_W), lambda b, t: (rowblk(b, t), 2 * GLA_K_W // GLA_V_W)),
            pl.BlockSpec((rows, GLA_V_W), lambda b, t: (rowblk(b, t), 2 * GLA_K_W // GLA_V_W + 1)),
            pl.BlockSpec((rows, GLA_K_W), lambda b, t: (rowblk(b, t), 0)),
            state_spec,
            pl.BlockSpec((1, GLA_DV), lambda b, t: (0, 0)),
            pl.BlockSpec(memory_space=pl.ANY),
        ],
        out_specs=[
            pl.BlockSpec((rows, GLA_V_W), lambda b, t: (rowblk(b, t), 0)),
            state_spec,
        ],
        out_shape=[
            jax.ShapeDtypeStruct(mix.shape, mix.dtype),
            jax.ShapeDtypeStruct((batch, GLA_HEADS, GLA_DK, GLA_DV), F32),
        ],
        scratch_shapes=[pltpu.VMEM((GLA_HEADS, GLA_DV, GLA_DK), F32)],
        input_output_aliases={7: 0},
        compiler_params=_params("parallel", "arbitrary"),
        name="gla",
    )(main, main, main, main, g, s0, nw, mix)
    return out, s


def _bucket_thresholds():
    half = N_BUCKETS // 2
    m = half // 2
    e = half - m
    thr = []
    for kk in range(1, e):
        n = m
        while n ** e * m ** kk < m ** e * MAX_DISTANCE ** kk:
            n += 1
        thr.append(n)
    return tuple(thr)


_BUCKET_THR = _bucket_thresholds()


def _t5_bucket(rel):
    half = N_BUCKETS // 2
    max_exact = half // 2
    n = jnp.abs(rel)
    large = jnp.full(rel.shape, max_exact, jnp.int32)
    for thr in _BUCKET_THR:
        large = large + (n >= thr).astype(jnp.int32)
    return jnp.where(rel > 0, half, 0) + jnp.where(n < max_exact, n, large)


def _bias_body(rb_ref, o_ref, *, q_off, k_off, k_step, key_major, scale):
    h = pl.program_id(0)
    t = pl.program_id(1)
    shape = o_ref.shape[2:]
    qpos = q_off + lax.broadcasted_iota(jnp.int32, shape, 1 if key_major else 0)
    kpos = k_off + t * k_step + lax.broadcasted_iota(jnp.int32, shape, 0 if key_major else 1)
    bucket = _t5_bucket(kpos - qpos)
    acc = jnp.zeros(shape, F32)
    for bkt in range(N_BUCKETS):
        acc = jnp.where(bucket == bkt, rb_ref[bkt, h], acc)
    visible = (kpos >> 6) <= (qpos >> 6)
    o_ref[0, 0] = jnp.where(visible, acc * scale, -jnp.inf)


assert CHUNK == 64


def _bias_table(rel_bias, *, tiles, rows, cols, q_off, k_off, k_step, key_major=False, scale=1.0):
    return pl.pallas_call(
        functools.partial(_bias_body, q_off=q_off, k_off=k_off, k_step=k_step, key_major=key_major,
                          scale=scale),
        grid=(DIFF_HEADS, tiles),
        in_specs=[pl.BlockSpec(memory_space=pltpu.SMEM)],
        out_specs=pl.BlockSpec((1, 1, rows, cols), lambda h, t: (h, t, 0, 0)),
        out_shape=jax.ShapeDtypeStruct((DIFF_HEADS, tiles, rows, cols), F32),
        compiler_params=_params("parallel", "parallel"),
        name="bias_table",
    )(rel_bias)


def _lambda(lam_ref, layer):
    lam = lam_ref[...]
    a = jnp.sum(lam[0:1, :] * lam[1:2, :], axis=-1, keepdims=True)
    b = jnp.sum(lam[2:3, :] * lam[3:4, :], axis=-1, keepdims=True)
    lam_init = 0.8 - 0.6 * math.exp(-0.3 * layer)
    return jnp.exp(a) - jnp.exp(b) + lam_init, lam_init


def _diff_finish(o1, l1, o2, l2, lam, lam_init, nw):
    o = o1 / l1 - lam * (o2 / l2)
    return ((o * _rms_scale(o)) * nw * (1.0 - lam_init)).astype(BF16)


def _attn_prompt_body(q_ref, k_ref, vt_ref, bias_ref, lam_ref, nw_ref, mix_in_ref,
                      o_ref, acc_ref, s_ref, p_ref, *, layer, blk):
    del mix_in_ref
    qi = pl.program_id(1)
    qt = (q_ref[...] * (DIFF_D ** -0.5 * LOG2E)).T.astype(BF16)
    qth = (qt[:DIFF_D], qt[DIFF_D:])
    acc_ref[...] = jnp.zeros_like(acc_ref)
    strips = [slice(c * SUB, (c + 1) * SUB) for c in range(blk // SUB)]

    def fold(a):
        return a.reshape(SUB // 8, 8, blk)

    def step(kj, stats):
        kb = k_ref[pl.ds(pl.multiple_of(kj * blk, blk), blk), :]
        vt = vt_ref[kj]
        t = jnp.minimum(qi - kj, 2)
        for half in range(2):
            s_ref[half] = _dot(kb[:, half * DIFF_D:(half + 1) * DIFF_D], qth[half]) + bias_ref[0, t]
        out = []
        for half in range(2):
            m, l = stats[half]
            bm = jnp.max(fold(s_ref[half, strips[0], :]), axis=0)
            for c in strips[1:]:
                bm = jnp.maximum(bm, jnp.max(fold(s_ref[half, c, :]), axis=0))
            m_new = jnp.maximum(m, jnp.max(bm, axis=0, keepdims=True))
            alpha = jnp.exp2(m - m_new)
            ls = jnp.zeros((8, blk), F32)
            for c in strips:
                p = jnp.exp2(s_ref[half, c, :] - m_new)
                ls = ls + jnp.sum(fold(p), axis=0)
                p_ref[half, c, :] = p.astype(BF16)
            l = alpha * l + jnp.sum(ls, axis=0, keepdims=True)
            acc_ref[half] = alpha * acc_ref[half] + _dot(vt, p_ref[half])
            out.append((m_new, l))
        return tuple(out)

    init = tuple((jnp.full((1, blk), -jnp.inf, F32), jnp.zeros((1, blk), F32)) for _ in range(2))
    stats = lax.fori_loop(0, qi + 1, step, init)

    lam, lam_init = _lambda(lam_ref, layer)
    ot = acc_ref[0] / stats[0][1] - lam * (acc_ref[1] / stats[1][1])
    scale = lax.rsqrt(jnp.mean(ot * ot, axis=0, keepdims=True) + EPS)
    ot = (ot * scale) * nw_ref[...] * (1.0 - lam_init)
    o_ref[...] = ot.T.astype(BF16)


def _attn_prompt(main, kvb, vt, bias, lam, nw_col, mix, *, layer, seq):
    blk = vt.shape[2]
    assert seq % blk == 0 and blk >= MAX_DISTANCE and blk % CHUNK == 0
    q_col0 = DQ_OFF // (2 * DIFF_D)
    o_col0 = GLA_V_W // DIFF_DV
    return pl.pallas_call(
        functools.partial(_attn_prompt_body, layer=layer, blk=blk),
        grid=(DIFF_HEADS, seq // blk),
        in_specs=[
            pl.BlockSpec((blk, 2 * DIFF_D), lambda h, i: (i, q_col0 + h)),
            pl.BlockSpec((seq, 2 * DIFF_D), lambda h, i: (0, h)),
            pl.BlockSpec((seq // blk, DIFF_DV, blk), lambda h, i: (0, h, 0)),
            pl.BlockSpec((1, 3, blk, blk), lambda h, i: (h, 0, 0, 0)),
            pl.BlockSpec((4, DIFF_D), lambda h, i: (0, 0)),
            pl.BlockSpec((DIFF_DV, 1), lambda h, i: (0, 0)),
            pl.BlockSpec(memory_space=pl.ANY),
        ],
        out_specs=pl.BlockSpec((blk, DIFF_DV), lambda h, i: (i, o_col0 + h)),
        out_shape=jax.ShapeDtypeStruct(mix.shape, mix.dtype),
        scratch_shapes=[pltpu.VMEM((2, DIFF_DV, blk), F32), pltpu.VMEM((2, blk, blk), F32),
                        pltpu.VMEM((2, blk, blk), BF16)],
        input_output_aliases={6: 0},
        compiler_params=_params("parallel", "arbitrary"),
        name="attn_prompt",
    )(main, kvb, vt, bias, lam, nw_col, mix)


def _attn_sample_body(q_ref, kn_ref, vn_ref, kc_hbm, vc_hbm, bias_ref, lam_ref, nw_ref, mix_in_ref,
                      o_ref, kbuf, vbuf, sem, *, layer, past):
    del mix_in_ref
    b = pl.program_id(0)
    h = pl.program_id(1)
    n_h = pl.num_programs(1)
    step = b * n_h + h
    slot = step % 2

    def copies(cb, ch, cslot):
        return (pltpu.make_async_copy(kc_hbm.at[layer, cb, :, ch, :], kbuf.at[cslot], sem.at[0, cslot]),
                pltpu.make_async_copy(vc_hbm.at[layer, cb, :, ch, :], vbuf.at[cslot], sem.at[1, cslot]))

    @pl.when(step == 0)
    def _():
        for c in copies(b, h, slot):
            c.start()

    @pl.when(step + 1 < pl.num_programs(0) * n_h)
    def _():
        wrap = h + 1 == n_h
        for c in copies(jnp.where(wrap, b + 1, b), jnp.where(wrap, 0, h + 1), 1 - slot):
            c.start()

    qs = (q_ref[...] * (DIFF_D ** -0.5)).astype(BF16)
    kn = kn_ref[...]
    bias_c = bias_ref[0, 0, :, :past]
    bias_n = bias_ref[0, 0, :, past:]
    for c in copies(b, h, slot):
        c.wait()
    ps = []
    for half in range(2):
        cols = slice(half * DIFF_D, (half + 1) * DIFF_D)
        sc = _dot_nt(qs[:, cols], kbuf[slot, :, cols].astype(BF16)) + bias_c
        sn = _dot_nt(qs[:, cols], kn[:, cols]) + bias_n
        m = jnp.maximum(jnp.max(sc, axis=-1, keepdims=True), jnp.max(sn, axis=-1, keepdims=True))
        pc = jnp.exp(sc - m)
        pn = jnp.exp(sn - m)
        inv = 1.0 / (jnp.sum(pc, axis=-1, keepdims=True) + jnp.sum(pn, axis=-1, keepdims=True))
        ps.append((pc * inv, pn * inv))
    lam, lam_init = _lambda(lam_ref, layer)
    wc = (ps[0][0] - lam * ps[1][0]).astype(BF16)
    wn = (ps[0][1] - lam * ps[1][1]).astype(BF16)
    o = _dot(wc, vbuf[slot].astype(BF16)) + _dot(wn, vn_ref[...])
    o_ref[...] = ((o * _rms_scale(o)) * nw_ref[...] * (1.0 - lam_init)).astype(BF16)


def _attn_sample(main, kvb, cache_k, cache_v, bias, lam, nw, mix, *, layer, row_off, batch, seq):
    past = cache_k.shape[2]
    assert row_off % seq == 0
    rb0 = row_off // seq
    q_col0 = DQ_OFF // (2 * DIFF_D)
    v_col0 = DIFF_QK_W // DIFF_DV
    o_col0 = GLA_V_W // DIFF_DV
    return pl.pallas_call(
        functools.partial(_attn_sample_body, layer=layer, past=past),
        grid=(batch, DIFF_HEADS),
        in_specs=[
            pl.BlockSpec((seq, 2 * DIFF_D), lambda b, h: (rb0 + b, q_col0 + h)),
            pl.BlockSpec((seq, 2 * DIFF_D), lambda b, h: (rb0 + b, h)),
            pl.BlockSpec((seq, DIFF_DV), lambda b, h: (rb0 + b, v_col0 + h)),
            pl.BlockSpec(memory_space=pl.ANY),
            pl.BlockSpec(memory_space=pl.ANY),
            pl.BlockSpec((1, 1, seq, past + seq), lambda b, h: (h, 0, 0, 0)),
            pl.BlockSpec((4, DIFF_D), lambda b, h: (0, 0)),
            pl.BlockSpec((1, DIFF_DV), lambda b, h: (0, 0)),
            pl.BlockSpec(memory_space=pl.ANY),
        ],
        out_specs=pl.BlockSpec((seq, DIFF_DV), lambda b, h: (rb0 + b, o_col0 + h)),
        out_shape=jax.ShapeDtypeStruct(mix.shape, mix.dtype),
        scratch_shapes=[
            pltpu.VMEM((2, past, 2 * DIFF_D), F32),
            pltpu.VMEM((2, past, DIFF_DV), F32),
            pltpu.SemaphoreType.DMA((2, 2)),
        ],
        input_output_aliases={8: 0},
        compiler_params=_params("arbitrary", "arbitrary"),
        name="attn_sample",
    )(main, kvb, kvb, cache_k, cache_v, bias, lam, nw, mix)


def _outproj_body(o_ref, w_ref, x_ref, y_ref):
    y_ref[...] = x_ref[...] + _dot(o_ref[...], w_ref[...])


def _outproj(mix, w, x, layer):
    n, d = x.shape
    tm = _row_tile(n)
    return pl.pallas_call(
        _outproj_body,
        grid=(n // tm,),
        in_specs=[
            pl.BlockSpec((tm, MIX_WIDTH), lambda i: (i, 0)),
            pl.BlockSpec((None, MIX_WIDTH, d), lambda i: (layer, 0, 0)),
            pl.BlockSpec((tm, d), lambda i: (i, 0)),
        ],
        out_specs=pl.BlockSpec((tm, d), lambda i: (i, 0)),
        out_shape=jax.ShapeDtypeStruct((n, d), F32),
        compiler_params=_params("parallel"),
        name="outproj",
    )(mix, w, x)


def kernel(x_prompt, x_sample, cache_k, cache_v, state_gla, ffn1_norm, ffn1_w_gate, ffn1_w_up, ffn1_w_down,
           mix_norm, w_in, gla_w_gk, gla_b_gk, gla_norm, diff_lambda, diff_norm, w_out,
           ffn2_norm, ffn2_w_gate, ffn2_w_up, ffn2_w_down, rel_bias, final_norm):
    pb, ps, d = x_prompt.shape
    sb, ss, _ = x_sample.shape
    depth = w_in.shape[0]
    past = cache_k.shape[2]
    assert pb == 1
    n_p, n_s = pb * ps, sb * ss
    xs = [x_prompt.reshape(n_p, d), x_sample.reshape(n_s, d)]

    blk = _row_tile(n_p + n_s)
    bias_p = _bias_table(rel_bias, tiles=3, rows=blk, cols=blk, q_off=0, k_off=0, k_step=-blk,
                         key_major=True, scale=LOG2E)
    bias_s = _bias_table(rel_bias, tiles=1, rows=ss, cols=past + ss, q_off=past, k_off=0, k_step=0)
    zero_state = jnp.zeros((pb, GLA_HEADS, GLA_DK, GLA_DV), F32)
    row = lambda a: a.reshape(1, -1)

    f1g, f1u, f1d = _cast_bf16(ffn1_w_gate), _cast_bf16(ffn1_w_up), _cast_bf16(ffn1_w_down)
    f2g, f2u, f2d = _cast_bf16(ffn2_w_gate), _cast_bf16(ffn2_w_up), _cast_bf16(ffn2_w_down)
    w_main, w_gz = _cast_w_in(w_in)
    w_o = _cast_bf16(w_out)

    kv_out = [jnp.zeros((depth, pb, ps, DIFF_HEADS, DIFF_DV), F32) for _ in range(2)]
    kv_out += [jnp.zeros((depth, sb, ss, DIFF_HEADS, DIFF_DV), F32) for _ in range(2)]
    p_states, s_states = [], []
    for l in range(depth):
        w_gk = jnp.pad(gla_w_gk[l], ((0, LANE - GLA_GK_RANK), (0, 0))).astype(BF16)

        x, = _ffn(xs, row(ffn1_norm[l]), f1g, f1u, f1d, row(final_norm), l, False, [n_p + n_s])
        (main, kvb, vt, g), kv_out = _inproj(x, row(mix_norm[l]), w_main, w_gz, w_gk, row(gla_b_gk[l]), kv_out,
                                             layer=l, split=n_p // blk, seq_s=ss)

        mix = jnp.zeros((n_p + n_s, MIX_WIDTH), BF16)
        mix, sp = _gla(main, g, zero_state, row(gla_norm[l]), mix,
                       row_off=0, batch=pb, seq=ps, chunk=CHUNK, rows=min(GLA_ROWS, ps))
        mix, s_s = _gla(main, g, state_gla[l], row(gla_norm[l]), mix,
                        row_off=n_p, batch=sb, seq=ss, chunk=ss, rows=ss)
        mix = _attn_prompt(main, kvb, vt, bias_p, diff_lambda[l], diff_norm[l].reshape(-1, 1), mix,
                           layer=l, seq=ps)
        mix = _attn_sample(main, kvb, cache_k, cache_v, bias_s, diff_lambda[l], row(diff_norm[l]), mix,
                           layer=l, row_off=n_p, batch=sb, seq=ss)

        x = _outproj(mix, w_o, x, l)
        last = l == depth - 1
        xs = _ffn([x], row(ffn2_norm[l]), f2g, f2u, f2d, row(final_norm), l, last,
                  [n_p, n_s] if last else [n_p + n_s])
        p_states.append(sp)
        s_states.append(s_s)

    prompt_k, prompt_v, sample_k, sample_v = kv_out
    return (xs[0].reshape(pb, ps, d), xs[1].reshape(sb, ss, d), prompt_k, prompt_v, jnp.stack(p_states),
            sample_k, sample_v, jnp.stack(s_states))
```

```python
import functools
import math

import jax
import jax.numpy as jnp
from jax import lax
from jax.experimental import pallas as pl
from jax.experimental.pallas import tpu as pltpu

F32 = jnp.float32
BF16 = jnp.bfloat16

EPS = 1e-6
CHUNK = 64
GLA_HEADS = 4
GLA_DK = 128
GLA_DV = 256
GLA_GK_RANK = 16
GLA_GATE_NORMALIZER = 16.0
DIFF_HEADS = 4
DIFF_D = 128
DIFF_DV = 2 * DIFF_D
N_BUCKETS = 32
MAX_DISTANCE = 128

GLA_K_W = GLA_HEADS * GLA_DK
GLA_V_W = GLA_HEADS * GLA_DV
DIFF_QK_W = DIFF_HEADS * 2 * DIFF_D
DIFF_V_W = DIFF_HEADS * DIFF_DV
MIX_WIDTH = GLA_V_W + DIFF_V_W
GZ_OFF = 2 * GLA_K_W + 2 * GLA_V_W
MAIN_W = GZ_OFF + DIFF_QK_W
DQ_OFF = GZ_OFF

LANE = 128
LOG2E = math.log2(math.e)
SUB = 64
VMEM_LIMIT = 56 * 1024 * 1024

ROW_TILE = 512
FF_TILE = 512
IN_TILE = 1024
GLA_ROWS = 256
SAMPLE_SLOTS = 3


def _dot(a, b):
    return jnp.dot(a, b, preferred_element_type=F32)


def _dot_nt(a, b):
    return lax.dot_general(a, b, (((1,), (1,)), ((), ())), preferred_element_type=F32)


def _dot_tn(a, b):
    return lax.dot_general(a, b, (((0,), (0,)), ((), ())), preferred_element_type=F32)


def _params(*sem):
    return pltpu.CompilerParams(dimension_semantics=sem, vmem_limit_bytes=VMEM_LIMIT)


def _row_tile(n):
    t = ROW_TILE
    while n % t:
        t //= 2
    return t


def _rms_scale(x):
    return lax.rsqrt(jnp.mean(x * x, axis=-1, keepdims=True) + EPS)


CAST_ROWS = 256


def _cast_body(w_ref, o_ref):
    o_ref[...] = w_ref[...].astype(BF16)


def _cast_bf16(w):
    depth, r, c = w.shape
    spec = pl.BlockSpec((1, CAST_ROWS, c), lambda l, i: (l, i, 0))
    return pl.pallas_call(
        _cast_body,
        grid=(depth, r // CAST_ROWS),
        in_specs=[spec],
        out_specs=spec,
        out_shape=jax.ShapeDtypeStruct(w.shape, BF16),
        compiler_params=_params("parallel", "parallel"),
        name="cast_bf16",
    )(w)


GZ_BLK = GZ_OFF // CAST_ROWS
assert GZ_OFF % CAST_ROWS == 0 and GLA_GK_RANK % 16 == 0 and GLA_GK_RANK < CAST_ROWS


def _cast_w_in_body(a_ref, b_ref, main_ref, gz_ref):
    k = pl.program_id(1)

    @pl.when(k < GZ_BLK)
    def _():
        main_ref[0] = a_ref[0].astype(BF16)

    @pl.when(k >= GZ_BLK)
    def _():
        main_ref[0, :CAST_ROWS - GLA_GK_RANK] = a_ref[0, GLA_GK_RANK:].astype(BF16)
        main_ref[0, CAST_ROWS - GLA_GK_RANK:] = b_ref[0, :GLA_GK_RANK].astype(BF16)

    @pl.when(k == GZ_BLK)
    def _():
        gz_ref[0, :GLA_GK_RANK] = a_ref[0, :GLA_GK_RANK].astype(BF16)
        gz_ref[0, GLA_GK_RANK:] = jnp.zeros((LANE - GLA_GK_RANK, a_ref.shape[2]), BF16)


def _cast_w_in(w_in_t):
    depth, c, r = w_in_t.shape
    wide = c - GLA_GK_RANK
    last = pl.cdiv(c, CAST_ROWS) - 1
    return pl.pallas_call(
        _cast_w_in_body,
        grid=(depth, wide // CAST_ROWS),
        in_specs=[pl.BlockSpec((1, CAST_ROWS, r), lambda l, k: (l, k, 0)),
                  pl.BlockSpec((1, CAST_ROWS, r), lambda l, k: (l, jnp.minimum(k + 1, last), 0))],
        out_specs=[pl.BlockSpec((1, CAST_ROWS, r), lambda l, k: (l, k, 0)),
                   pl.BlockSpec((1, LANE, r), lambda l, k: (l, 0, 0))],
        out_shape=[jax.ShapeDtypeStruct((depth, wide, r), BF16),
                   jax.ShapeDtypeStruct((depth, LANE, r), BF16)],
        compiler_params=_params("parallel", "arbitrary"),
        name="cast_w_in",
    )(w_in_t, w_in_t)


def _ffn_body(*refs, final, n_in, n_out, split):
    x_refs = refs[:n_in]
    nw_ref, wg_ref, wu_ref, wd_ref, fw_ref = refs[n_in:n_in + 5]
    o_refs = refs[n_in + 5:n_in + 5 + n_out]
    xn_ref, acc_ref = refs[n_in + 5 + n_out:]
    i = pl.program_id(0)
    j = pl.program_id(1)

    def load_x():
        if n_in == 1:
            return x_refs[0][...]
        return jnp.where(i < split, x_refs[0][...], x_refs[1][...])

    @pl.when(j == 0)
    def _():
        x = load_x()
        xn_ref[...] = ((x * _rms_scale(x)) * nw_ref[...]).astype(BF16)
        acc_ref[...] = jnp.zeros_like(acc_ref)

    xn = xn_ref[...]
    h = _dot(xn, wg_ref[...])
    u = _dot(xn, wu_ref[...])
    a = (h * jax.nn.sigmoid(h) * u).astype(BF16)
    acc_ref[...] += _dot(a, wd_ref[...])

    @pl.when(j == pl.num_programs(1) - 1)
    def _():
        y = load_x() + 0.5 * acc_ref[...]
        if final:
            y = (y * _rms_scale(y)) * fw_ref[...]
        if n_out == 1:
            o_refs[0][...] = y
        else:
            @pl.when(i < split)
            def _():
                o_refs[0][...] = y

            @pl.when(i >= split)
            def _():
                o_refs[1][...] = y


def _ffn(xs, nw, wg, wu, wd, fw, layer, final, out_rows):
    d = xs[0].shape[1]
    n = sum(x.shape[0] for x in xs)
    assert n == sum(out_rows)
    f = wg.shape[2]
    tm = _row_tile(n)
    first = xs[0].shape[0] if len(xs) == 2 else out_rows[0]
    assert first % tm == 0
    split = first // tm
    parts = [lambda i, j: (jnp.minimum(i, split - 1), 0), lambda i, j: (jnp.maximum(i - split, 0), 0)]
    whole = [lambda i, j: (i, 0)]
    outs = pl.pallas_call(
        functools.partial(_ffn_body, final=final, n_in=len(xs), n_out=len(out_rows), split=split),
        grid=(n // tm, f // FF_TILE),
        in_specs=[pl.BlockSpec((tm, d), m) for m in (parts if len(xs) == 2 else whole)] + [
            pl.BlockSpec((1, d), lambda i, j: (0, 0)),
            pl.BlockSpec((None, d, FF_TILE), lambda i, j: (layer, 0, j)),
            pl.BlockSpec((None, d, FF_TILE), lambda i, j: (layer, 0, j)),
            pl.BlockSpec((None, FF_TILE, d), lambda i, j: (layer, j, 0)),
            pl.BlockSpec((1, d), lambda i, j: (0, 0)),
        ],
        out_specs=[pl.BlockSpec((tm, d), m) for m in (parts if len(out_rows) == 2 else whole)],
        out_shape=[jax.ShapeDtypeStruct((r, d), F32) for r in out_rows],
        scratch_shapes=[pltpu.VMEM((tm, d), BF16), pltpu.VMEM((tm, d), F32)],
        compiler_params=_params("arbitrary" if len(out_rows) == 2 else "parallel", "arbitrary"),
        name="ffn",
    )(*xs, nw, wg, wu, wd, fw)
    return outs


N_MAIN_BLK = MAIN_W // IN_TILE
N_K_BLK = DIFF_QK_W // IN_TILE
N_V_BLK = DIFF_V_W // IN_TILE


assert N_K_BLK == 1 and N_V_BLK == 1


def _inproj_body(x_ref, nw_ref, w_ref, wgz_ref, wgk_ref, bgk_ref, pk_in, pv_in, sk_in, sv_in,
                 main_ref, kvb_ref, vt_ref, g_ref, pk_ref, pv_ref, sk_ref, sv_ref,
                 xn_ref, y_ref, sem, *, layer, split, seq_s):
    del pk_in, pv_in, sk_in, sv_in
    i = pl.program_id(0)
    j = pl.program_id(1)
    n_i = pl.num_programs(0)
    tm = x_ref.shape[0]

    @pl.when(j == 0)
    def _():
        x = x_ref[...]
        xn = ((x * _rms_scale(x)) * nw_ref[...]).astype(BF16)
        xn_ref[...] = xn
        gz = _dot_nt(xn, wgz_ref[...])
        z = _dot(gz.astype(BF16), wgk_ref[...]) + bgk_ref[...]
        g_ref[...] = (jnp.minimum(z, 0.0) - jnp.log1p(jnp.exp(-jnp.abs(z)))) * (1.0 / GLA_GATE_NORMALIZER)

    y = _dot_nt(xn_ref[...], w_ref[...])

    @pl.when(j < N_MAIN_BLK)
    def _():
        main_ref[...] = y

    def row_copies(slot, tile, dst_p, dst_s, act):
        heads = [slice(h * DIFF_DV, (h + 1) * DIFF_DV) for h in range(DIFF_HEADS)]

        @pl.when(tile < split)
        def _():
            for h, cols in enumerate(heads):
                act(pltpu.make_async_copy(y_ref.at[slot, :, cols],
                                          dst_p.at[layer, 0, pl.ds(tile * tm, tm), h, :], sem.at[slot]))

        @pl.when(tile >= split)
        def _():
            for b in range(tm // seq_s):
                for h, cols in enumerate(heads):
                    act(pltpu.make_async_copy(y_ref.at[slot, b * seq_s:(b + 1) * seq_s, cols],
                                              dst_s.at[layer, (tile - split) * (tm // seq_s) + b, :, h, :],
                                              sem.at[slot]))

    def kv_step(slot, dst_p, dst_s):
        @pl.when(i > 0)
        def _():
            row_copies(slot, i - 1, dst_p, dst_s, lambda c: c.wait())

        y_ref[slot] = y
        kvb_ref[...] = y.astype(BF16)
        if slot == 1:
            vt_ref[0] = y.T.astype(BF16)
        row_copies(slot, i, dst_p, dst_s, lambda c: c.start())

        @pl.when(i == n_i - 1)
        def _():
            row_copies(slot, i, dst_p, dst_s, lambda c: c.wait())

    @pl.when(j == N_MAIN_BLK)
    def _():
        kv_step(0, pk_ref, sk_ref)

    @pl.when(j == N_MAIN_BLK + 1)
    def _():
        kv_step(1, pv_ref, sv_ref)


def _inproj(x, nw, w_main, w_gz, w_gk, b_gk, kv_out, *, layer, split, seq_s):
    n, d = x.shape
    tm = _row_tile(n)
    assert tm % seq_s == 0 and kv_out[0].shape[1] == 1
    nj = N_MAIN_BLK + 2
    k0 = N_MAIN_BLK
    anyspace = pl.BlockSpec(memory_space=pl.ANY)
    outs = pl.pallas_call(
        functools.partial(_inproj_body, layer=layer, split=split, seq_s=seq_s),
        grid=(n // tm, nj),
        in_specs=[
            pl.BlockSpec((tm, d), lambda i, j: (i, 0)),
            pl.BlockSpec((1, d), lambda i, j: (0, 0)),
            pl.BlockSpec((None, IN_TILE, d), lambda i, j: (layer, j, 0)),
            pl.BlockSpec((None, LANE, d), lambda i, j: (layer, 0, 0)),
            pl.BlockSpec((LANE, GLA_K_W), lambda i, j: (0, 0)),
            pl.BlockSpec((1, GLA_K_W), lambda i, j: (0, 0)),
            anyspace, anyspace, anyspace, anyspace,
        ],
        out_specs=[
            pl.BlockSpec((tm, IN_TILE), lambda i, j: (i, jnp.minimum(j, k0 - 1))),
            pl.BlockSpec((tm, IN_TILE), lambda i, j: (i, jnp.clip(j - k0, 0, 1))),
            pl.BlockSpec((1, IN_TILE, tm), lambda i, j: (i, 0, 0)),
            pl.BlockSpec((tm, GLA_K_W), lambda i, j: (i, 0)),
            anyspace, anyspace, anyspace, anyspace,
        ],
        out_shape=[
            jax.ShapeDtypeStruct((n, MAIN_W), F32),
            jax.ShapeDtypeStruct((n, DIFF_QK_W + DIFF_V_W), BF16),
            jax.ShapeDtypeStruct((n // tm, DIFF_V_W, tm), BF16),
            jax.ShapeDtypeStruct((n, GLA_K_W), F32),
        ] + [jax.ShapeDtypeStruct(a.shape, a.dtype) for a in kv_out],
        scratch_shapes=[pltpu.VMEM((tm, d), BF16), pltpu.VMEM((2, tm, IN_TILE), F32),
                        pltpu.SemaphoreType.DMA((2,))],
        input_output_aliases={6: 4, 7: 5, 8: 6, 9: 7},
        compiler_params=_params("arbitrary", "arbitrary"),
        name="inproj",
    )(x, nw, w_main, w_gz, w_gk, b_gk, *kv_out)
    return outs[:4], outs[4:]


def _split3(a):
    hi = a.astype(BF16)
    r = a - hi.astype(F32)
    mid = r.astype(BF16)
    lo = (r - mid.astype(F32)).astype(BF16)
    return hi, mid, lo


def _gla_body(q_ref, k_ref, v_ref, r_ref, g_ref, s0_ref, nw_ref, mix_in_ref, o_ref, s_ref, st_ref, *, chunk):
    del mix_in_ref
    t = pl.program_id(1)
    rows = q_ref.shape[0]

    @pl.when(t == 0)
    def _():
        for h in range(GLA_HEADS):
            st_ref[h] = s0_ref[0, h].T

    ri = lax.broadcasted_iota(jnp.int32, (chunk, chunk), 0)
    ci = lax.broadcasted_iota(jnp.int32, (chunk, chunk), 1)
    causal = ci <= ri
    tri = causal.astype(BF16)

    for c in range(rows // chunk):
        sl = pl.ds(c * chunk, chunk)
        for h in range(GLA_HEADS):
            kc = slice(h * GLA_DK, (h + 1) * GLA_DK)
            vc = slice(h * GLA_DV, (h + 1) * GLA_DV)
            q = q_ref[sl, kc] * (GLA_DK ** -0.5)
            k = k_ref[sl, kc]
            v = v_ref[sl, vc].astype(BF16)
            g_hi, g_mid, g_lo = _split3(g_ref[sl, kc])
            b = _dot(tri, g_hi) + _dot(tri, g_mid) + _dot(tri, g_lo)
            b_last = b[chunk - 1:chunk, :]
            qe = (q * jnp.exp(b)).astype(BF16)
            ke = (k * jnp.exp(-b)).astype(BF16)
            kd = (k * jnp.exp(b_last - b)).astype(BF16)
            a = jnp.where(causal, _dot_nt(qe, ke), 0.0).astype(BF16)
            st = st_ref[h]
            o = _dot_nt(qe, st.astype(BF16)) + _dot(a, v)
            st_ref[h] = st * jnp.exp(b_last) + _dot_tn(v, kd)
            o = (o * _rms_scale(o)) * nw_ref[...]
            r = r_ref[sl, vc]
            o_ref[sl, vc] = (o * (r * jax.nn.sigmoid(r))).astype(BF16)

    @pl.when(t == pl.num_programs(1) - 1)
    def _():
        for h in range(GLA_HEADS):
            s_ref[0, h] = st_ref[h].T


def _gla(main, g, s0, nw, mix, *, row_off, batch, seq, chunk, rows):
    assert seq % rows == 0 and rows % chunk == 0 and row_off % rows == 0
    nt = seq // rows
    rb0 = row_off // rows

    def rowblk(b, t):
        return rb0 + b * nt + t

    state_spec = pl.BlockSpec((1, GLA_HEADS, GLA_DK, GLA_DV), lambda b, t: (b, 0, 0, 0))
    out, s = pl.pallas_call(
        functools.partial(_gla_body, chunk=chunk),
        grid=(batch, nt),
        in_specs=[
            pl.BlockSpec((rows, GLA_K_W), lambda b, t: (rowblk(b, t), 0)),
            pl.BlockSpec((rows, GLA_K_W), lambda b, t: (rowblk(b, t), 1)),
            pl.BlockSpec((rows, GLA_V_W), lambda b, t: (rowblk(b, t), 2 * GLA_K_W // GLA_V_W)),
            pl.BlockSpec((rows, GLA_V_W), lambda b, t: (rowblk(b, t), 2 * GLA_K_W // GLA_V_W + 1)),
            pl.BlockSpec((rows, GLA_K_W), lambda b, t: (rowblk(b, t), 0)),
            state_spec,
            pl.BlockSpec((1, GLA_DV), lambda b, t: (0, 0)),
            pl.BlockSpec(memory_space=pl.ANY),
        ],
        out_specs=[
            pl.BlockSpec((rows, GLA_V_W), lambda b, t: (rowblk(b, t), 0)),
            state_spec,
        ],
        out_shape=[
            jax.ShapeDtypeStruct(mix.shape, mix.dtype),
            jax.ShapeDtypeStruct((batch, GLA_HEADS, GLA_DK, GLA_DV), F32),
        ],
        scratch_shapes=[pltpu.VMEM((GLA_HEADS, GLA_DV, GLA_DK), F32)],
        input_output_aliases={7: 0},
        compiler_params=_params("parallel", "arbitrary"),
        name="gla",
    )(main, main, main, main, g, s0, nw, mix)
    return out, s


def _bucket_thresholds():
    half = N_BUCKETS // 2
    m = half // 2
    e = half - m
    thr = []
    for kk in range(1, e):
        n = m
        while n ** e * m ** kk < m ** e * MAX_DISTANCE ** kk:
            n += 1
        thr.append(n)
    return tuple(thr)


_BUCKET_THR = _bucket_thresholds()


def _t5_bucket(rel):
    half = N_BUCKETS // 2
    max_exact = half // 2
    n = jnp.abs(rel)
    large = jnp.full(rel.shape, max_exact, jnp.int32)
    for thr in _BUCKET_THR:
        large = large + (n >= thr).astype(jnp.int32)
    return jnp.where(rel > 0, half, 0) + jnp.where(n < max_exact, n, large)


def _bias_body(rb_ref, o_ref, *, q_off, k_off, k_step, key_major, scale):
    h = pl.program_id(0)
    t = pl.program_id(1)
    shape = o_ref.shape[2:]
    qpos = q_off + lax.broadcasted_iota(jnp.int32, shape, 1 if key_major else 0)
    kpos = k_off + t * k_step + lax.broadcasted_iota(jnp.int32, shape, 0 if key_major else 1)
    bucket = _t5_bucket(kpos - qpos)
    acc = jnp.zeros(shape, F32)
    for bkt in range(N_BUCKETS):
        acc = jnp.where(bucket == bkt, rb_ref[bkt, h], acc)
    visible = (kpos >> 6) <= (qpos >> 6)
    o_ref[0, 0] = jnp.where(visible, acc * scale, -jnp.inf)


assert CHUNK == 64


def _bias_table(rel_bias, *, tiles, rows, cols, q_off, k_off, k_step, key_major=False, scale=1.0):
    return pl.pallas_call(
        functools.partial(_bias_body, q_off=q_off, k_off=k_off, k_step=k_step, key_major=key_major,
                          scale=scale),
        grid=(DIFF_HEADS, tiles),
        in_specs=[pl.BlockSpec(memory_space=pltpu.SMEM)],
        out_specs=pl.BlockSpec((1, 1, rows, cols), lambda h, t: (h, t, 0, 0)),
        out_shape=jax.ShapeDtypeStruct((DIFF_HEADS, tiles, rows, cols), F32),
        compiler_params=_params("parallel", "parallel"),
        name="bias_table",
    )(rel_bias)


def _lambda(lam_ref, layer):
    lam = lam_ref[...]
    a = jnp.sum(lam[0:1, :] * lam[1:2, :], axis=-1, keepdims=True)
    b = jnp.sum(lam[2:3, :] * lam[3:4, :], axis=-1, keepdims=True)
    lam_init = 0.8 - 0.6 * math.exp(-0.3 * layer)
    return jnp.exp(a) - jnp.exp(b) + lam_init, lam_init


def _diff_finish(o1, l1, o2, l2, lam, lam_init, nw):
    o = o1 / l1 - lam * (o2 / l2)
    return ((o * _rms_scale(o)) * nw * (1.0 - lam_init)).astype(BF16)


def _attn_prompt_body(q_ref, k_ref, vt_ref, bias_ref, lam_ref, nw_ref, mix_in_ref,
                      o_ref, acc_ref, s_ref, p_ref, *, layer, blk):
    del mix_in_ref
    qi = pl.program_id(1)
    qt = (q_ref[...] * (DIFF_D ** -0.5 * LOG2E)).T.astype(BF16)
    qth = (qt[:DIFF_D], qt[DIFF_D:])
    acc_ref[...] = jnp.zeros_like(acc_ref)
    strips = [slice(c * SUB, (c + 1) * SUB) for c in range(blk // SUB)]

    def fold(a):
        return a.reshape(SUB // 8, 8, blk)

    p_ref[...] = jnp.zeros_like(p_ref)

    def step(kj, stats):
        kb = k_ref[pl.ds(pl.multiple_of(kj * blk, blk), blk), :]
        vt_prev = vt_ref[jnp.maximum(kj - 1, 0)]
        t = jnp.minimum(qi - kj, 2)
        for half in range(2):
            s_ref[half] = _dot(kb[:, half * DIFF_D:(half + 1) * DIFF_D], qth[half]) + bias_ref[0, t]
        for half in range(2):
            acc_ref[half] += _dot(vt_prev, p_ref[half])
        out = []
        for half in range(2):
            m, l = stats[half]
            bm = jnp.max(fold(s_ref[half, strips[0], :]), axis=0)
            for c in strips[1:]:
                bm = jnp.maximum(bm, jnp.max(fold(s_ref[half, c, :]), axis=0))
            m_new = jnp.maximum(m, jnp.max(bm, axis=0, keepdims=True))
            alpha = jnp.exp2(m - m_new)
            acc_ref[half] = alpha * acc_ref[half]
            ls = jnp.zeros((8, blk), F32)
            for c in strips:
                p = jnp.exp2(s_ref[half, c, :] - m_new)
                ls = ls + jnp.sum(fold(p), axis=0)
                p_ref[half, c, :] = p.astype(BF16)
            l = alpha * l + jnp.sum(ls, axis=0, keepdims=True)
            out.append((m_new, l))
        return tuple(out)

    init = tuple((jnp.full((1, blk), -jnp.inf, F32), jnp.zeros((1, blk), F32)) for _ in range(2))
    stats = lax.fori_loop(0, qi + 1, step, init)

    lam, lam_init = _lambda(lam_ref, layer)
    vt_last = vt_ref[qi]
    o1 = acc_ref[0] + _dot(vt_last, p_ref[0])
    o2 = acc_ref[1] + _dot(vt_last, p_ref[1])
    ot = o1 / stats[0][1] - lam * (o2 / stats[1][1])
    scale = lax.rsqrt(jnp.mean(ot * ot, axis=0, keepdims=True) + EPS)
    ot = (ot * scale) * nw_ref[...] * (1.0 - lam_init)
    o_ref[...] = ot.T.astype(BF16)


def _attn_prompt(main, kvb, vt, bias, lam, nw_col, mix, *, layer, seq):
    blk = vt.shape[2]
    assert seq % blk == 0 and blk >= MAX_DISTANCE and blk % CHUNK == 0
    q_col0 = DQ_OFF // (2 * DIFF_D)
    o_col0 = GLA_V_W // DIFF_DV
    return pl.pallas_call(
        functools.partial(_attn_prompt_body, layer=layer, blk=blk),
        grid=(DIFF_HEADS, seq // blk),
        in_specs=[
            pl.BlockSpec((blk, 2 * DIFF_D), lambda h, i: (i, q_col0 + h)),
            pl.BlockSpec((seq, 2 * DIFF_D), lambda h, i: (0, h)),
            pl.BlockSpec((seq // blk, DIFF_DV, blk), lambda h, i: (0, h, 0)),
            pl.BlockSpec((1, 3, blk, blk), lambda h, i: (h, 0, 0, 0)),
            pl.BlockSpec((4, DIFF_D), lambda h, i: (0, 0)),
            pl.BlockSpec((DIFF_DV, 1), lambda h, i: (0, 0)),
            pl.BlockSpec(memory_space=pl.ANY),
        ],
        out_specs=pl.BlockSpec((blk, DIFF_DV), lambda h, i: (i, o_col0 + h)),
        out_shape=jax.ShapeDtypeStruct(mix.shape, mix.dtype),
        scratch_shapes=[pltpu.VMEM((2, DIFF_DV, blk), F32), pltpu.VMEM((2, blk, blk), F32),
                        pltpu.VMEM((2, blk, blk), BF16)],
        input_output_aliases={6: 0},
        compiler_params=_params("parallel", "arbitrary"),
        name="attn_prompt",
    )(main, kvb, vt, bias, lam, nw_col, mix)


def _attn_sample_body(q_ref, kn_ref, vn_ref, kc_hbm, vc_hbm, bias_ref, lam_ref, nw_ref, mix_in_ref,
                      o_ref, kbuf, vbuf, sem, *, layer, past):
    del mix_in_ref
    b = pl.program_id(0)
    h = pl.program_id(1)
    n_h = DIFF_HEADS
    step = b * n_h + h
    n_steps = pl.num_programs(0) * n_h
    slot = step % SAMPLE_SLOTS

    def copies(cstep):
        cb, ch, cslot = cstep // n_h, cstep % n_h, cstep % SAMPLE_SLOTS
        return (pltpu.make_async_copy(kc_hbm.at[layer, cb, :, ch, :], kbuf.at[cslot], sem.at[0, cslot]),
                pltpu.make_async_copy(vc_hbm.at[layer, cb, :, ch, :], vbuf.at[cslot], sem.at[1, cslot]))

    def start(cstep):
        @pl.when(cstep < n_steps)
        def _():
            for c in copies(cstep):
                c.start()

    @pl.when(step == 0)
    def _():
        for ahead in range(SAMPLE_SLOTS - 1):
            start(step + ahead)

    start(step + SAMPLE_SLOTS - 1)

    qs = (q_ref[...] * (DIFF_D ** -0.5)).astype(BF16)
    kn = kn_ref[...]
    bias_c = bias_ref[0, 0, :, :past]
    bias_n = bias_ref[0, 0, :, past:]
    for c in copies(step):
        c.wait()
    ps = []
    for half in range(2):
        cols = slice(half * DIFF_D, (half + 1) * DIFF_D)
        sc = _dot_nt(qs[:, cols], kbuf[slot, :, cols].astype(BF16)) + bias_c
        sn = _dot_nt(qs[:, cols], kn[:, cols]) + bias_n
        m = jnp.maximum(jnp.max(sc, axis=-1, keepdims=True), jnp.max(sn, axis=-1, keepdims=True))
        pc = jnp.exp(sc - m)
        pn = jnp.exp(sn - m)
        inv = 1.0 / (jnp.sum(pc, axis=-1, keepdims=True) + jnp.sum(pn, axis=-1, keepdims=True))
        ps.append((pc * inv, pn * inv))
    lam, lam_init = _lambda(lam_ref, layer)
    wc = (ps[0][0] - lam * ps[1][0]).astype(BF16)
    wn = (ps[0][1] - lam * ps[1][1]).astype(BF16)
    o = _dot(wc, vbuf[slot].astype(BF16)) + _dot(wn, vn_ref[...])
    o_ref[...] = ((o * _rms_scale(o)) * nw_ref[...] * (1.0 - lam_init)).astype(BF16)


def _attn_sample(main, kvb, cache_k, cache_v, bias, lam, nw, mix, *, layer, row_off, batch, seq):
    past = cache_k.shape[2]
    assert row_off % seq == 0
    rb0 = row_off // seq
    q_col0 = DQ_OFF // (2 * DIFF_D)
    v_col0 = DIFF_QK_W // DIFF_DV
    o_col0 = GLA_V_W // DIFF_DV
    return pl.pallas_call(
        functools.partial(_attn_sample_body, layer=layer, past=past),
        grid=(batch, DIFF_HEADS),
        in_specs=[
            pl.BlockSpec((seq, 2 * DIFF_D), lambda b, h: (rb0 + b, q_col0 + h)),
            pl.BlockSpec((seq, 2 * DIFF_D), lambda b, h: (rb0 + b, h)),
            pl.BlockSpec((seq, DIFF_DV), lambda b, h: (rb0 + b, v_col0 + h)),
            pl.BlockSpec(memory_space=pl.ANY),
            pl.BlockSpec(memory_space=pl.ANY),
            pl.BlockSpec((1, 1, seq, past + seq), lambda b, h: (h, 0, 0, 0)),
            pl.BlockSpec((4, DIFF_D), lambda b, h: (0, 0)),
            pl.BlockSpec((1, DIFF_DV), lambda b, h: (0, 0)),
            pl.BlockSpec(memory_space=pl.ANY),
        ],
        out_specs=pl.BlockSpec((seq, DIFF_DV), lambda b, h: (rb0 + b, o_col0 + h)),
        out_shape=jax.ShapeDtypeStruct(mix.shape, mix.dtype),
        scratch_shapes=[
            pltpu.VMEM((SAMPLE_SLOTS, past, 2 * DIFF_D), F32),
            pltpu.VMEM((SAMPLE_SLOTS, past, DIFF_DV), F32),
            pltpu.SemaphoreType.DMA((2, SAMPLE_SLOTS)),
        ],
        input_output_aliases={8: 0},
        compiler_params=_params("arbitrary", "arbitrary"),
        name="attn_sample",
    )(main, kvb, kvb, cache_k, cache_v, bias, lam, nw, mix)


def _outproj_body(o_ref, w_ref, x_ref, y_ref):
    y_ref[...] = x_ref[...] + _dot(o_ref[...], w_ref[...])


def _outproj(mix, w, x, layer):
    n, d = x.shape
    tm = _row_tile(n)
    return pl.pallas_call(
        _outproj_body,
        grid=(n // tm,),
        in_specs=[
            pl.BlockSpec((tm, MIX_WIDTH), lambda i: (i, 0)),
            pl.BlockSpec((None, MIX_WIDTH, d), lambda i: (layer, 0, 0)),
            pl.BlockSpec((tm, d), lambda i: (i, 0)),
        ],
        out_specs=pl.BlockSpec((tm, d), lambda i: (i, 0)),
        out_shape=jax.ShapeDtypeStruct((n, d), F32),
        compiler_params=_params("parallel"),
        name="outproj",
    )(mix, w, x)


def kernel(x_prompt, x_sample, cache_k, cache_v, state_gla, ffn1_norm, ffn1_w_gate, ffn1_w_up, ffn1_w_down,
           mix_norm, w_in, gla_w_gk, gla_b_gk, gla_norm, diff_lambda, diff_norm, w_out,
           ffn2_norm, ffn2_w_gate, ffn2_w_up, ffn2_w_down, rel_bias, final_norm):
    pb, ps, d = x_prompt.shape
    sb, ss, _ = x_sample.shape
    depth = w_in.shape[0]
    past = cache_k.shape[2]
    assert pb == 1
    n_p, n_s = pb * ps, sb * ss
    xs = [x_prompt.reshape(n_p, d), x_sample.reshape(n_s, d)]

    blk = _row_tile(n_p + n_s)
    bias_p = _bias_table(rel_bias, tiles=3, rows=blk, cols=blk, q_off=0, k_off=0, k_step=-blk,
                         key_major=True, scale=LOG2E)
    bias_s = _bias_table(rel_bias, tiles=1, rows=ss, cols=past + ss, q_off=past, k_off=0, k_step=0)
    zero_state = jnp.zeros((pb, GLA_HEADS, GLA_DK, GLA_DV), F32)
    row = lambda a: a.reshape(1, -1)

    f1g, f1u, f1d = _cast_bf16(ffn1_w_gate), _cast_bf16(ffn1_w_up), _cast_bf16(ffn1_w_down)
    f2g, f2u, f2d = _cast_bf16(ffn2_w_gate), _cast_bf16(ffn2_w_up), _cast_bf16(ffn2_w_down)
    w_main, w_gz = _cast_w_in(jnp.swapaxes(w_in, 1, 2))
    w_o = _cast_bf16(w_out)

    kv_out = [jnp.zeros((depth, pb, ps, DIFF_HEADS, DIFF_DV), F32) for _ in range(2)]
    kv_out += [jnp.zeros((depth, sb, ss, DIFF_HEADS, DIFF_DV), F32) for _ in range(2)]
    p_states, s_states = [], []
    for l in range(depth):
        w_gk = jnp.pad(gla_w_gk[l], ((0, LANE - GLA_GK_RANK), (0, 0))).astype(BF16)

        x, = _ffn(xs, row(ffn1_norm[l]), f1g, f1u, f1d, row(final_norm), l, False, [n_p + n_s])
        (main, kvb, vt, g), kv_out = _inproj(x, row(mix_norm[l]), w_main, w_gz, w_gk, row(gla_b_gk[l]), kv_out,
                                             layer=l, split=n_p // blk, seq_s=ss)

        mix = jnp.zeros((n_p + n_s, MIX_WIDTH), BF16)
        mix, sp = _gla(main, g, zero_state, row(gla_norm[l]), mix,
                       row_off=0, batch=pb, seq=ps, chunk=CHUNK, rows=min(GLA_ROWS, ps))
        mix, s_s = _gla(main, g, state_gla[l], row(gla_norm[l]), mix,
                        row_off=n_p, batch=sb, seq=ss, chunk=ss, rows=ss)
        mix = _attn_prompt(main, kvb, vt, bias_p, diff_lambda[l], diff_norm[l].reshape(-1, 1), mix,
                           layer=l, seq=ps)
        mix = _attn_sample(main, kvb, cache_k, cache_v, bias_s, diff_lambda[l], row(diff_norm[l]), mix,
                           layer=l, row_off=n_p, batch=sb, seq=ss)

        x = _outproj(mix, w_o, x, l)
        last = l == depth - 1
        xs = _ffn([x], row(ffn2_norm[l]), f2g, f2u, f2d, row(final_norm), l, last,
                  [n_p, n_s] if last else [n_p + n_s])
        p_states.append(sp)
        s_states.append(s_s)

    prompt_k, prompt_v, sample_k, sample_v = kv_out
    return (xs[0].reshape(pb, ps, d), xs[1].reshape(sb, ss, d), prompt_k, prompt_v, jnp.stack(p_states),
            sample_k, sample_v, jnp.stack(s_states))
```

```python
import functools
import math

import jax
import jax.numpy as jnp
from jax import lax
from jax.experimental import pallas as pl
from jax.experimental.pallas import tpu as pltpu

F32 = jnp.float32
BF16 = jnp.bfloat16

EPS = 1e-6
CHUNK = 64
GLA_HEADS = 4
GLA_DK = 128
GLA_DV = 256
GLA_GK_RANK = 16
GLA_GATE_NORMALIZER = 16.0
DIFF_HEADS = 4
DIFF_D = 128
DIFF_DV = 2 * DIFF_D
N_BUCKETS = 32
MAX_DISTANCE = 128

GLA_K_W = GLA_HEADS * GLA_DK
GLA_V_W = GLA_HEADS * GLA_DV
DIFF_QK_W = DIFF_HEADS * 2 * DIFF_D
DIFF_V_W = DIFF_HEADS * DIFF_DV
MIX_WIDTH = GLA_V_W + DIFF_V_W
GZ_OFF = 2 * GLA_K_W + 2 * GLA_V_W
MAIN_W = GZ_OFF + DIFF_QK_W
DQ_OFF = GZ_OFF

LANE = 128
LOG2E = math.log2(math.e)
SUB = 64
VMEM_LIMIT = 56 * 1024 * 1024

ROW_TILE = 512
FFN_ROWS = 1024
FF_TILE = 512
IN_TILE = 1024
GLA_ROWS = 256
SAMPLE_SLOTS = 3


def _dot(a, b):
    return jnp.dot(a, b, preferred_element_type=F32)


def _dot_nt(a, b):
    return lax.dot_general(a, b, (((1,), (1,)), ((), ())), preferred_element_type=F32)


def _dot_tn(a, b):
    return lax.dot_general(a, b, (((0,), (0,)), ((), ())), preferred_element_type=F32)


def _params(*sem):
    return pltpu.CompilerParams(dimension_semantics=sem, vmem_limit_bytes=VMEM_LIMIT)


def _row_tile(n):
    t = ROW_TILE
    while n % t:
        t //= 2
    return t


def _rms_scale(x):
    return lax.rsqrt(jnp.mean(x * x, axis=-1, keepdims=True) + EPS)


CAST_ROWS = 256


def _cast_body(w_ref, o_ref):
    o_ref[...] = w_ref[...].astype(BF16)


def _cast_bf16(w):
    depth, r, c = w.shape
    spec = pl.BlockSpec((1, CAST_ROWS, c), lambda l, i: (l, i, 0))
    return pl.pallas_call(
        _cast_body,
        grid=(depth, r // CAST_ROWS),
        in_specs=[spec],
        out_specs=spec,
        out_shape=jax.ShapeDtypeStruct(w.shape, BF16),
        compiler_params=_params("parallel", "parallel"),
        name="cast_bf16",
    )(w)


GZ_BLK = GZ_OFF // CAST_ROWS
assert GZ_OFF % CAST_ROWS == 0 and GLA_GK_RANK % 16 == 0 and GLA_GK_RANK < CAST_ROWS


def _cast_w_in_body(a_ref, b_ref, main_ref, gz_ref):
    k = pl.program_id(1)

    @pl.when(k < GZ_BLK)
    def _():
        main_ref[0] = a_ref[0].astype(BF16)

    @pl.when(k >= GZ_BLK)
    def _():
        main_ref[0, :CAST_ROWS - GLA_GK_RANK] = a_ref[0, GLA_GK_RANK:].astype(BF16)
        main_ref[0, CAST_ROWS - GLA_GK_RANK:] = b_ref[0, :GLA_GK_RANK].astype(BF16)

    @pl.when(k == GZ_BLK)
    def _():
        gz_ref[0, :GLA_GK_RANK] = a_ref[0, :GLA_GK_RANK].astype(BF16)
        gz_ref[0, GLA_GK_RANK:] = jnp.zeros((LANE - GLA_GK_RANK, a_ref.shape[2]), BF16)


def _cast_w_in(w_in_t):
    depth, c, r = w_in_t.shape
    wide = c - GLA_GK_RANK
    last = pl.cdiv(c, CAST_ROWS) - 1
    return pl.pallas_call(
        _cast_w_in_body,
        grid=(depth, wide // CAST_ROWS),
        in_specs=[pl.BlockSpec((1, CAST_ROWS, r), lambda l, k: (l, k, 0)),
                  pl.BlockSpec((1, CAST_ROWS, r), lambda l, k: (l, jnp.clip(k + 1, GZ_BLK + 1, last), 0))],
        out_specs=[pl.BlockSpec((1, CAST_ROWS, r), lambda l, k: (l, k, 0)),
                   pl.BlockSpec((1, LANE, r), lambda l, k: (l, 0, 0))],
        out_shape=[jax.ShapeDtypeStruct((depth, wide, r), BF16),
                   jax.ShapeDtypeStruct((depth, LANE, r), BF16)],
        compiler_params=_params("parallel", "arbitrary"),
        name="cast_w_in",
    )(w_in_t, w_in_t)


def _ffn_body(*refs, final, n_in, n_out, split):
    x_refs = refs[:n_in]
    nw_ref, wg_ref, wu_ref, wd_ref, fw_ref = refs[n_in:n_in + 5]
    o_refs = refs[n_in + 5:n_in + 5 + n_out]
    xn_ref = refs[n_in + 5 + n_out]
    acc_ref = o_refs[0] if n_out == 1 else refs[n_in + 6 + n_out]
    i = pl.program_id(0)
    j = pl.program_id(1)

    def load_x():
        if n_in == 1:
            return x_refs[0][...]
        return jnp.where(i < split, x_refs[0][...], x_refs[1][...])

    @pl.when(j == 0)
    def _():
        x = load_x()
        xn_ref[...] = ((x * _rms_scale(x)) * nw_ref[...]).astype(BF16)
        acc_ref[...] = jnp.zeros_like(acc_ref)

    xn = xn_ref[...]
    h = _dot(xn, wg_ref[...])
    u = _dot(xn, wu_ref[...])
    a = (h * jax.nn.sigmoid(h) * u).astype(BF16)
    acc_ref[...] += _dot(a, wd_ref[...])

    @pl.when(j == pl.num_programs(1) - 1)
    def _():
        y = load_x() + 0.5 * acc_ref[...]
        if final:
            y = (y * _rms_scale(y)) * fw_ref[...]
        if n_out == 1:
            o_refs[0][...] = y
        else:
            @pl.when(i < split)
            def _():
                o_refs[0][...] = y

            @pl.when(i >= split)
            def _():
                o_refs[1][...] = y


def _ffn(xs, nw, wg, wu, wd, fw, layer, final, out_rows):
    d = xs[0].shape[1]
    n = sum(x.shape[0] for x in xs)
    assert n == sum(out_rows)
    f = wg.shape[2]
    single = len(xs) == 1 and len(out_rows) == 1
    tm = FFN_ROWS if single and n % FFN_ROWS == 0 else _row_tile(n)
    first = xs[0].shape[0] if len(xs) == 2 else out_rows[0]
    assert first % tm == 0
    split = first // tm
    parts = [lambda i, j: (jnp.minimum(i, split - 1), 0), lambda i, j: (jnp.maximum(i - split, 0), 0)]
    whole = [lambda i, j: (i, 0)]
    outs = pl.pallas_call(
        functools.partial(_ffn_body, final=final, n_in=len(xs), n_out=len(out_rows), split=split),
        grid=(n // tm, f // FF_TILE),
        in_specs=[pl.BlockSpec((tm, d), m) for m in (parts if len(xs) == 2 else whole)] + [
            pl.BlockSpec((1, d), lambda i, j: (0, 0)),
            pl.BlockSpec((None, d, FF_TILE), lambda i, j: (layer, 0, j)),
            pl.BlockSpec((None, d, FF_TILE), lambda i, j: (layer, 0, j)),
            pl.BlockSpec((None, FF_TILE, d), lambda i, j: (layer, j, 0)),
            pl.BlockSpec((1, d), lambda i, j: (0, 0)),
        ],
        out_specs=[pl.BlockSpec((tm, d), m) for m in (parts if len(out_rows) == 2 else whole)],
        out_shape=[jax.ShapeDtypeStruct((r, d), F32) for r in out_rows],
        scratch_shapes=[pltpu.VMEM((tm, d), BF16)] + ([pltpu.VMEM((tm, d), F32)] if len(out_rows) == 2 else []),
        compiler_params=_params("arbitrary" if len(out_rows) == 2 else "parallel", "arbitrary"),
        name="ffn",
    )(*xs, nw, wg, wu, wd, fw)
    return outs


N_MAIN_BLK = MAIN_W // IN_TILE
N_K_BLK = DIFF_QK_W // IN_TILE
N_V_BLK = DIFF_V_W // IN_TILE


assert N_K_BLK == 1 and N_V_BLK == 1


def _inproj_body(x_ref, nw_ref, w_ref, wgz_ref, wgk_ref, bgk_ref, pk_in, pv_in, sk_in, sv_in,
                 main_ref, kvb_ref, vt_ref, g_ref, pk_ref, pv_ref, sk_ref, sv_ref,
                 xn_ref, y_ref, sem, *, layer, split, seq_s):
    del pk_in, pv_in, sk_in, sv_in
    i = pl.program_id(0)
    j = pl.program_id(1)
    n_i = pl.num_programs(0)
    tm = x_ref.shape[0]

    @pl.when(j == 0)
    def _():
        x = x_ref[...]
        xn = ((x * _rms_scale(x)) * nw_ref[...]).astype(BF16)
        xn_ref[...] = xn
        gz = _dot_nt(xn, wgz_ref[...])
        z = _dot(gz.astype(BF16), wgk_ref[...]) + bgk_ref[...]
        g_ref[...] = (jnp.minimum(z, 0.0) - jnp.log1p(jnp.exp(-jnp.abs(z)))) * (1.0 / GLA_GATE_NORMALIZER)

    y = _dot_nt(xn_ref[...], w_ref[...])

    @pl.when(j < N_MAIN_BLK)
    def _():
        main_ref[...] = y

    def row_copies(slot, tile, dst_p, dst_s, act):
        heads = [slice(h * DIFF_DV, (h + 1) * DIFF_DV) for h in range(DIFF_HEADS)]

        @pl.when(tile < split)
        def _():
            for h, cols in enumerate(heads):
                act(pltpu.make_async_copy(y_ref.at[slot, :, cols],
                                          dst_p.at[layer, 0, pl.ds(tile * tm, tm), h, :], sem.at[slot]))

        @pl.when(tile >= split)
        def _():
            for b in range(tm // seq_s):
                for h, cols in enumerate(heads):
                    act(pltpu.make_async_copy(y_ref.at[slot, b * seq_s:(b + 1) * seq_s, cols],
                                              dst_s.at[layer, (tile - split) * (tm // seq_s) + b, :, h, :],
                                              sem.at[slot]))

    def kv_step(slot, dst_p, dst_s):
        @pl.when(i > 0)
        def _():
            row_copies(slot, i - 1, dst_p, dst_s, lambda c: c.wait())

        y_ref[slot] = y
        kvb_ref[...] = y.astype(BF16)
        if slot == 1:
            vt_ref[0] = y.T.astype(BF16)
        row_copies(slot, i, dst_p, dst_s, lambda c: c.start())

        @pl.when(i == n_i - 1)
        def _():
            row_copies(slot, i, dst_p, dst_s, lambda c: c.wait())

    @pl.when(j == N_MAIN_BLK)
    def _():
        kv_step(0, pk_ref, sk_ref)

    @pl.when(j == N_MAIN_BLK + 1)
    def _():
        kv_step(1, pv_ref, sv_ref)


def _inproj(x, nw, w_main, w_gz, w_gk, b_gk, kv_out, *, layer, split, seq_s):
    n, d = x.shape
    tm = _row_tile(n)
    assert tm % seq_s == 0 and kv_out[0].shape[1] == 1
    nj = N_MAIN_BLK + 2
    k0 = N_MAIN_BLK
    anyspace = pl.BlockSpec(memory_space=pl.ANY)
    outs = pl.pallas_call(
        functools.partial(_inproj_body, layer=layer, split=split, seq_s=seq_s),
        grid=(n // tm, nj),
        in_specs=[
            pl.BlockSpec((tm, d), lambda i, j: (i, 0)),
            pl.BlockSpec((1, d), lambda i, j: (0, 0)),
            pl.BlockSpec((None, IN_TILE, d), lambda i, j: (layer, j, 0)),
            pl.BlockSpec((None, LANE, d), lambda i, j: (layer, 0, 0)),
            pl.BlockSpec((LANE, GLA_K_W), lambda i, j: (0, 0)),
            pl.BlockSpec((1, GLA_K_W), lambda i, j: (0, 0)),
            anyspace, anyspace, anyspace, anyspace,
        ],
        out_specs=[
            pl.BlockSpec((tm, IN_TILE), lambda i, j: (i, jnp.minimum(j, k0 - 1))),
            pl.BlockSpec((tm, IN_TILE), lambda i, j: (i, jnp.clip(j - k0, 0, 1))),
            pl.BlockSpec((1, IN_TILE, tm), lambda i, j: (i, 0, 0)),
            pl.BlockSpec((tm, GLA_K_W), lambda i, j: (i, 0)),
            anyspace, anyspace, anyspace, anyspace,
        ],
        out_shape=[
            jax.ShapeDtypeStruct((n, MAIN_W), F32),
            jax.ShapeDtypeStruct((n, DIFF_QK_W + DIFF_V_W), BF16),
            jax.ShapeDtypeStruct((n // tm, DIFF_V_W, tm), BF16),
            jax.ShapeDtypeStruct((n, GLA_K_W), F32),
        ] + [jax.ShapeDtypeStruct(a.shape, a.dtype) for a in kv_out],
        scratch_shapes=[pltpu.VMEM((tm, d), BF16), pltpu.VMEM((2, tm, IN_TILE), F32),
                        pltpu.SemaphoreType.DMA((2,))],
        input_output_aliases={6: 4, 7: 5, 8: 6, 9: 7},
        compiler_params=_params("arbitrary", "arbitrary"),
        name="inproj",
    )(x, nw, w_main, w_gz, w_gk, b_gk, *kv_out)
    return outs[:4], outs[4:]


def _split3(a):
    hi = a.astype(BF16)
    r = a - hi.astype(F32)
    mid = r.astype(BF16)
    lo = (r - mid.astype(F32)).astype(BF16)
    return hi, mid, lo


def _gla_body(q_ref, k_ref, v_ref, r_ref, g_ref, s0_ref, nw_ref, mix_in_ref, o_ref, s_ref, st_ref, *, chunk):
    del mix_in_ref
    t = pl.program_id(1)
    rows = q_ref.shape[0]

    @pl.when(t == 0)
    def _():
        for h in range(GLA_HEADS):
            st_ref[h] = s0_ref[0, h].T

    ri = lax.broadcasted_iota(jnp.int32, (chunk, chunk), 0)
    ci = lax.broadcasted_iota(jnp.int32, (chunk, chunk), 1)
    causal = ci <= ri
    tri = causal.astype(BF16)

    for c in range(rows // chunk):
        sl = pl.ds(c * chunk, chunk)
        for h in range(GLA_HEADS):
            kc = slice(h * GLA_DK, (h + 1) * GLA_DK)
            vc = slice(h * GLA_DV, (h + 1) * GLA_DV)
            q = q_ref[sl, kc] * (GLA_DK ** -0.5)
            k = k_ref[sl, kc]
            v = v_ref[sl, vc].astype(BF16)
            g_hi, g_mid, g_lo = _split3(g_ref[sl, kc])
            b = _dot(tri, g_hi) + _dot(tri, g_mid) + _dot(tri, g_lo)
            b_last = b[chunk - 1:chunk, :]
            qe = (q * jnp.exp(b)).astype(BF16)
            ke = (k * jnp.exp(-b)).astype(BF16)
            kd = (k * jnp.exp(b_last - b)).astype(BF16)
            a = jnp.where(causal, _dot_nt(qe, ke), 0.0).astype(BF16)
            st = st_ref[h]
            o = _dot_nt(qe, st.astype(BF16)) + _dot(a, v)
            st_ref[h] = st * jnp.exp(b_last) + _dot_tn(v, kd)
            o = (o * _rms_scale(o)) * nw_ref[...]
            r = r_ref[sl, vc]
            o_ref[sl, vc] = (o * (r * jax.nn.sigmoid(r))).astype(BF16)

    @pl.when(t == pl.num_programs(1) - 1)
    def _():
        for h in range(GLA_HEADS):
            s_ref[0, h] = st_ref[h].T


def _gla(main, g, s0, nw, mix, *, row_off, batch, seq, chunk, rows):
    assert seq % rows == 0 and rows % chunk == 0 and row_off % rows == 0
    nt = seq // rows
    rb0 = row_off // rows

    def rowblk(b, t):
        return rb0 + b * nt + t

    state_spec = pl.BlockSpec((1, GLA_HEADS, GLA_DK, GLA_DV), lambda b, t: (b, 0, 0, 0))
    out, s = pl.pallas_call(
        functools.partial(_gla_body, chunk=chunk),
        grid=(batch, nt),
        in_specs=[
            pl.BlockSpec((rows, GLA_K_W), lambda b, t: (rowblk(b, t), 0)),
            pl.BlockSpec((rows, GLA_K_W), lambda b, t: (rowblk(b, t), 1)),
            pl.BlockSpec((rows, GLA_V_W), lambda b, t: (rowblk(b, t), 2 * GLA_K_W // GLA_V_W)),
            pl.BlockSpec((rows, GLA_V_W), lambda b, t: (rowblk(b, t), 2 * GLA_K_W // GLA_V_W + 1)),
            pl.BlockSpec((rows, GLA_K_W), lambda b, t: (rowblk(b, t), 0)),
            state_spec,
            pl.BlockSpec((1, GLA_DV), lambda b, t: (0, 0)),
            pl.BlockSpec(memory_space=pl.ANY),
        ],
        out_specs=[
            pl.BlockSpec((rows, GLA_V_W), lambda b, t: (rowblk(b, t), 0)),
            state_spec,
        ],
        out_shape=[
            jax.ShapeDtypeStruct(mix.shape, mix.dtype),
            jax.ShapeDtypeStruct((batch, GLA_HEADS, GLA_DK, GLA_DV), F32),
        ],
        scratch_shapes=[pltpu.VMEM((GLA_HEADS, GLA_DV, GLA_DK), F32)],
        input_output_aliases={7: 0},
        compiler_params=_params("parallel", "arbitrary"),
        name="gla",
    )(main, main, main, main, g, s0, nw, mix)
    return out, s


def _bucket_thresholds():
    half = N_BUCKETS // 2
    m = half // 2
    e = half - m
    thr = []
    for kk in range(1, e):
        n = m
        while n ** e * m ** kk < m ** e * MAX_DISTANCE ** kk:
            n += 1
        thr.append(n)
    return tuple(thr)


_BUCKET_THR = _bucket_thresholds()


def _t5_bucket(rel):
    half = N_BUCKETS // 2
    max_exact = half // 2
    n = jnp.abs(rel)
    large = jnp.full(rel.shape, max_exact, jnp.int32)
    for thr in _BUCKET_THR:
        large = large + (n >= thr).astype(jnp.int32)
    return jnp.where(rel > 0, half, 0) + jnp.where(n < max_exact, n, large)


def _bias_body(rb_ref, o_ref, *, q_off, k_off, k_step, key_major, scale):
    h = pl.program_id(0)
    t = pl.program_id(1)
    shape = o_ref.shape[2:]
    qpos = q_off + lax.broadcasted_iota(jnp.int32, shape, 1 if key_major else 0)
    kpos = k_off + t * k_step + lax.broadcasted_iota(jnp.int32, shape, 0 if key_major else 1)
    bucket = _t5_bucket(kpos - qpos)
    acc = jnp.zeros(shape, F32)
    for bkt in range(N_BUCKETS):
        acc = jnp.where(bucket == bkt, rb_ref[bkt, h], acc)
    visible = (kpos >> 6) <= (qpos >> 6)
    o_ref[0, 0] = jnp.where(visible, acc * scale, -jnp.inf)


assert CHUNK == 64


def _bias_table(rel_bias, *, tiles, rows, cols, q_off, k_off, k_step, key_major=False, scale=1.0):
    return pl.pallas_call(
        functools.partial(_bias_body, q_off=q_off, k_off=k_off, k_step=k_step, key_major=key_major,
                          scale=scale),
        grid=(DIFF_HEADS, tiles),
        in_specs=[pl.BlockSpec(memory_space=pltpu.SMEM)],
        out_specs=pl.BlockSpec((1, 1, rows, cols), lambda h, t: (h, t, 0, 0)),
        out_shape=jax.ShapeDtypeStruct((DIFF_HEADS, tiles, rows, cols), F32),
        compiler_params=_params("parallel", "parallel"),
        name="bias_table",
    )(rel_bias)


def _lambda(lam_ref, layer):
    lam = lam_ref[...]
    a = jnp.sum(lam[0:1, :] * lam[1:2, :], axis=-1, keepdims=True)
    b = jnp.sum(lam[2:3, :] * lam[3:4, :], axis=-1, keepdims=True)
    lam_init = 0.8 - 0.6 * math.exp(-0.3 * layer)
    return jnp.exp(a) - jnp.exp(b) + lam_init, lam_init


def _diff_finish(o1, l1, o2, l2, lam, lam_init, nw):
    o = o1 / l1 - lam * (o2 / l2)
    return ((o * _rms_scale(o)) * nw * (1.0 - lam_init)).astype(BF16)


def _attn_prompt_body(q_ref, k_ref, vt_ref, bias_ref, lam_ref, nw_ref, mix_in_ref,
                      o_ref, acc_ref, s_ref, p_ref, *, layer, blk):
    del mix_in_ref
    qi = pl.program_id(1)
    qt = (q_ref[...] * (DIFF_D ** -0.5 * LOG2E)).T.astype(BF16)
    qth = (qt[:DIFF_D], qt[DIFF_D:])
    acc_ref[...] = jnp.zeros_like(acc_ref)
    strips = [slice(c * SUB, (c + 1) * SUB) for c in range(blk // SUB)]

    def fold(a):
        return a.reshape(SUB // 8, 8, blk)

    p_ref[...] = jnp.zeros_like(p_ref)

    def step(kj, stats):
        kb = k_ref[pl.ds(pl.multiple_of(kj * blk, blk), blk), :]
        vt_prev = vt_ref[jnp.maximum(kj - 1, 0)]
        t = jnp.minimum(qi - kj, 2)
        for half in range(2):
            s_ref[half] = _dot(kb[:, half * DIFF_D:(half + 1) * DIFF_D], qth[half]) + bias_ref[0, t]
        for half in range(2):
            acc_ref[half] += _dot(vt_prev, p_ref[half])
        out = []
        for half in range(2):
            m, l = stats[half]
            bm = jnp.max(fold(s_ref[half, strips[0], :]), axis=0)
            for c in strips[1:]:
                bm = jnp.maximum(bm, jnp.max(fold(s_ref[half, c, :]), axis=0))
            m_new = jnp.maximum(m, jnp.max(bm, axis=0, keepdims=True))
            alpha = jnp.exp2(m - m_new)
            acc_ref[half] = alpha * acc_ref[half]
            ls = jnp.zeros((8, blk), F32)
            for c in strips:
                p = jnp.exp2(s_ref[half, c, :] - m_new)
                ls = ls + jnp.sum(fold(p), axis=0)
                p_ref[half, c, :] = p.astype(BF16)
            l = alpha * l + jnp.sum(ls, axis=0, keepdims=True)
            out.append((m_new, l))
        return tuple(out)

    init = tuple((jnp.full((1, blk), -jnp.inf, F32), jnp.zeros((1, blk), F32)) for _ in range(2))
    stats = lax.fori_loop(0, qi + 1, step, init)

    lam, lam_init = _lambda(lam_ref, layer)
    vt_last = vt_ref[qi]
    o1 = acc_ref[0] + _dot(vt_last, p_ref[0])
    o2 = acc_ref[1] + _dot(vt_last, p_ref[1])
    ot = o1 / stats[0][1] - lam * (o2 / stats[1][1])
    scale = lax.rsqrt(jnp.mean(ot * ot, axis=0, keepdims=True) + EPS)
    ot = (ot * scale) * nw_ref[...] * (1.0 - lam_init)
    o_ref[...] = ot.T.astype(BF16)


def _attn_prompt(main, kvb, vt, bias, lam, nw_col, mix, *, layer, seq):
    blk = vt.shape[2]
    assert seq % blk == 0 and blk >= MAX_DISTANCE and blk % CHUNK == 0
    q_col0 = DQ_OFF // (2 * DIFF_D)
    o_col0 = GLA_V_W // DIFF_DV
    return pl.pallas_call(
        functools.partial(_attn_prompt_body, layer=layer, blk=blk),
        grid=(DIFF_HEADS, seq // blk),
        in_specs=[
            pl.BlockSpec((blk, 2 * DIFF_D), lambda h, i: (i, q_col0 + h)),
            pl.BlockSpec((seq, 2 * DIFF_D), lambda h, i: (0, h)),
            pl.BlockSpec((seq // blk, DIFF_DV, blk), lambda h, i: (0, h, 0)),
            pl.BlockSpec((1, 3, blk, blk), lambda h, i: (h, 0, 0, 0)),
            pl.BlockSpec((4, DIFF_D), lambda h, i: (0, 0)),
            pl.BlockSpec((DIFF_DV, 1), lambda h, i: (0, 0)),
            pl.BlockSpec(memory_space=pl.ANY),
        ],
        out_specs=pl.BlockSpec((blk, DIFF_DV), lambda h, i: (i, o_col0 + h)),
        out_shape=jax.ShapeDtypeStruct(mix.shape, mix.dtype),
        scratch_shapes=[pltpu.VMEM((2, DIFF_DV, blk), F32), pltpu.VMEM((2, blk, blk), F32),
                        pltpu.VMEM((2, blk, blk), BF16)],
        input_output_aliases={6: 0},
        compiler_params=_params("parallel", "arbitrary"),
        name="attn_prompt",
    )(main, kvb, vt, bias, lam, nw_col, mix)


def _attn_sample_body(q_ref, kn_ref, vn_ref, kc_hbm, vc_hbm, bias_ref, lam_ref, nw_ref, mix_in_ref,
                      o_ref, kbuf, vbuf, sem, *, layer, past):
    del mix_in_ref
    b = pl.program_id(0)
    h = pl.program_id(1)
    n_h = DIFF_HEADS
    step = b * n_h + h
    n_steps = pl.num_programs(0) * n_h
    slot = step % SAMPLE_SLOTS

    def copies(cstep):
        cb, ch, cslot = cstep // n_h, cstep % n_h, cstep % SAMPLE_SLOTS
        return (pltpu.make_async_copy(kc_hbm.at[layer, cb, :, ch, :], kbuf.at[cslot], sem.at[0, cslot]),
                pltpu.make_async_copy(vc_hbm.at[layer, cb, :, ch, :], vbuf.at[cslot], sem.at[1, cslot]))

    def start(cstep):
        @pl.when(cstep < n_steps)
        def _():
            for c in copies(cstep):
                c.start()

    @pl.when(step == 0)
    def _():
        for ahead in range(SAMPLE_SLOTS - 1):
            start(step + ahead)

    start(step + SAMPLE_SLOTS - 1)

    qs = (q_ref[...] * (DIFF_D ** -0.5)).astype(BF16)
    kn = kn_ref[...]
    bias_c = bias_ref[0, 0, :, :past]
    bias_n = bias_ref[0, 0, :, past:]
    for c in copies(step):
        c.wait()
    ps = []
    for half in range(2):
        cols = slice(half * DIFF_D, (half + 1) * DIFF_D)
        sc = _dot_nt(qs[:, cols], kbuf[slot, :, cols].astype(BF16)) + bias_c
        sn = _dot_nt(qs[:, cols], kn[:, cols]) + bias_n
        m = jnp.maximum(jnp.max(sc, axis=-1, keepdims=True), jnp.max(sn, axis=-1, keepdims=True))
        pc = jnp.exp(sc - m)
        pn = jnp.exp(sn - m)
        inv = 1.0 / (jnp.sum(pc, axis=-1, keepdims=True) + jnp.sum(pn, axis=-1, keepdims=True))
        ps.append((pc * inv, pn * inv))
    lam, lam_init = _lambda(lam_ref, layer)
    wc = (ps[0][0] - lam * ps[1][0]).astype(BF16)
    wn = (ps[0][1] - lam * ps[1][1]).astype(BF16)
    o = _dot(wc, vbuf[slot].astype(BF16)) + _dot(wn, vn_ref[...])
    o_ref[...] = ((o * _rms_scale(o)) * nw_ref[...] * (1.0 - lam_init)).astype(BF16)


def _attn_sample(main, kvb, cache_k, cache_v, bias, lam, nw, mix, *, layer, row_off, batch, seq):
    past = cache_k.shape[2]
    assert row_off % seq == 0
    rb0 = row_off // seq
    q_col0 = DQ_OFF // (2 * DIFF_D)
    v_col0 = DIFF_QK_W // DIFF_DV
    o_col0 = GLA_V_W // DIFF_DV
    return pl.pallas_call(
        functools.partial(_attn_sample_body, layer=layer, past=past),
        grid=(batch, DIFF_HEADS),
        in_specs=[
            pl.BlockSpec((seq, 2 * DIFF_D), lambda b, h: (rb0 + b, q_col0 + h)),
            pl.BlockSpec((seq, 2 * DIFF_D), lambda b, h: (rb0 + b, h)),
            pl.BlockSpec((seq, DIFF_DV), lambda b, h: (rb0 + b, v_col0 + h)),
            pl.BlockSpec(memory_space=pl.ANY),
            pl.BlockSpec(memory_space=pl.ANY),
            pl.BlockSpec((1, 1, seq, past + seq), lambda b, h: (h, 0, 0, 0)),
            pl.BlockSpec((4, DIFF_D), lambda b, h: (0, 0)),
            pl.BlockSpec((1, DIFF_DV), lambda b, h: (0, 0)),
            pl.BlockSpec(memory_space=pl.ANY),
        ],
        out_specs=pl.BlockSpec((seq, DIFF_DV), lambda b, h: (rb0 + b, o_col0 + h)),
        out_shape=jax.ShapeDtypeStruct(mix.shape, mix.dtype),
        scratch_shapes=[
            pltpu.VMEM((SAMPLE_SLOTS, past, 2 * DIFF_D), F32),
            pltpu.VMEM((SAMPLE_SLOTS, past, DIFF_DV), F32),
            pltpu.SemaphoreType.DMA((2, SAMPLE_SLOTS)),
        ],
        input_output_aliases={8: 0},
        compiler_params=_params("arbitrary", "arbitrary"),
        name="attn_sample",
    )(main, kvb, kvb, cache_k, cache_v, bias, lam, nw, mix)


def _outproj_body(o_ref, w_ref, x_ref, y_ref):
    y_ref[...] = x_ref[...] + _dot(o_ref[...], w_ref[...])


def _outproj(mix, w, x, layer):
    n, d = x.shape
    tm = _row_tile(n)
    return pl.pallas_call(
        _outproj_body,
        grid=(n // tm,),
        in_specs=[
            pl.BlockSpec((tm, MIX_WIDTH), lambda i: (i, 0)),
            pl.BlockSpec((None, MIX_WIDTH, d), lambda i: (layer, 0, 0)),
            pl.BlockSpec((tm, d), lambda i: (i, 0)),
        ],
        out_specs=pl.BlockSpec((tm, d), lambda i: (i, 0)),
        out_shape=jax.ShapeDtypeStruct((n, d), F32),
        compiler_params=_params("parallel"),
        name="outproj",
    )(mix, w, x)


def kernel(x_prompt, x_sample, cache_k, cache_v, state_gla, ffn1_norm, ffn1_w_gate, ffn1_w_up, ffn1_w_down,
           mix_norm, w_in, gla_w_gk, gla_b_gk, gla_norm, diff_lambda, diff_norm, w_out,
           ffn2_norm, ffn2_w_gate, ffn2_w_up, ffn2_w_down, rel_bias, final_norm):
    pb, ps, d = x_prompt.shape
    sb, ss, _ = x_sample.shape
    depth = w_in.shape[0]
    past = cache_k.shape[2]
    assert pb == 1
    n_p, n_s = pb * ps, sb * ss
    xs = [x_prompt.reshape(n_p, d), x_sample.reshape(n_s, d)]

    blk = _row_tile(n_p + n_s)
    bias_p = _bias_table(rel_bias, tiles=3, rows=blk, cols=blk, q_off=0, k_off=0, k_step=-blk,
                         key_major=True, scale=LOG2E)
    bias_s = _bias_table(rel_bias, tiles=1, rows=ss, cols=past + ss, q_off=past, k_off=0, k_step=0)
    zero_state = jnp.zeros((pb, GLA_HEADS, GLA_DK, GLA_DV), F32)
    row = lambda a: a.reshape(1, -1)

    f1g, f1u, f1d = _cast_bf16(ffn1_w_gate), _cast_bf16(ffn1_w_up), _cast_bf16(ffn1_w_down)
    f2g, f2u, f2d = _cast_bf16(ffn2_w_gate), _cast_bf16(ffn2_w_up), _cast_bf16(ffn2_w_down)
    w_main, w_gz = _cast_w_in(jnp.swapaxes(w_in, 1, 2))
    w_o = _cast_bf16(w_out)

    kv_out = [jnp.zeros((depth, pb, ps, DIFF_HEADS, DIFF_DV), F32) for _ in range(2)]
    kv_out += [jnp.zeros((depth, sb, ss, DIFF_HEADS, DIFF_DV), F32) for _ in range(2)]
    p_states, s_states = [], []
    for l in range(depth):
        w_gk = jnp.pad(gla_w_gk[l], ((0, LANE - GLA_GK_RANK), (0, 0))).astype(BF16)

        x, = _ffn(xs, row(ffn1_norm[l]), f1g, f1u, f1d, row(final_norm), l, False, [n_p + n_s])
        (main, kvb, vt, g), kv_out = _inproj(x, row(mix_norm[l]), w_main, w_gz, w_gk, row(gla_b_gk[l]), kv_out,
                                             layer=l, split=n_p // blk, seq_s=ss)

        mix = jnp.zeros((n_p + n_s, MIX_WIDTH), BF16)
        mix, sp = _gla(main, g, zero_state, row(gla_norm[l]), mix,
                       row_off=0, batch=pb, seq=ps, chunk=CHUNK, rows=min(GLA_ROWS, ps))
        mix, s_s = _gla(main, g, state_gla[l], row(gla_norm[l]), mix,
                        row_off=n_p, batch=sb, seq=ss, chunk=ss, rows=ss)
        mix = _attn_prompt(main, kvb, vt, bias_p, diff_lambda[l], diff_norm[l].reshape(-1, 1), mix,
                           layer=l, seq=ps)
        mix = _attn_sample(main, kvb, cache_k, cache_v, bias_s, diff_lambda[l], row(diff_norm[l]), mix,
                           layer=l, row_off=n_p, batch=sb, seq=ss)

        x = _outproj(mix, w_o, x, l)
        last = l == depth - 1
        xs = _ffn([x], row(ffn2_norm[l]), f2g, f2u, f2d, row(final_norm), l, last,
                  [n_p, n_s] if last else [n_p + n_s])
        p_states.append(sp)
        s_states.append(s_s)

    prompt_k, prompt_v, sample_k, sample_v = kv_out
    return (xs[0].reshape(pb, ps, d), xs[1].reshape(sb, ss, d), prompt_k, prompt_v, jnp.stack(p_states),
            sample_k, sample_v, jnp.stack(s_states))
```

```python
import functools
import math

import jax
import jax.numpy as jnp
from jax import lax
from jax.experimental import pallas as pl
from jax.experimental.pallas import tpu as pltpu

F32 = jnp.float32
BF16 = jnp.bfloat16

EPS = 1e-6
CHUNK = 64
GLA_HEADS = 4
GLA_DK = 128
GLA_DV = 256
GLA_GK_RANK = 16
GLA_GATE_NORMALIZER = 16.0
DIFF_HEADS = 4
DIFF_D = 128
DIFF_DV = 2 * DIFF_D
N_BUCKETS = 32
MAX_DISTANCE = 128

GLA_K_W = GLA_HEADS * GLA_DK
GLA_V_W = GLA_HEADS * GLA_DV
DIFF_QK_W = DIFF_HEADS * 2 * DIFF_D
DIFF_V_W = DIFF_HEADS * DIFF_DV
MIX_WIDTH = GLA_V_W + DIFF_V_W
GZ_OFF = 2 * GLA_K_W + 2 * GLA_V_W
MAIN_W = GZ_OFF + DIFF_QK_W
DQ_OFF = GZ_OFF

LANE = 128
LOG2E = math.log2(math.e)
SUB = 64
VMEM_LIMIT = 56 * 1024 * 1024

ROW_TILE = 512
FFN_ROWS = 1024
FF_TILE = 512
IN_TILE = 1024
GLA_ROWS = 256
SAMPLE_SLOTS = 3


def _dot(a, b):
    return jnp.dot(a, b, preferred_element_type=F32)


def _dot_nt(a, b):
    return lax.dot_general(a, b, (((1,), (1,)), ((), ())), preferred_element_type=F32)


def _dot_tn(a, b):
    return lax.dot_general(a, b, (((0,), (0,)), ((), ())), preferred_element_type=F32)


def _params(*sem):
    return pltpu.CompilerParams(dimension_semantics=sem, vmem_limit_bytes=VMEM_LIMIT)


def _row_tile(n):
    t = ROW_TILE
    while n % t:
        t //= 2
    return t


def _rms_scale(x):
    return lax.rsqrt(jnp.mean(x * x, axis=-1, keepdims=True) + EPS)


CAST_ROWS = 256


def _cast_body(w_ref, o_ref):
    o_ref[...] = w_ref[...].astype(BF16)


def _cast_bf16(w, layer):
    _, r, c = w.shape
    return pl.pallas_call(
        _cast_body,
        grid=(r // CAST_ROWS,),
        in_specs=[pl.BlockSpec((None, CAST_ROWS, c), lambda i: (layer, i, 0))],
        out_specs=pl.BlockSpec((CAST_ROWS, c), lambda i: (i, 0)),
        out_shape=jax.ShapeDtypeStruct((r, c), BF16),
        compiler_params=_params("parallel"),
        name="cast_bf16",
    )(w)


def _cast_rows(rows, steps):
    return -(-rows // (16 * steps)) * 16


class _SideCasts:
    def __init__(self, weights, layers, steps):
        self.layers = layers
        self.steps = steps
        self.shapes = [w.shape[1:] for w in weights]
        self.rows = [_cast_rows(s[0], steps) for s in self.shapes]
        assert all(s[0] >= r and s[0] % 16 == 0 for s, r in zip(self.shapes, self.rows))

    def __len__(self):
        return len(self.shapes)

    def out_shapes(self):
        return [jax.ShapeDtypeStruct(s, BF16) for s in self.shapes]

    def scratch_shapes(self):
        n = len(self)
        if n == 0:
            return []
        return ([pltpu.VMEM((2, r, s[1]), F32) for s, r in zip(self.shapes, self.rows)]
                + [pltpu.VMEM((r, s[1]), BF16) for s, r in zip(self.shapes, self.rows)]
                + [pltpu.SemaphoreType.DMA((n, 2)), pltpu.SemaphoreType.DMA((n,))])

    def run(self, step, srcs, dsts, scratch):
        n = len(self)
        if n == 0:
            return
        in_bufs, out_bufs, in_sem, out_sem = scratch[:n], scratch[n:2 * n], scratch[2 * n], scratch[2 * n + 1]

        def start_row(k, t):
            r, total = self.rows[k], self.shapes[k][0]
            return pl.multiple_of(jnp.minimum(t * r, total - r), 16)

        def fetch(k, t):
            return pltpu.make_async_copy(srcs[k].at[self.layers[k], pl.ds(start_row(k, t), self.rows[k]), :],
                                         in_bufs[k].at[t % 2], in_sem.at[k, t % 2])

        def store(k, t):
            return pltpu.make_async_copy(out_bufs[k], dsts[k].at[pl.ds(start_row(k, t), self.rows[k]), :],
                                         out_sem.at[k])

        @pl.when(step == 0)
        def _():
            for k in range(n):
                fetch(k, step).start()

        @pl.when(step + 1 < self.steps)
        def _():
            for k in range(n):
                fetch(k, step + 1).start()

        for k in range(n):
            fetch(k, step).wait()

            @pl.when(step > 0)
            def _():
                store(k, step - 1).wait()

            out_bufs[k][...] = in_bufs[k][step % 2].astype(BF16)
            store(k, step).start()

        @pl.when(step == self.steps - 1)
        def _():
            for k in range(n):
                store(k, step).wait()


GZ_BLK = GZ_OFF // CAST_ROWS
assert GZ_OFF % CAST_ROWS == 0 and GLA_GK_RANK % 16 == 0 and GLA_GK_RANK < CAST_ROWS


def _cast_w_in_body(a_ref, b_ref, main_ref, gz_ref):
    k = pl.program_id(1)

    @pl.when(k < GZ_BLK)
    def _():
        main_ref[0] = a_ref[0].astype(BF16)

    @pl.when(k >= GZ_BLK)
    def _():
        main_ref[0, :CAST_ROWS - GLA_GK_RANK] = a_ref[0, GLA_GK_RANK:].astype(BF16)
        main_ref[0, CAST_ROWS - GLA_GK_RANK:] = b_ref[0, :GLA_GK_RANK].astype(BF16)

    @pl.when(k == GZ_BLK)
    def _():
        gz_ref[0, :GLA_GK_RANK] = a_ref[0, :GLA_GK_RANK].astype(BF16)
        gz_ref[0, GLA_GK_RANK:] = jnp.zeros((LANE - GLA_GK_RANK, a_ref.shape[2]), BF16)


def _cast_w_in(w_in_t):
    depth, c, r = w_in_t.shape
    wide = c - GLA_GK_RANK
    last = pl.cdiv(c, CAST_ROWS) - 1
    return pl.pallas_call(
        _cast_w_in_body,
        grid=(depth, wide // CAST_ROWS),
        in_specs=[pl.BlockSpec((1, CAST_ROWS, r), lambda l, k: (l, k, 0)),
                  pl.BlockSpec((1, CAST_ROWS, r), lambda l, k: (l, jnp.clip(k + 1, GZ_BLK + 1, last), 0))],
        out_specs=[pl.BlockSpec((1, CAST_ROWS, r), lambda l, k: (l, k, 0)),
                   pl.BlockSpec((1, LANE, r), lambda l, k: (l, 0, 0))],
        out_shape=[jax.ShapeDtypeStruct((depth, wide, r), BF16),
                   jax.ShapeDtypeStruct((depth, LANE, r), BF16)],
        compiler_params=_params("parallel", "arbitrary"),
        name="cast_w_in",
    )(w_in_t, w_in_t)


def _ffn_body(*refs, final, n_in, n_out, split):
    x_refs = refs[:n_in]
    nw_ref, wg_ref, wu_ref, wd_ref, fw_ref = refs[n_in:n_in + 5]
    o_refs = refs[n_in + 5:n_in + 5 + n_out]
    xn_ref = refs[n_in + 5 + n_out]
    acc_ref = o_refs[0] if n_out == 1 else refs[n_in + 6 + n_out]
    i = pl.program_id(0)
    j = pl.program_id(1)

    def load_x():
        if n_in == 1:
            return x_refs[0][...]
        return jnp.where(i < split, x_refs[0][...], x_refs[1][...])

    @pl.when(j == 0)
    def _():
        x = load_x()
        xn_ref[...] = ((x * _rms_scale(x)) * nw_ref[...]).astype(BF16)
        acc_ref[...] = jnp.zeros_like(acc_ref)

    xn = xn_ref[...]
    h = _dot(xn, wg_ref[...])
    u = _dot(xn, wu_ref[...])
    a = (h * jax.nn.sigmoid(h) * u).astype(BF16)
    acc_ref[...] += _dot(a, wd_ref[...])

    @pl.when(j == pl.num_programs(1) - 1)
    def _():
        y = load_x() + 0.5 * acc_ref[...]
        if final:
            y = (y * _rms_scale(y)) * fw_ref[...]
        if n_out == 1:
            o_refs[0][...] = y
        else:
            @pl.when(i < split)
            def _():
                o_refs[0][...] = y

            @pl.when(i >= split)
            def _():
                o_refs[1][...] = y


def _ffn(xs, nw, wg, wu, wd, fw, final, out_rows):
    d = xs[0].shape[1]
    n = sum(x.shape[0] for x in xs)
    assert n == sum(out_rows)
    f = wg.shape[1]
    single = len(xs) == 1 and len(out_rows) == 1
    tm = FFN_ROWS if single and n % FFN_ROWS == 0 else _row_tile(n)
    first = xs[0].shape[0] if len(xs) == 2 else out_rows[0]
    assert first % tm == 0
    split = first // tm
    parts = [lambda i, j: (jnp.minimum(i, split - 1), 0), lambda i, j: (jnp.maximum(i - split, 0), 0)]
    whole = [lambda i, j: (i, 0)]
    outs = pl.pallas_call(
        functools.partial(_ffn_body, final=final, n_in=len(xs), n_out=len(out_rows), split=split),
        grid=(n // tm, f // FF_TILE),
        in_specs=[pl.BlockSpec((tm, d), m) for m in (parts if len(xs) == 2 else whole)] + [
            pl.BlockSpec((1, d), lambda i, j: (0, 0)),
            pl.BlockSpec((d, FF_TILE), lambda i, j: (0, j)),
            pl.BlockSpec((d, FF_TILE), lambda i, j: (0, j)),
            pl.BlockSpec((FF_TILE, d), lambda i, j: (j, 0)),
            pl.BlockSpec((1, d), lambda i, j: (0, 0)),
        ],
        out_specs=[pl.BlockSpec((tm, d), m) for m in (parts if len(out_rows) == 2 else whole)],
        out_shape=[jax.ShapeDtypeStruct((r, d), F32) for r in out_rows],
        scratch_shapes=[pltpu.VMEM((tm, d), BF16)] + ([pltpu.VMEM((tm, d), F32)] if len(out_rows) == 2 else []),
        compiler_params=_params("arbitrary" if len(out_rows) == 2 else "parallel", "arbitrary"),
        name="ffn",
    )(*xs, nw, wg, wu, wd, fw)
    return outs


N_MAIN_BLK = MAIN_W // IN_TILE
N_K_BLK = DIFF_QK_W // IN_TILE
N_V_BLK = DIFF_V_W // IN_TILE


assert N_K_BLK == 1 and N_V_BLK == 1


def _inproj_body(x_ref, nw_ref, w_ref, wgz_ref, wgk_ref, bgk_ref, pk_in, pv_in, sk_in, sv_in,
                 main_ref, kvb_ref, vt_ref, g_ref, pk_ref, pv_ref, sk_ref, sv_ref,
                 xn_ref, y_ref, sem, *, layer, split, seq_s):
    del pk_in, pv_in, sk_in, sv_in
    i = pl.program_id(0)
    j = pl.program_id(1)
    n_i = pl.num_programs(0)
    tm = x_ref.shape[0]

    @pl.when(j == 0)
    def _():
        x = x_ref[...]
        xn = ((x * _rms_scale(x)) * nw_ref[...]).astype(BF16)
        xn_ref[...] = xn
        gz = _dot_nt(xn, wgz_ref[...])
        z = _dot(gz.astype(BF16), wgk_ref[...]) + bgk_ref[...]
        g_ref[...] = (jnp.minimum(z, 0.0) - jnp.log1p(jnp.exp(-jnp.abs(z)))) * (1.0 / GLA_GATE_NORMALIZER)

    y = _dot_nt(xn_ref[...], w_ref[...])

    @pl.when(j < N_MAIN_BLK)
    def _():
        main_ref[...] = y

    def row_copies(slot, tile, dst_p, dst_s, act):
        heads = [slice(h * DIFF_DV, (h + 1) * DIFF_DV) for h in range(DIFF_HEADS)]

        @pl.when(tile < split)
        def _():
            for h, cols in enumerate(heads):
                act(pltpu.make_async_copy(y_ref.at[slot, :, cols],
                                          dst_p.at[layer, 0, pl.ds(tile * tm, tm), h, :], sem.at[slot]))

        @pl.when(tile >= split)
        def _():
            for b in range(tm // seq_s):
                for h, cols in enumerate(heads):
                    act(pltpu.make_async_copy(y_ref.at[slot, b * seq_s:(b + 1) * seq_s, cols],
                                              dst_s.at[layer, (tile - split) * (tm // seq_s) + b, :, h, :],
                                              sem.at[slot]))

    def kv_step(slot, dst_p, dst_s):
        @pl.when(i > 0)
        def _():
            row_copies(slot, i - 1, dst_p, dst_s, lambda c: c.wait())

        y_ref[slot] = y
        kvb_ref[...] = y.astype(BF16)
        if slot == 1:
            vt_ref[0] = y.T.astype(BF16)
        row_copies(slot, i, dst_p, dst_s, lambda c: c.start())

        @pl.when(i == n_i - 1)
        def _():
            row_copies(slot, i, dst_p, dst_s, lambda c: c.wait())

    @pl.when(j == N_MAIN_BLK)
    def _():
        kv_step(0, pk_ref, sk_ref)

    @pl.when(j == N_MAIN_BLK + 1)
    def _():
        kv_step(1, pv_ref, sv_ref)


def _inproj(x, nw, w_main, w_gz, w_gk, b_gk, kv_out, *, layer, split, seq_s):
    n, d = x.shape
    tm = _row_tile(n)
    assert tm % seq_s == 0 and kv_out[0].shape[1] == 1
    nj = N_MAIN_BLK + 2
    k0 = N_MAIN_BLK
    anyspace = pl.BlockSpec(memory_space=pl.ANY)
    outs = pl.pallas_call(
        functools.partial(_inproj_body, layer=layer, split=split, seq_s=seq_s),
        grid=(n // tm, nj),
        in_specs=[
            pl.BlockSpec((tm, d), lambda i, j: (i, 0)),
            pl.BlockSpec((1, d), lambda i, j: (0, 0)),
            pl.BlockSpec((None, IN_TILE, d), lambda i, j: (layer, j, 0)),
            pl.BlockSpec((None, LANE, d), lambda i, j: (layer, 0, 0)),
            pl.BlockSpec((LANE, GLA_K_W), lambda i, j: (0, 0)),
            pl.BlockSpec((1, GLA_K_W), lambda i, j: (0, 0)),
            anyspace, anyspace, anyspace, anyspace,
        ],
        out_specs=[
            pl.BlockSpec((tm, IN_TILE), lambda i, j: (i, jnp.minimum(j, k0 - 1))),
            pl.BlockSpec((tm, IN_TILE), lambda i, j: (i, jnp.clip(j - k0, 0, 1))),
            pl.BlockSpec((1, IN_TILE, tm), lambda i, j: (i, 0, 0)),
            pl.BlockSpec((tm, GLA_K_W), lambda i, j: (i, 0)),
            anyspace, anyspace, anyspace, anyspace,
        ],
        out_shape=[
            jax.ShapeDtypeStruct((n, MAIN_W), F32),
            jax.ShapeDtypeStruct((n, DIFF_QK_W + DIFF_V_W), BF16),
            jax.ShapeDtypeStruct((n // tm, DIFF_V_W, tm), BF16),
            jax.ShapeDtypeStruct((n, GLA_K_W), F32),
        ] + [jax.ShapeDtypeStruct(a.shape, a.dtype) for a in kv_out],
        scratch_shapes=[pltpu.VMEM((tm, d), BF16), pltpu.VMEM((2, tm, IN_TILE), F32),
                        pltpu.SemaphoreType.DMA((2,))],
        input_output_aliases={6: 4, 7: 5, 8: 6, 9: 7},
        compiler_params=_params("arbitrary", "arbitrary"),
        name="inproj",
    )(x, nw, w_main, w_gz, w_gk, b_gk, *kv_out)
    return outs[:4], outs[4:]


def _split3(a):
    hi = a.astype(BF16)
    r = a - hi.astype(F32)
    mid = r.astype(BF16)
    lo = (r - mid.astype(F32)).astype(BF16)
    return hi, mid, lo


def _gla_body(q_ref, k_ref, v_ref, r_ref, g_ref, s0_ref, nw_ref, mix_in_ref, o_ref, s_ref, st_ref, *, chunk):
    del mix_in_ref
    t = pl.program_id(1)
    rows = q_ref.shape[0]

    @pl.when(t == 0)
    def _():
        for h in range(GLA_HEADS):
            st_ref[h] = s0_ref[0, h].T

    ri = lax.broadcasted_iota(jnp.int32, (chunk, chunk), 0)
    ci = lax.broadcasted_iota(jnp.int32, (chunk, chunk), 1)
    causal = ci <= ri
    tri = causal.astype(BF16)

    for c in range(rows // chunk):
        sl = pl.ds(c * chunk, chunk)
        for h in range(GLA_HEADS):
            kc = slice(h * GLA_DK, (h + 1) * GLA_DK)
            vc = slice(h * GLA_DV, (h + 1) * GLA_DV)
            q = q_ref[sl, kc] * (GLA_DK ** -0.5)
            k = k_ref[sl, kc]
            v = v_ref[sl, vc].astype(BF16)
            g_hi, g_mid, g_lo = _split3(g_ref[sl, kc])
            b = _dot(tri, g_hi) + _dot(tri, g_mid) + _dot(tri, g_lo)
            b_last = b[chunk - 1:chunk, :]
            qe = (q * jnp.exp(b)).astype(BF16)
            ke = (k * jnp.exp(-b)).astype(BF16)
            kd = (k * jnp.exp(b_last - b)).astype(BF16)
            a = jnp.where(causal, _dot_nt(qe, ke), 0.0).astype(BF16)
            st = st_ref[h]
            o = _dot_nt(qe, st.astype(BF16)) + _dot(a, v)
            st_ref[h] = st * jnp.exp(b_last) + _dot_tn(v, kd)
            o = (o * _rms_scale(o)) * nw_ref[...]
            r = r_ref[sl, vc]
            o_ref[sl, vc] = (o * (r * jax.nn.sigmoid(r))).astype(BF16)

    @pl.when(t == pl.num_programs(1) - 1)
    def _():
        for h in range(GLA_HEADS):
            s_ref[0, h] = st_ref[h].T


def _gla(main, g, s0, nw, mix, *, row_off, batch, seq, chunk, rows):
    assert seq % rows == 0 and rows % chunk == 0 and row_off % rows == 0
    nt = seq // rows
    rb0 = row_off // rows

    def rowblk(b, t):
        return rb0 + b * nt + t

    state_spec = pl.BlockSpec((1, GLA_HEADS, GLA_DK, GLA_DV), lambda b, t: (b, 0, 0, 0))
    out, s = pl.pallas_call(
        functools.partial(_gla_body, chunk=chunk),
        grid=(batch, nt),
        in_specs=[
            pl.BlockSpec((rows, GLA_K_W), lambda b, t: (rowblk(b, t), 0)),
            pl.BlockSpec((rows, GLA_K_W), lambda b, t: (rowblk(b, t), 1)),
            pl.BlockSpec((rows, GLA_V_W), lambda b, t: (rowblk(b, t), 2 * GLA_K_W // GLA_V_W)),
            pl.BlockSpec((rows, GLA_V_W), lambda b, t: (rowblk(b, t), 2 * GLA_K_W // GLA_V_W + 1)),
            pl.BlockSpec((rows, GLA_K_W), lambda b, t: (rowblk(b, t), 0)),
            state_spec,
            pl.BlockSpec((1, GLA_DV), lambda b, t: (0, 0)),
            pl.BlockSpec(memory_space=pl.ANY),
        ],
        out_specs=[
            pl.BlockSpec((rows, GLA_V_W), lambda b, t: (rowblk(b, t), 0)),
            state_spec,
        ],
        out_shape=[
            jax.ShapeDtypeStruct(mix.shape, mix.dtype),
            jax.ShapeDtypeStruct((batch, GLA_HEADS, GLA_DK, GLA_DV), F32),
        ],
        scratch_shapes=[pltpu.VMEM((GLA_HEADS, GLA_DV, GLA_DK), F32)],
        input_output_aliases={7: 0},
        compiler_params=_params("parallel", "arbitrary"),
        name="gla",
    )(main, main, main, main, g, s0, nw, mix)
    return out, s


def _bucket_thresholds():
    half = N_BUCKETS // 2
    m = half // 2
    e = half - m
    thr = []
    for kk in range(1, e):
        n = m
        while n ** e * m ** kk < m ** e * MAX_DISTANCE ** kk:
            n += 1
        thr.append(n)
    return tuple(thr)


_BUCKET_THR = _bucket_thresholds()


def _t5_bucket(rel):
    half = N_BUCKETS // 2
    max_exact = half // 2
    n = jnp.abs(rel)
    large = jnp.full(rel.shape, max_exact, jnp.int32)
    for thr in _BUCKET_THR:
        large = large + (n >= thr).astype(jnp.int32)
    return jnp.where(rel > 0, half, 0) + jnp.where(n < max_exact, n, large)


def _bias_body(rb_ref, o_ref, *, q_off, k_off, k_step, key_major, scale):
    h = pl.program_id(0)
    t = pl.program_id(1)
    shape = o_ref.shape[2:]
    qpos = q_off + lax.broadcasted_iota(jnp.int32, shape, 1 if key_major else 0)
    kpos = k_off + t * k_step + lax.broadcasted_iota(jnp.int32, shape, 0 if key_major else 1)
    bucket = _t5_bucket(kpos - qpos)
    acc = jnp.zeros(shape, F32)
    for bkt in range(N_BUCKETS):
        acc = jnp.where(bucket == bkt, rb_ref[bkt, h], acc)
    visible = (kpos >> 6) <= (qpos >> 6)
    o_ref[0, 0] = jnp.where(visible, acc * scale, -jnp.inf)


assert CHUNK == 64


def _bias_table(rel_bias, *, tiles, rows, cols, q_off, k_off, k_step, key_major=False, scale=1.0):
    return pl.pallas_call(
        functools.partial(_bias_body, q_off=q_off, k_off=k_off, k_step=k_step, key_major=key_major,
                          scale=scale),
        grid=(DIFF_HEADS, tiles),
        in_specs=[pl.BlockSpec(memory_space=pltpu.SMEM)],
        out_specs=pl.BlockSpec((1, 1, rows, cols), lambda h, t: (h, t, 0, 0)),
        out_shape=jax.ShapeDtypeStruct((DIFF_HEADS, tiles, rows, cols), F32),
        compiler_params=_params("parallel", "parallel"),
        name="bias_table",
    )(rel_bias)


def _lambda(lam_ref, layer):
    lam = lam_ref[...]
    a = jnp.sum(lam[0:1, :] * lam[1:2, :], axis=-1, keepdims=True)
    b = jnp.sum(lam[2:3, :] * lam[3:4, :], axis=-1, keepdims=True)
    lam_init = 0.8 - 0.6 * math.exp(-0.3 * layer)
    return jnp.exp(a) - jnp.exp(b) + lam_init, lam_init


def _diff_finish(o1, l1, o2, l2, lam, lam_init, nw):
    o = o1 / l1 - lam * (o2 / l2)
    return ((o * _rms_scale(o)) * nw * (1.0 - lam_init)).astype(BF16)


def _attn_prompt_body(q_ref, k_ref, vt_ref, bias_ref, lam_ref, nw_ref, mix_in_ref, *rest, layer, blk, casts):
    del mix_in_ref
    n = len(casts)
    cast_srcs, o_ref, cast_dsts = rest[:n], rest[n], rest[n + 1:2 * n + 1]
    acc_ref, s_ref, p_ref = rest[2 * n + 1:2 * n + 4]
    casts.run(pl.program_id(0) * pl.num_programs(1) + pl.program_id(1), cast_srcs, cast_dsts, rest[2 * n + 4:])
    qi = pl.program_id(1)
    qt = (q_ref[...] * (DIFF_D ** -0.5 * LOG2E)).T.astype(BF16)
    qth = (qt[:DIFF_D], qt[DIFF_D:])
    acc_ref[...] = jnp.zeros_like(acc_ref)
    strips = [slice(c * SUB, (c + 1) * SUB) for c in range(blk // SUB)]

    def fold(a):
        return a.reshape(SUB // 8, 8, blk)

    p_ref[...] = jnp.zeros_like(p_ref)

    def step(kj, stats):
        kb = k_ref[pl.ds(pl.multiple_of(kj * blk, blk), blk), :]
        vt_prev = vt_ref[jnp.maximum(kj - 1, 0)]
        t = jnp.minimum(qi - kj, 2)
        for half in range(2):
            s_ref[half] = _dot(kb[:, half * DIFF_D:(half + 1) * DIFF_D], qth[half]) + bias_ref[0, t]
        for half in range(2):
            acc_ref[half] += _dot(vt_prev, p_ref[half])
        out = []
        for half in range(2):
            m, l = stats[half]
            bm = jnp.max(fold(s_ref[half, strips[0], :]), axis=0)
            for c in strips[1:]:
                bm = jnp.maximum(bm, jnp.max(fold(s_ref[half, c, :]), axis=0))
            m_new = jnp.maximum(m, jnp.max(bm, axis=0, keepdims=True))
            alpha = jnp.exp2(m - m_new)
            acc_ref[half] = alpha * acc_ref[half]
            ls = jnp.zeros((8, blk), F32)
            for c in strips:
                p = jnp.exp2(s_ref[half, c, :] - m_new)
                ls = ls + jnp.sum(fold(p), axis=0)
                p_ref[half, c, :] = p.astype(BF16)
            l = alpha * l + jnp.sum(ls, axis=0, keepdims=True)
            out.append((m_new, l))
        return tuple(out)

    init = tuple((jnp.full((1, blk), -jnp.inf, F32), jnp.zeros((1, blk), F32)) for _ in range(2))
    stats = lax.fori_loop(0, qi + 1, step, init)

    lam, lam_init = _lambda(lam_ref, layer)
    vt_last = vt_ref[qi]
    o1 = acc_ref[0] + _dot(vt_last, p_ref[0])
    o2 = acc_ref[1] + _dot(vt_last, p_ref[1])
    ot = o1 / stats[0][1] - lam * (o2 / stats[1][1])
    scale = lax.rsqrt(jnp.mean(ot * ot, axis=0, keepdims=True) + EPS)
    ot = (ot * scale) * nw_ref[...] * (1.0 - lam_init)
    o_ref[...] = ot.T.astype(BF16)


def _attn_prompt(main, kvb, vt, bias, lam, nw_col, mix, cast_weights=(), cast_layers=(), *, layer, seq):
    blk = vt.shape[2]
    assert seq % blk == 0 and blk >= MAX_DISTANCE and blk % CHUNK == 0
    q_col0 = DQ_OFF // (2 * DIFF_D)
    o_col0 = GLA_V_W // DIFF_DV
    casts = _SideCasts(cast_weights, cast_layers, DIFF_HEADS * (seq // blk))
    anyspace = pl.BlockSpec(memory_space=pl.ANY)
    outs = pl.pallas_call(
        functools.partial(_attn_prompt_body, layer=layer, blk=blk, casts=casts),
        grid=(DIFF_HEADS, seq // blk),
        in_specs=[
            pl.BlockSpec((blk, 2 * DIFF_D), lambda h, i: (i, q_col0 + h)),
            pl.BlockSpec((seq, 2 * DIFF_D), lambda h, i: (0, h)),
            pl.BlockSpec((seq // blk, DIFF_DV, blk), lambda h, i: (0, h, 0)),
            pl.BlockSpec((1, 3, blk, blk), lambda h, i: (h, 0, 0, 0)),
            pl.BlockSpec((4, DIFF_D), lambda h, i: (0, 0)),
            pl.BlockSpec((DIFF_DV, 1), lambda h, i: (0, 0)),
            anyspace,
        ] + [anyspace] * len(casts),
        out_specs=[pl.BlockSpec((blk, DIFF_DV), lambda h, i: (i, o_col0 + h))] + [anyspace] * len(casts),
        out_shape=[jax.ShapeDtypeStruct(mix.shape, mix.dtype)] + casts.out_shapes(),
        scratch_shapes=[pltpu.VMEM((2, DIFF_DV, blk), F32), pltpu.VMEM((2, blk, blk), F32),
                        pltpu.VMEM((2, blk, blk), BF16)] + casts.scratch_shapes(),
        input_output_aliases={6: 0},
        compiler_params=_params("arbitrary", "arbitrary"),
        name="attn_prompt",
    )(main, kvb, vt, bias, lam, nw_col, mix, *cast_weights)
    return outs[0], outs[1:]


def _attn_sample_body(q_ref, kn_ref, vn_ref, kc_hbm, vc_hbm, bias_ref, lam_ref, nw_ref, mix_in_ref,
                      o_ref, kbuf, vbuf, sem, *, layer, past):
    del mix_in_ref
    b = pl.program_id(0)
    h = pl.program_id(1)
    n_h = DIFF_HEADS
    step = b * n_h + h
    n_steps = pl.num_programs(0) * n_h
    slot = step % SAMPLE_SLOTS

    def copies(cstep):
        cb, ch, cslot = cstep // n_h, cstep % n_h, cstep % SAMPLE_SLOTS
        return (pltpu.make_async_copy(kc_hbm.at[layer, cb, :, ch, :], kbuf.at[cslot], sem.at[0, cslot]),
                pltpu.make_async_copy(vc_hbm.at[layer, cb, :, ch, :], vbuf.at[cslot], sem.at[1, cslot]))

    def start(cstep):
        @pl.when(cstep < n_steps)
        def _():
            for c in copies(cstep):
                c.start()

    @pl.when(step == 0)
    def _():
        for ahead in range(SAMPLE_SLOTS - 1):
            start(step + ahead)

    start(step + SAMPLE_SLOTS - 1)

    qs = (q_ref[...] * (DIFF_D ** -0.5)).astype(BF16)
    kn = kn_ref[...]
    bias_c = bias_ref[0, 0, :, :past]
    bias_n = bias_ref[0, 0, :, past:]
    for c in copies(step):
        c.wait()
    ps = []
    for half in range(2):
        cols = slice(half * DIFF_D, (half + 1) * DIFF_D)
        sc = _dot_nt(qs[:, cols], kbuf[slot, :, cols].astype(BF16)) + bias_c
        sn = _dot_nt(qs[:, cols], kn[:, cols]) + bias_n
        m = jnp.maximum(jnp.max(sc, axis=-1, keepdims=True), jnp.max(sn, axis=-1, keepdims=True))
        pc = jnp.exp(sc - m)
        pn = jnp.exp(sn - m)
        inv = 1.0 / (jnp.sum(pc, axis=-1, keepdims=True) + jnp.sum(pn, axis=-1, keepdims=True))
        ps.append((pc * inv, pn * inv))
    lam, lam_init = _lambda(lam_ref, layer)
    wc = (ps[0][0] - lam * ps[1][0]).astype(BF16)
    wn = (ps[0][1] - lam * ps[1][1]).astype(BF16)
    o = _dot(wc, vbuf[slot].astype(BF16)) + _dot(wn, vn_ref[...])
    o_ref[...] = ((o * _rms_scale(o)) * nw_ref[...] * (1.0 - lam_init)).astype(BF16)


def _attn_sample(main, kvb, cache_k, cache_v, bias, lam, nw, mix, *, layer, row_off, batch, seq):
    past = cache_k.shape[2]
    assert row_off % seq == 0
    rb0 = row_off // seq
    q_col0 = DQ_OFF // (2 * DIFF_D)
    v_col0 = DIFF_QK_W // DIFF_DV
    o_col0 = GLA_V_W // DIFF_DV
    return pl.pallas_call(
        functools.partial(_attn_sample_body, layer=layer, past=past),
        grid=(batch, DIFF_HEADS),
        in_specs=[
            pl.BlockSpec((seq, 2 * DIFF_D), lambda b, h: (rb0 + b, q_col0 + h)),
            pl.BlockSpec((seq, 2 * DIFF_D), lambda b, h: (rb0 + b, h)),
            pl.BlockSpec((seq, DIFF_DV), lambda b, h: (rb0 + b, v_col0 + h)),
            pl.BlockSpec(memory_space=pl.ANY),
            pl.BlockSpec(memory_space=pl.ANY),
            pl.BlockSpec((1, 1, seq, past + seq), lambda b, h: (h, 0, 0, 0)),
            pl.BlockSpec((4, DIFF_D), lambda b, h: (0, 0)),
            pl.BlockSpec((1, DIFF_DV), lambda b, h: (0, 0)),
            pl.BlockSpec(memory_space=pl.ANY),
        ],
        out_specs=pl.BlockSpec((seq, DIFF_DV), lambda b, h: (rb0 + b, o_col0 + h)),
        out_shape=jax.ShapeDtypeStruct(mix.shape, mix.dtype),
        scratch_shapes=[
            pltpu.VMEM((SAMPLE_SLOTS, past, 2 * DIFF_D), F32),
            pltpu.VMEM((SAMPLE_SLOTS, past, DIFF_DV), F32),
            pltpu.SemaphoreType.DMA((2, SAMPLE_SLOTS)),
        ],
        input_output_aliases={8: 0},
        compiler_params=_params("arbitrary", "arbitrary"),
        name="attn_sample",
    )(main, kvb, kvb, cache_k, cache_v, bias, lam, nw, mix)


def _outproj_body(o_ref, w_ref, x_ref, y_ref):
    y_ref[...] = x_ref[...] + _dot(o_ref[...], w_ref[...])


def _outproj(mix, w, x):
    n, d = x.shape
    tm = _row_tile(n)
    return pl.pallas_call(
        _outproj_body,
        grid=(n // tm,),
        in_specs=[
            pl.BlockSpec((tm, MIX_WIDTH), lambda i: (i, 0)),
            pl.BlockSpec((MIX_WIDTH, d), lambda i: (0, 0)),
            pl.BlockSpec((tm, d), lambda i: (i, 0)),
        ],
        out_specs=pl.BlockSpec((tm, d), lambda i: (i, 0)),
        out_shape=jax.ShapeDtypeStruct((n, d), F32),
        compiler_params=_params("parallel"),
        name="outproj",
    )(mix, w, x)


def kernel(x_prompt, x_sample, cache_k, cache_v, state_gla, ffn1_norm, ffn1_w_gate, ffn1_w_up, ffn1_w_down,
           mix_norm, w_in, gla_w_gk, gla_b_gk, gla_norm, diff_lambda, diff_norm, w_out,
           ffn2_norm, ffn2_w_gate, ffn2_w_up, ffn2_w_down, rel_bias, final_norm):
    pb, ps, d = x_prompt.shape
    sb, ss, _ = x_sample.shape
    depth = w_in.shape[0]
    past = cache_k.shape[2]
    assert pb == 1
    n_p, n_s = pb * ps, sb * ss
    xs = [x_prompt.reshape(n_p, d), x_sample.reshape(n_s, d)]

    blk = _row_tile(n_p + n_s)
    bias_p = _bias_table(rel_bias, tiles=3, rows=blk, cols=blk, q_off=0, k_off=0, k_step=-blk,
                         key_major=True, scale=LOG2E)
    bias_s = _bias_table(rel_bias, tiles=1, rows=ss, cols=past + ss, q_off=past, k_off=0, k_step=0)
    zero_state = jnp.zeros((pb, GLA_HEADS, GLA_DK, GLA_DV), F32)
    row = lambda a: a.reshape(1, -1)

    ffn1_w = (ffn1_w_gate, ffn1_w_up, ffn1_w_down)
    ffn2_w = (ffn2_w_gate, ffn2_w_up, ffn2_w_down)
    w_main, w_gz = _cast_w_in(jnp.swapaxes(w_in, 1, 2))
    f1 = {0: [_cast_bf16(w, 0) for w in ffn1_w]}
    later = [(w, l) for l in range(depth) for w in ((ffn1_w if l else ()) + ffn2_w + (w_out,))]
    f2, w_o = {}, {}

    kv_out = [jnp.zeros((depth, pb, ps, DIFF_HEADS, DIFF_DV), F32) for _ in range(2)]
    kv_out += [jnp.zeros((depth, sb, ss, DIFF_HEADS, DIFF_DV), F32) for _ in range(2)]
    p_states, s_states = [], []
    for l in range(depth):
        w_gk = jnp.pad(gla_w_gk[l], ((0, LANE - GLA_GK_RANK), (0, 0))).astype(BF16)

        x, = _ffn(xs, row(ffn1_norm[l]), *f1[l], row(final_norm), False, [n_p + n_s])
        (main, kvb, vt, g), kv_out = _inproj(x, row(mix_norm[l]), w_main, w_gz, w_gk, row(gla_b_gk[l]), kv_out,
                                             layer=l, split=n_p // blk, seq_s=ss)

        mix = jnp.zeros((n_p + n_s, MIX_WIDTH), BF16)
        mix, sp = _gla(main, g, zero_state, row(gla_norm[l]), mix,
                       row_off=0, batch=pb, seq=ps, chunk=CHUNK, rows=min(GLA_ROWS, ps))
        mix, s_s = _gla(main, g, state_gla[l], row(gla_norm[l]), mix,
                        row_off=n_p, batch=sb, seq=ss, chunk=ss, rows=ss)
        mix, cast = _attn_prompt(main, kvb, vt, bias_p, diff_lambda[l], diff_norm[l].reshape(-1, 1), mix,
                                 [w for w, _ in later] if l == 0 else [], [k for _, k in later] if l == 0 else [],
                                 layer=l, seq=ps)
        cast = list(cast)
        for k in range(depth if l == 0 else 0):
            if k:
                f1[k] = [cast.pop(0) for _ in ffn1_w]
            f2[k] = [cast.pop(0) for _ in ffn2_w]
            w_o[k] = cast.pop(0)
        mix = _attn_sample(main, kvb, cache_k, cache_v, bias_s, diff_lambda[l], row(diff_norm[l]), mix,
                           layer=l, row_off=n_p, batch=sb, seq=ss)

        x = _outproj(mix, w_o[l], x)
        last = l == depth - 1
        xs = _ffn([x], row(ffn2_norm[l]), *f2[l], row(final_norm), last, [n_p, n_s] if last else [n_p + n_s])
        p_states.append(sp)
        s_states.append(s_s)

    prompt_k, prompt_v, sample_k, sample_v = kv_out
    return (xs[0].reshape(pb, ps, d), xs[1].reshape(sb, ss, d), prompt_k, prompt_v, jnp.stack(p_states),
            sample_k, sample_v, jnp.stack(s_states))
```

```python
import functools
import math

import jax
import jax.numpy as jnp
from jax import lax
from jax.experimental import pallas as pl
from jax.experimental.pallas import tpu as pltpu

F32 = jnp.float32
BF16 = jnp.bfloat16

EPS = 1e-6
CHUNK = 64
GLA_HEADS = 4
GLA_DK = 128
GLA_DV = 256
GLA_GK_RANK = 16
GLA_GATE_NORMALIZER = 16.0
DIFF_HEADS = 4
DIFF_D = 128
DIFF_DV = 2 * DIFF_D
N_BUCKETS = 32
MAX_DISTANCE = 128

GLA_K_W = GLA_HEADS * GLA_DK
GLA_V_W = GLA_HEADS * GLA_DV
DIFF_QK_W = DIFF_HEADS * 2 * DIFF_D
DIFF_V_W = DIFF_HEADS * DIFF_DV
MIX_WIDTH = GLA_V_W + DIFF_V_W
GZ_OFF = 2 * GLA_K_W + 2 * GLA_V_W
MAIN_W = GZ_OFF + DIFF_QK_W
DQ_OFF = GZ_OFF

LANE = 128
LOG2E = math.log2(math.e)
SUB = 64
VMEM_LIMIT = 56 * 1024 * 1024

ROW_TILE = 512
FFN_ROWS = 1024
FF_TILE = 512
IN_TILE = 1024
GLA_ROWS = 256
SAMPLE_SLOTS = 3


def _dot(a, b):
    return jnp.dot(a, b, preferred_element_type=F32)


def _dot_nt(a, b):
    return lax.dot_general(a, b, (((1,), (1,)), ((), ())), preferred_element_type=F32)


def _dot_tn(a, b):
    return lax.dot_general(a, b, (((0,), (0,)), ((), ())), preferred_element_type=F32)


def _params(*sem):
    return pltpu.CompilerParams(dimension_semantics=sem, vmem_limit_bytes=VMEM_LIMIT)


def _row_tile(n):
    t = ROW_TILE
    while n % t:
        t //= 2
    return t


def _rms_scale(x):
    return lax.rsqrt(jnp.mean(x * x, axis=-1, keepdims=True) + EPS)


CAST_ROWS = 256


def _cast_body(w_ref, o_ref):
    o_ref[...] = w_ref[...].astype(BF16)


def _cast_bf16(w, layer):
    _, r, c = w.shape
    return pl.pallas_call(
        _cast_body,
        grid=(r // CAST_ROWS,),
        in_specs=[pl.BlockSpec((None, CAST_ROWS, c), lambda i: (layer, i, 0))],
        out_specs=pl.BlockSpec((CAST_ROWS, c), lambda i: (i, 0)),
        out_shape=jax.ShapeDtypeStruct((r, c), BF16),
        compiler_params=_params("parallel"),
        name="cast_bf16",
    )(w)


def _cast_rows(rows, steps):
    return -(-rows // (16 * steps)) * 16


class _SideCasts:
    def __init__(self, weights, layers, steps):
        self.layers = layers
        self.steps = steps
        self.shapes = [w.shape[1:] for w in weights]
        self.rows = [_cast_rows(s[0], steps) for s in self.shapes]
        assert all(s[0] >= r and s[0] % 16 == 0 for s, r in zip(self.shapes, self.rows))

    def __len__(self):
        return len(self.shapes)

    def out_shapes(self):
        return [jax.ShapeDtypeStruct(s, BF16) for s in self.shapes]

    def scratch_shapes(self):
        n = len(self)
        if n == 0:
            return []
        return ([pltpu.VMEM((2, r, s[1]), F32) for s, r in zip(self.shapes, self.rows)]
                + [pltpu.VMEM((r, s[1]), BF16) for s, r in zip(self.shapes, self.rows)]
                + [pltpu.SemaphoreType.DMA((n, 2)), pltpu.SemaphoreType.DMA((n,))])

    def run(self, step, srcs, dsts, scratch):
        n = len(self)
        if n == 0:
            return
        in_bufs, out_bufs, in_sem, out_sem = scratch[:n], scratch[n:2 * n], scratch[2 * n], scratch[2 * n + 1]

        def start_row(k, t):
            r, total = self.rows[k], self.shapes[k][0]
            return pl.multiple_of(jnp.minimum(t * r, total - r), 16)

        def fetch(k, t):
            return pltpu.make_async_copy(srcs[k].at[self.layers[k], pl.ds(start_row(k, t), self.rows[k]), :],
                                         in_bufs[k].at[t % 2], in_sem.at[k, t % 2])

        def store(k, t):
            return pltpu.make_async_copy(out_bufs[k], dsts[k].at[pl.ds(start_row(k, t), self.rows[k]), :],
                                         out_sem.at[k])

        @pl.when(step == 0)
        def _():
            for k in range(n):
                fetch(k, step).start()

        @pl.when(step + 1 < self.steps)
        def _():
            for k in range(n):
                fetch(k, step + 1).start()

        for k in range(n):
            fetch(k, step).wait()

            @pl.when(step > 0)
            def _():
                store(k, step - 1).wait()

            out_bufs[k][...] = in_bufs[k][step % 2].astype(BF16)
            store(k, step).start()

        @pl.when(step == self.steps - 1)
        def _():
            for k in range(n):
                store(k, step).wait()


GZ_BLK = GZ_OFF // CAST_ROWS
assert GZ_OFF % CAST_ROWS == 0 and GLA_GK_RANK % 16 == 0 and GLA_GK_RANK < CAST_ROWS


def _cast_w_in_body(a_ref, b_ref, main_ref, gz_ref):
    k = pl.program_id(1)

    @pl.when(k < GZ_BLK)
    def _():
        main_ref[0] = a_ref[0].astype(BF16)

    @pl.when(k >= GZ_BLK)
    def _():
        main_ref[0, :CAST_ROWS - GLA_GK_RANK] = a_ref[0, GLA_GK_RANK:].astype(BF16)
        main_ref[0, CAST_ROWS - GLA_GK_RANK:] = b_ref[0, :GLA_GK_RANK].astype(BF16)

    @pl.when(k == GZ_BLK)
    def _():
        gz_ref[0, :GLA_GK_RANK] = a_ref[0, :GLA_GK_RANK].astype(BF16)
        gz_ref[0, GLA_GK_RANK:] = jnp.zeros((LANE - GLA_GK_RANK, a_ref.shape[2]), BF16)


def _cast_w_in(w_in_t):
    depth, c, r = w_in_t.shape
    wide = c - GLA_GK_RANK
    last = pl.cdiv(c, CAST_ROWS) - 1
    return pl.pallas_call(
        _cast_w_in_body,
        grid=(depth, wide // CAST_ROWS),
        in_specs=[pl.BlockSpec((1, CAST_ROWS, r), lambda l, k: (l, k, 0)),
                  pl.BlockSpec((1, CAST_ROWS, r), lambda l, k: (l, jnp.clip(k + 1, GZ_BLK + 1, last), 0))],
        out_specs=[pl.BlockSpec((1, CAST_ROWS, r), lambda l, k: (l, k, 0)),
                   pl.BlockSpec((1, LANE, r), lambda l, k: (l, 0, 0))],
        out_shape=[jax.ShapeDtypeStruct((depth, wide, r), BF16),
                   jax.ShapeDtypeStruct((depth, LANE, r), BF16)],
        compiler_params=_params("parallel", "arbitrary"),
        name="cast_w_in",
    )(w_in_t, w_in_t)


def _ffn_body(*refs, final, n_in, n_out, split):
    x_refs = refs[:n_in]
    nw_ref, wg_ref, wu_ref, wd_ref, fw_ref = refs[n_in:n_in + 5]
    o_refs = refs[n_in + 5:n_in + 5 + n_out]
    xn_ref = refs[n_in + 5 + n_out]
    acc_ref = o_refs[0] if n_out == 1 else refs[n_in + 6 + n_out]
    i = pl.program_id(0)
    j = pl.program_id(1)

    def load_x():
        if n_in == 1:
            return x_refs[0][...]
        return jnp.where(i < split, x_refs[0][...], x_refs[1][...])

    @pl.when(j == 0)
    def _():
        x = load_x()
        xn_ref[...] = ((x * _rms_scale(x)) * nw_ref[...]).astype(BF16)
        acc_ref[...] = jnp.zeros_like(acc_ref)

    xn = xn_ref[...]
    h = _dot(xn, wg_ref[...])
    u = _dot(xn, wu_ref[...])
    a = (h * jax.nn.sigmoid(h) * u).astype(BF16)
    acc_ref[...] += _dot(a, wd_ref[...])

    @pl.when(j == pl.num_programs(1) - 1)
    def _():
        y = load_x() + 0.5 * acc_ref[...]
        if final:
            y = (y * _rms_scale(y)) * fw_ref[...]
        if n_out == 1:
            o_refs[0][...] = y
        else:
            @pl.when(i < split)
            def _():
                o_refs[0][...] = y

            @pl.when(i >= split)
            def _():
                o_refs[1][...] = y


def _ffn(xs, nw, wg, wu, wd, fw, final, out_rows):
    d = xs[0].shape[1]
    n = sum(x.shape[0] for x in xs)
    assert n == sum(out_rows)
    f = wg.shape[1]
    single = len(xs) == 1 and len(out_rows) == 1
    tm = FFN_ROWS if single and n % FFN_ROWS == 0 else _row_tile(n)
    first = xs[0].shape[0] if len(xs) == 2 else out_rows[0]
    assert first % tm == 0
    split = first // tm
    parts = [lambda i, j: (jnp.minimum(i, split - 1), 0), lambda i, j: (jnp.maximum(i - split, 0), 0)]
    whole = [lambda i, j: (i, 0)]
    outs = pl.pallas_call(
        functools.partial(_ffn_body, final=final, n_in=len(xs), n_out=len(out_rows), split=split),
        grid=(n // tm, f // FF_TILE),
        in_specs=[pl.BlockSpec((tm, d), m) for m in (parts if len(xs) == 2 else whole)] + [
            pl.BlockSpec((1, d), lambda i, j: (0, 0)),
            pl.BlockSpec((d, FF_TILE), lambda i, j: (0, j)),
            pl.BlockSpec((d, FF_TILE), lambda i, j: (0, j)),
            pl.BlockSpec((FF_TILE, d), lambda i, j: (j, 0)),
            pl.BlockSpec((1, d), lambda i, j: (0, 0)),
        ],
        out_specs=[pl.BlockSpec((tm, d), m) for m in (parts if len(out_rows) == 2 else whole)],
        out_shape=[jax.ShapeDtypeStruct((r, d), F32) for r in out_rows],
        scratch_shapes=[pltpu.VMEM((tm, d), BF16)] + ([pltpu.VMEM((tm, d), F32)] if len(out_rows) == 2 else []),
        compiler_params=_params("arbitrary" if len(out_rows) == 2 else "parallel", "arbitrary"),
        name="ffn",
    )(*xs, nw, wg, wu, wd, fw)
    return outs


N_MAIN_BLK = MAIN_W // IN_TILE
N_K_BLK = DIFF_QK_W // IN_TILE
N_V_BLK = DIFF_V_W // IN_TILE


assert N_K_BLK == 1 and N_V_BLK == 1


def _inproj_body(x_ref, nw_ref, w_ref, wgz_ref, wgk_ref, bgk_ref, pk_in, pv_in, sk_in, sv_in,
                 main_ref, kvb_ref, vt_ref, g_ref, pk_ref, pv_ref, sk_ref, sv_ref,
                 xn_ref, y_ref, sem, *, layer, split, seq_s):
    del pk_in, pv_in, sk_in, sv_in
    i = pl.program_id(0)
    j = pl.program_id(1)
    n_i = pl.num_programs(0)
    tm = x_ref.shape[0]

    @pl.when(j == 0)
    def _():
        x = x_ref[...]
        xn = ((x * _rms_scale(x)) * nw_ref[...]).astype(BF16)
        xn_ref[...] = xn
        gz = _dot_nt(xn, wgz_ref[...])
        z = _dot(gz.astype(BF16), wgk_ref[...]) + bgk_ref[...]
        g_ref[...] = (jnp.minimum(z, 0.0) - jnp.log1p(jnp.exp(-jnp.abs(z)))) * (1.0 / GLA_GATE_NORMALIZER)

    y = _dot_nt(xn_ref[...], w_ref[...])

    @pl.when(j < N_MAIN_BLK)
    def _():
        main_ref[...] = y

    def row_copies(slot, tile, dst_p, dst_s, act):
        heads = [slice(h * DIFF_DV, (h + 1) * DIFF_DV) for h in range(DIFF_HEADS)]

        @pl.when(tile < split)
        def _():
            for h, cols in enumerate(heads):
                act(pltpu.make_async_copy(y_ref.at[slot, :, cols],
                                          dst_p.at[layer, 0, pl.ds(tile * tm, tm), h, :], sem.at[slot]))

        @pl.when(tile >= split)
        def _():
            for b in range(tm // seq_s):
                for h, cols in enumerate(heads):
                    act(pltpu.make_async_copy(y_ref.at[slot, b * seq_s:(b + 1) * seq_s, cols],
                                              dst_s.at[layer, (tile - split) * (tm // seq_s) + b, :, h, :],
                                              sem.at[slot]))

    def kv_step(slot, dst_p, dst_s):
        @pl.when(i > 0)
        def _():
            row_copies(slot, i - 1, dst_p, dst_s, lambda c: c.wait())

        y_ref[slot] = y
        kvb_ref[...] = y.astype(BF16)
        if slot == 1:
            vt_ref[0] = y.T.astype(BF16)
        row_copies(slot, i, dst_p, dst_s, lambda c: c.start())

        @pl.when(i == n_i - 1)
        def _():
            row_copies(slot, i, dst_p, dst_s, lambda c: c.wait())

    @pl.when(j == N_MAIN_BLK)
    def _():
        kv_step(0, pk_ref, sk_ref)

    @pl.when(j == N_MAIN_BLK + 1)
    def _():
        kv_step(1, pv_ref, sv_ref)


def _inproj(x, nw, w_main, w_gz, w_gk, b_gk, kv_out, *, layer, split, seq_s):
    n, d = x.shape
    tm = _row_tile(n)
    assert tm % seq_s == 0 and kv_out[0].shape[1] == 1
    nj = N_MAIN_BLK + 2
    k0 = N_MAIN_BLK
    anyspace = pl.BlockSpec(memory_space=pl.ANY)
    outs = pl.pallas_call(
        functools.partial(_inproj_body, layer=layer, split=split, seq_s=seq_s),
        grid=(n // tm, nj),
        in_specs=[
            pl.BlockSpec((tm, d), lambda i, j: (i, 0)),
            pl.BlockSpec((1, d), lambda i, j: (0, 0)),
            pl.BlockSpec((None, IN_TILE, d), lambda i, j: (layer, j, 0)),
            pl.BlockSpec((None, LANE, d), lambda i, j: (layer, 0, 0)),
            pl.BlockSpec((LANE, GLA_K_W), lambda i, j: (0, 0)),
            pl.BlockSpec((1, GLA_K_W), lambda i, j: (0, 0)),
            anyspace, anyspace, anyspace, anyspace,
        ],
        out_specs=[
            pl.BlockSpec((tm, IN_TILE), lambda i, j: (i, jnp.minimum(j, k0 - 1))),
            pl.BlockSpec((tm, IN_TILE), lambda i, j: (i, jnp.clip(j - k0, 0, 1))),
            pl.BlockSpec((1, IN_TILE, tm), lambda i, j: (i, 0, 0)),
            pl.BlockSpec((tm, GLA_K_W), lambda i, j: (i, 0)),
            anyspace, anyspace, anyspace, anyspace,
        ],
        out_shape=[
            jax.ShapeDtypeStruct((n, MAIN_W), F32),
            jax.ShapeDtypeStruct((n, DIFF_QK_W + DIFF_V_W), BF16),
            jax.ShapeDtypeStruct((n // tm, DIFF_V_W, tm), BF16),
            jax.ShapeDtypeStruct((n, GLA_K_W), F32),
        ] + [jax.ShapeDtypeStruct(a.shape, a.dtype) for a in kv_out],
        scratch_shapes=[pltpu.VMEM((tm, d), BF16), pltpu.VMEM((2, tm, IN_TILE), F32),
                        pltpu.SemaphoreType.DMA((2,))],
        input_output_aliases={6: 4, 7: 5, 8: 6, 9: 7},
        compiler_params=_params("arbitrary", "arbitrary"),
        name="inproj",
    )(x, nw, w_main, w_gz, w_gk, b_gk, *kv_out)
    return outs[:4], outs[4:]


def _split3(a):
    hi = a.astype(BF16)
    r = a - hi.astype(F32)
    mid = r.astype(BF16)
    lo = (r - mid.astype(F32)).astype(BF16)
    return hi, mid, lo


def _gla_body(q_ref, k_ref, v_ref, r_ref, g_ref, s0_ref, nw_ref, mix_in_ref, o_ref, s_ref, st_ref, *, chunk):
    del mix_in_ref
    t = pl.program_id(1)
    rows = q_ref.shape[0]
    shift = chunk.bit_length() - 1
    assert chunk == 1 << shift

    @pl.when(t == 0)
    def _():
        for h in range(GLA_HEADS):
            st_ref[h] = s0_ref[0, h].T

    ri = lax.broadcasted_iota(jnp.int32, (rows, rows), 0)
    ci = lax.broadcasted_iota(jnp.int32, (rows, rows), 1)
    causal = jnp.logical_and((ri >> shift) == (ci >> shift), ci <= ri)
    tri = causal.astype(BF16)

    g_hi, g_mid, g_lo = _split3(g_ref[...])
    b_all = _dot(tri, g_hi) + _dot(tri, g_mid) + _dot(tri, g_lo)

    for h in range(GLA_HEADS):
        kc = slice(h * GLA_DK, (h + 1) * GLA_DK)
        vc = slice(h * GLA_DV, (h + 1) * GLA_DV)
        b = b_all[:, kc]
        k = k_ref[:, kc]
        v = v_ref[:, vc].astype(BF16)
        qe = (q_ref[:, kc] * (GLA_DK ** -0.5) * jnp.exp(b)).astype(BF16)
        ke = (k * jnp.exp(-b)).astype(BF16)
        a = jnp.where(causal, _dot_nt(qe, ke), 0.0).astype(BF16)
        o_in = _dot(a, v)
        for c in range(rows // chunk):
            sl = slice(c * chunk, (c + 1) * chunk)
            b_last = b[(c + 1) * chunk - 1:(c + 1) * chunk, :]
            kd = (k[sl] * jnp.exp(b_last - b[sl])).astype(BF16)
            st = st_ref[h]
            o = _dot_nt(qe[sl], st.astype(BF16)) + o_in[sl]
            st_ref[h] = st * jnp.exp(b_last) + _dot_tn(v[sl], kd)
            o = (o * _rms_scale(o)) * nw_ref[...]
            r = r_ref[sl, vc]
            o_ref[sl, vc] = (o * (r * jax.nn.sigmoid(r))).astype(BF16)

    @pl.when(t == pl.num_programs(1) - 1)
    def _():
        for h in range(GLA_HEADS):
            s_ref[0, h] = st_ref[h].T


def _gla(main, g, s0, nw, mix, *, layer, row_off, batch, seq, chunk, rows):
    assert seq % rows == 0 and rows % chunk == 0 and row_off % rows == 0
    nt = seq // rows
    rb0 = row_off // rows

    def rowblk(b, t):
        return rb0 + b * nt + t

    state_spec = pl.BlockSpec((1, GLA_HEADS, GLA_DK, GLA_DV), lambda b, t: (b, 0, 0, 0))
    state_in = pl.BlockSpec((None, 1, GLA_HEADS, GLA_DK, GLA_DV), lambda b, t: (layer, b, 0, 0, 0))
    out, s = pl.pallas_call(
        functools.partial(_gla_body, chunk=chunk),
        grid=(batch, nt),
        in_specs=[
            pl.BlockSpec((rows, GLA_K_W), lambda b, t: (rowblk(b, t), 0)),
            pl.BlockSpec((rows, GLA_K_W), lambda b, t: (rowblk(b, t), 1)),
            pl.BlockSpec((rows, GLA_V_W), lambda b, t: (rowblk(b, t), 2 * GLA_K_W // GLA_V_W)),
            pl.BlockSpec((rows, GLA_V_W), lambda b, t: (rowblk(b, t), 2 * GLA_K_W // GLA_V_W + 1)),
            pl.BlockSpec((rows, GLA_K_W), lambda b, t: (rowblk(b, t), 0)),
            state_in,
            pl.BlockSpec((1, GLA_DV), lambda b, t: (0, 0)),
            pl.BlockSpec(memory_space=pl.ANY),
        ],
        out_specs=[
            pl.BlockSpec((rows, GLA_V_W), lambda b, t: (rowblk(b, t), 0)),
            state_spec,
        ],
        out_shape=[
            jax.ShapeDtypeStruct(mix.shape, mix.dtype),
            jax.ShapeDtypeStruct((batch, GLA_HEADS, GLA_DK, GLA_DV), F32),
        ],
        scratch_shapes=[pltpu.VMEM((GLA_HEADS, GLA_DV, GLA_DK), F32)],
        input_output_aliases={7: 0},
        compiler_params=_params("parallel", "arbitrary"),
        name="gla",
    )(main, main, main, main, g, s0, nw, mix)
    return out, s


def _bucket_thresholds():
    half = N_BUCKETS // 2
    m = half // 2
    e = half - m
    thr = []
    for kk in range(1, e):
        n = m
        while n ** e * m ** kk < m ** e * MAX_DISTANCE ** kk:
            n += 1
        thr.append(n)
    return tuple(thr)


_BUCKET_THR = _bucket_thresholds()


def _t5_bucket(rel):
    half = N_BUCKETS // 2
    max_exact = half // 2
    n = jnp.abs(rel)
    large = jnp.full(rel.shape, max_exact, jnp.int32)
    for thr in _BUCKET_THR:
        large = large + (n >= thr).astype(jnp.int32)
    return jnp.where(rel > 0, half, 0) + jnp.where(n < max_exact, n, large)


def _bias_body(rb_ref, o_ref, *, q_off, k_off, k_step, key_major, scale):
    h = pl.program_id(0)
    t = pl.program_id(1)
    shape = o_ref.shape[2:]
    qpos = q_off + lax.broadcasted_iota(jnp.int32, shape, 1 if key_major else 0)
    kpos = k_off + t * k_step + lax.broadcasted_iota(jnp.int32, shape, 0 if key_major else 1)
    bucket = _t5_bucket(kpos - qpos)
    acc = jnp.zeros(shape, F32)
    for bkt in range(N_BUCKETS):
        acc = jnp.where(bucket == bkt, rb_ref[bkt, h], acc)
    visible = (kpos >> 6) <= (qpos >> 6)
    o_ref[0, 0] = jnp.where(visible, acc * scale, -jnp.inf)


assert CHUNK == 64


def _bias_table(rel_bias, *, tiles, rows, cols, q_off, k_off, k_step, key_major=False, scale=1.0):
    return pl.pallas_call(
        functools.partial(_bias_body, q_off=q_off, k_off=k_off, k_step=k_step, key_major=key_major,
                          scale=scale),
        grid=(DIFF_HEADS, tiles),
        in_specs=[pl.BlockSpec(memory_space=pltpu.SMEM)],
        out_specs=pl.BlockSpec((1, 1, rows, cols), lambda h, t: (h, t, 0, 0)),
        out_shape=jax.ShapeDtypeStruct((DIFF_HEADS, tiles, rows, cols), F32),
        compiler_params=_params("parallel", "parallel"),
        name="bias_table",
    )(rel_bias)


def _lambda(lam_ref, layer):
    lam = lam_ref[...]
    a = jnp.sum(lam[0:1, :] * lam[1:2, :], axis=-1, keepdims=True)
    b = jnp.sum(lam[2:3, :] * lam[3:4, :], axis=-1, keepdims=True)
    lam_init = 0.8 - 0.6 * math.exp(-0.3 * layer)
    return jnp.exp(a) - jnp.exp(b) + lam_init, lam_init


def _diff_finish(o1, l1, o2, l2, lam, lam_init, nw):
    o = o1 / l1 - lam * (o2 / l2)
    return ((o * _rms_scale(o)) * nw * (1.0 - lam_init)).astype(BF16)


def _attn_prompt_body(q_ref, k_ref, vt_ref, bias_ref, lam_ref, nw_ref, mix_in_ref, *rest, layer, blk, casts):
    del mix_in_ref
    n = len(casts)
    cast_srcs, o_ref, cast_dsts = rest[:n], rest[n], rest[n + 1:2 * n + 1]
    acc_ref, s_ref, p_ref = rest[2 * n + 1:2 * n + 4]
    casts.run(pl.program_id(0) * pl.num_programs(1) + pl.program_id(1), cast_srcs, cast_dsts, rest[2 * n + 4:])
    qi = pl.program_id(1)
    qt = (q_ref[...] * (DIFF_D ** -0.5 * LOG2E)).T.astype(BF16)
    qth = (qt[:DIFF_D], qt[DIFF_D:])
    acc_ref[...] = jnp.zeros_like(acc_ref)
    strips = [slice(c * SUB, (c + 1) * SUB) for c in range(blk // SUB)]

    def fold(a):
        return a.reshape(SUB // 8, 8, blk)

    p_ref[...] = jnp.zeros_like(p_ref)

    def step(kj, stats):
        kb = k_ref[pl.ds(pl.multiple_of(kj * blk, blk), blk), :]
        vt_prev = vt_ref[jnp.maximum(kj - 1, 0)]
        t = jnp.minimum(qi - kj, 2)
        for half in range(2):
            s_ref[half] = _dot(kb[:, half * DIFF_D:(half + 1) * DIFF_D], qth[half]) + bias_ref[0, t]
        for half in range(2):
            acc_ref[half] += _dot(vt_prev, p_ref[half])
        out = []
        for half in range(2):
            m, l = stats[half]
            bm = jnp.max(fold(s_ref[half, strips[0], :]), axis=0)
            for c in strips[1:]:
                bm = jnp.maximum(bm, jnp.max(fold(s_ref[half, c, :]), axis=0))
            m_new = jnp.maximum(m, jnp.max(bm, axis=0, keepdims=True))
            alpha = jnp.exp2(m - m_new)
            acc_ref[half] = alpha * acc_ref[half]
            ls = jnp.zeros((8, blk), F32)
            for c in strips:
                p = jnp.exp2(s_ref[half, c, :] - m_new)
                ls = ls + jnp.sum(fold(p), axis=0)
                p_ref[half, c, :] = p.astype(BF16)
            l = alpha * l + jnp.sum(ls, axis=0, keepdims=True)
            out.append((m_new, l))
        return tuple(out)

    init = tuple((jnp.full((1, blk), -jnp.inf, F32), jnp.zeros((1, blk), F32)) for _ in range(2))
    stats = lax.fori_loop(0, qi + 1, step, init)

    lam, lam_init = _lambda(lam_ref, layer)
    vt_last = vt_ref[qi]
    o1 = acc_ref[0] + _dot(vt_last, p_ref[0])
    o2 = acc_ref[1] + _dot(vt_last, p_ref[1])
    ot = o1 / stats[0][1] - lam * (o2 / stats[1][1])
    scale = lax.rsqrt(jnp.mean(ot * ot, axis=0, keepdims=True) + EPS)
    ot = (ot * scale) * nw_ref[...] * (1.0 - lam_init)
    o_ref[...] = ot.T.astype(BF16)


def _attn_prompt(main, kvb, vt, bias, lam, nw_col, mix, cast_weights=(), cast_layers=(), *, layer, seq):
    blk = vt.shape[2]
    assert seq % blk == 0 and blk >= MAX_DISTANCE and blk % CHUNK == 0
    q_col0 = DQ_OFF // (2 * DIFF_D)
    o_col0 = GLA_V_W // DIFF_DV
    casts = _SideCasts(cast_weights, cast_layers, DIFF_HEADS * (seq // blk))
    anyspace = pl.BlockSpec(memory_space=pl.ANY)
    outs = pl.pallas_call(
        functools.partial(_attn_prompt_body, layer=layer, blk=blk, casts=casts),
        grid=(DIFF_HEADS, seq // blk),
        in_specs=[
            pl.BlockSpec((blk, 2 * DIFF_D), lambda h, i: (i, q_col0 + h)),
            pl.BlockSpec((seq, 2 * DIFF_D), lambda h, i: (0, h)),
            pl.BlockSpec((seq // blk, DIFF_DV, blk), lambda h, i: (0, h, 0)),
            pl.BlockSpec((1, 3, blk, blk), lambda h, i: (h, 0, 0, 0)),
            pl.BlockSpec((4, DIFF_D), lambda h, i: (0, 0)),
            pl.BlockSpec((DIFF_DV, 1), lambda h, i: (0, 0)),
            anyspace,
        ] + [anyspace] * len(casts),
        out_specs=[pl.BlockSpec((blk, DIFF_DV), lambda h, i: (i, o_col0 + h))] + [anyspace] * len(casts),
        out_shape=[jax.ShapeDtypeStruct(mix.shape, mix.dtype)] + casts.out_shapes(),
        scratch_shapes=[pltpu.VMEM((2, DIFF_DV, blk), F32), pltpu.VMEM((2, blk, blk), F32),
                        pltpu.VMEM((2, blk, blk), BF16)] + casts.scratch_shapes(),
        input_output_aliases={6: 0},
        compiler_params=_params("arbitrary", "arbitrary"),
        name="attn_prompt",
    )(main, kvb, vt, bias, lam, nw_col, mix, *cast_weights)
    return outs[0], outs[1:]


def _attn_sample_body(q_ref, kn_ref, vn_ref, kc_hbm, vc_hbm, bias_ref, lam_ref, nw_ref, mix_in_ref,
                      o_ref, kbuf, vbuf, sem, *, layer, past):
    del mix_in_ref
    b = pl.program_id(0)
    h = pl.program_id(1)
    n_h = DIFF_HEADS
    step = b * n_h + h
    n_steps = pl.num_programs(0) * n_h
    slot = step % SAMPLE_SLOTS

    def copies(cstep):
        cb, ch, cslot = cstep // n_h, cstep % n_h, cstep % SAMPLE_SLOTS
        return (pltpu.make_async_copy(kc_hbm.at[layer, cb, :, ch, :], kbuf.at[cslot], sem.at[0, cslot]),
                pltpu.make_async_copy(vc_hbm.at[layer, cb, :, ch, :], vbuf.at[cslot], sem.at[1, cslot]))

    def start(cstep):
        @pl.when(cstep < n_steps)
        def _():
            for c in copies(cstep):
                c.start()

    @pl.when(step == 0)
    def _():
        for ahead in range(SAMPLE_SLOTS - 1):
            start(step + ahead)

    start(step + SAMPLE_SLOTS - 1)

    qs = (q_ref[...] * (DIFF_D ** -0.5)).astype(BF16)
    kn = kn_ref[...]
    bias_c = bias_ref[0, 0, :, :past]
    bias_n = bias_ref[0, 0, :, past:]
    for c in copies(step):
        c.wait()
    ps = []
    for half in range(2):
        cols = slice(half * DIFF_D, (half + 1) * DIFF_D)
        sc = _dot_nt(qs[:, cols], kbuf[slot, :, cols].astype(BF16)) + bias_c
        sn = _dot_nt(qs[:, cols], kn[:, cols]) + bias_n
        m = jnp.maximum(jnp.max(sc, axis=-1, keepdims=True), jnp.max(sn, axis=-1, keepdims=True))
        pc = jnp.exp(sc - m)
        pn = jnp.exp(sn - m)
        inv = 1.0 / (jnp.sum(pc, axis=-1, keepdims=True) + jnp.sum(pn, axis=-1, keepdims=True))
        ps.append((pc * inv, pn * inv))
    lam, lam_init = _lambda(lam_ref, layer)
    wc = (ps[0][0] - lam * ps[1][0]).astype(BF16)
    wn = (ps[0][1] - lam * ps[1][1]).astype(BF16)
    o = _dot(wc, vbuf[slot].astype(BF16)) + _dot(wn, vn_ref[...])
    o_ref[...] = ((o * _rms_scale(o)) * nw_ref[...] * (1.0 - lam_init)).astype(BF16)


def _attn_sample(main, kvb, cache_k, cache_v, bias, lam, nw, mix, *, layer, row_off, batch, seq):
    past = cache_k.shape[2]
    assert row_off % seq == 0
    rb0 = row_off // seq
    q_col0 = DQ_OFF // (2 * DIFF_D)
    v_col0 = DIFF_QK_W // DIFF_DV
    o_col0 = GLA_V_W // DIFF_DV
    return pl.pallas_call(
        functools.partial(_attn_sample_body, layer=layer, past=past),
        grid=(batch, DIFF_HEADS),
        in_specs=[
            pl.BlockSpec((seq, 2 * DIFF_D), lambda b, h: (rb0 + b, q_col0 + h)),
            pl.BlockSpec((seq, 2 * DIFF_D), lambda b, h: (rb0 + b, h)),
            pl.BlockSpec((seq, DIFF_DV), lambda b, h: (rb0 + b, v_col0 + h)),
            pl.BlockSpec(memory_space=pl.ANY),
            pl.BlockSpec(memory_space=pl.ANY),
            pl.BlockSpec((1, 1, seq, past + seq), lambda b, h: (h, 0, 0, 0)),
            pl.BlockSpec((4, DIFF_D), lambda b, h: (0, 0)),
            pl.BlockSpec((1, DIFF_DV), lambda b, h: (0, 0)),
            pl.BlockSpec(memory_space=pl.ANY),
        ],
        out_specs=pl.BlockSpec((seq, DIFF_DV), lambda b, h: (rb0 + b, o_col0 + h)),
        out_shape=jax.ShapeDtypeStruct(mix.shape, mix.dtype),
        scratch_shapes=[
            pltpu.VMEM((SAMPLE_SLOTS, past, 2 * DIFF_D), F32),
            pltpu.VMEM((SAMPLE_SLOTS, past, DIFF_DV), F32),
            pltpu.SemaphoreType.DMA((2, SAMPLE_SLOTS)),
        ],
        input_output_aliases={8: 0},
        compiler_params=_params("arbitrary", "arbitrary"),
        name="attn_sample",
    )(main, kvb, kvb, cache_k, cache_v, bias, lam, nw, mix)


def _outproj_body(o_ref, w_ref, x_ref, y_ref):
    y_ref[...] = x_ref[...] + _dot(o_ref[...], w_ref[...])


def _outproj(mix, w, x):
    n, d = x.shape
    tm = _row_tile(n)
    return pl.pallas_call(
        _outproj_body,
        grid=(n // tm,),
        in_specs=[
            pl.BlockSpec((tm, MIX_WIDTH), lambda i: (i, 0)),
            pl.BlockSpec((MIX_WIDTH, d), lambda i: (0, 0)),
            pl.BlockSpec((tm, d), lambda i: (i, 0)),
        ],
        out_specs=pl.BlockSpec((tm, d), lambda i: (i, 0)),
        out_shape=jax.ShapeDtypeStruct((n, d), F32),
        compiler_params=_params("parallel"),
        name="outproj",
    )(mix, w, x)


def kernel(x_prompt, x_sample, cache_k, cache_v, state_gla, ffn1_norm, ffn1_w_gate, ffn1_w_up, ffn1_w_down,
           mix_norm, w_in, gla_w_gk, gla_b_gk, gla_norm, diff_lambda, diff_norm, w_out,
           ffn2_norm, ffn2_w_gate, ffn2_w_up, ffn2_w_down, rel_bias, final_norm):
    pb, ps, d = x_prompt.shape
    sb, ss, _ = x_sample.shape
    depth = w_in.shape[0]
    past = cache_k.shape[2]
    assert pb == 1
    n_p, n_s = pb * ps, sb * ss
    xs = [x_prompt.reshape(n_p, d), x_sample.reshape(n_s, d)]

    blk = _row_tile(n_p + n_s)
    bias_p = _bias_table(rel_bias, tiles=3, rows=blk, cols=blk, q_off=0, k_off=0, k_step=-blk,
                         key_major=True, scale=LOG2E)
    bias_s = _bias_table(rel_bias, tiles=1, rows=ss, cols=past + ss, q_off=past, k_off=0, k_step=0)
    zero_state = jnp.zeros((1, pb, GLA_HEADS, GLA_DK, GLA_DV), F32)
    row = lambda a: a.reshape(1, -1)

    ffn1_w = (ffn1_w_gate, ffn1_w_up, ffn1_w_down)
    ffn2_w = (ffn2_w_gate, ffn2_w_up, ffn2_w_down)
    w_main, w_gz = _cast_w_in(jnp.swapaxes(w_in, 1, 2))
    f1 = {0: [_cast_bf16(w, 0) for w in ffn1_w]}
    later = [(w, l) for l in range(depth) for w in ((ffn1_w if l else ()) + ffn2_w + (w_out,))]
    f2, w_o = {}, {}

    kv_out = [jnp.zeros((depth, pb, ps, DIFF_HEADS, DIFF_DV), F32) for _ in range(2)]
    kv_out += [jnp.zeros((depth, sb, ss, DIFF_HEADS, DIFF_DV), F32) for _ in range(2)]
    p_states, s_states = [], []
    mix = jnp.zeros((n_p + n_s, MIX_WIDTH), BF16)
    for l in range(depth):
        w_gk = jnp.pad(gla_w_gk[l], ((0, LANE - GLA_GK_RANK), (0, 0))).astype(BF16)

        x, = _ffn(xs, row(ffn1_norm[l]), *f1[l], row(final_norm), False, [n_p + n_s])
        (main, kvb, vt, g), kv_out = _inproj(x, row(mix_norm[l]), w_main, w_gz, w_gk, row(gla_b_gk[l]), kv_out,
                                             layer=l, split=n_p // blk, seq_s=ss)

        mix, sp = _gla(main, g, zero_state, row(gla_norm[l]), mix,
                       layer=0, row_off=0, batch=pb, seq=ps, chunk=CHUNK, rows=min(GLA_ROWS, ps))
        mix, s_s = _gla(main, g, state_gla, row(gla_norm[l]), mix,
                        layer=l, row_off=n_p, batch=sb, seq=ss, chunk=ss, rows=ss)
        mix, cast = _attn_prompt(main, kvb, vt, bias_p, diff_lambda[l], diff_norm[l].reshape(-1, 1), mix,
                                 [w for w, _ in later] if l == 0 else [], [k for _, k in later] if l == 0 else [],
                                 layer=l, seq=ps)
        cast = list(cast)
        for k in range(depth if l == 0 else 0):
            if k:
                f1[k] = [cast.pop(0) for _ in ffn1_w]
            f2[k] = [cast.pop(0) for _ in ffn2_w]
            w_o[k] = cast.pop(0)
        mix = _attn_sample(main, kvb, cache_k, cache_v, bias_s, diff_lambda[l], row(diff_norm[l]), mix,
                           layer=l, row_off=n_p, batch=sb, seq=ss)

        x = _outproj(mix, w_o[l], x)
        last = l == depth - 1
        xs = _ffn([x], row(ffn2_norm[l]), *f2[l], row(final_norm), last, [n_p, n_s] if last else [n_p + n_s])
        p_states.append(sp)
        s_states.append(s_s)

    prompt_k, prompt_v, sample_k, sample_v = kv_out
    return (xs[0].reshape(pb, ps, d), xs[1].reshape(sb, ss, d), prompt_k, prompt_v, jnp.stack(p_states),
            sample_k, sample_v, jnp.stack(s_states))
```

```python
import functools
import math

import jax
import jax.numpy as jnp
from jax import lax
from jax.experimental import pallas as pl
from jax.experimental.pallas import tpu as pltpu

F32 = jnp.float32
BF16 = jnp.bfloat16

EPS = 1e-6
CHUNK = 64
GLA_HEADS = 4
GLA_DK = 128
GLA_DV = 256
GLA_GK_RANK = 16
GLA_GATE_NORMALIZER = 16.0
DIFF_HEADS = 4
DIFF_D = 128
DIFF_DV = 2 * DIFF_D
N_BUCKETS = 32
MAX_DISTANCE = 128

GLA_K_W = GLA_HEADS * GLA_DK
GLA_V_W = GLA_HEADS * GLA_DV
DIFF_QK_W = DIFF_HEADS * 2 * DIFF_D
DIFF_V_W = DIFF_HEADS * DIFF_DV
MIX_WIDTH = GLA_V_W + DIFF_V_W
GZ_OFF = 2 * GLA_K_W + 2 * GLA_V_W
MAIN_W = GZ_OFF + DIFF_QK_W
DQ_OFF = GZ_OFF

LANE = 128
LOG2E = math.log2(math.e)
SUB = 64
VMEM_LIMIT = 56 * 1024 * 1024

ROW_TILE = 512
FFN_ROWS = 1024
FF_TILE = 512
IN_TILE = 1024
GLA_ROWS = 256
SAMPLE_SLOTS = 3


def _dot(a, b):
    return jnp.dot(a, b, preferred_element_type=F32)


def _dot_nt(a, b):
    return lax.dot_general(a, b, (((1,), (1,)), ((), ())), preferred_element_type=F32)


def _dot_tn(a, b):
    return lax.dot_general(a, b, (((0,), (0,)), ((), ())), preferred_element_type=F32)


def _params(*sem):
    return pltpu.CompilerParams(dimension_semantics=sem, vmem_limit_bytes=VMEM_LIMIT)


def _row_tile(n):
    t = ROW_TILE
    while n % t:
        t //= 2
    return t


def _rms_scale(x):
    return lax.rsqrt(jnp.mean(x * x, axis=-1, keepdims=True) + EPS)


CAST_ROWS = 256


def _cast_body(w_ref, o_ref):
    o_ref[...] = w_ref[...].astype(BF16)


def _cast_bf16(w, layer):
    _, r, c = w.shape
    return pl.pallas_call(
        _cast_body,
        grid=(r // CAST_ROWS,),
        in_specs=[pl.BlockSpec((None, CAST_ROWS, c), lambda i: (layer, i, 0))],
        out_specs=pl.BlockSpec((CAST_ROWS, c), lambda i: (i, 0)),
        out_shape=jax.ShapeDtypeStruct((r, c), BF16),
        compiler_params=_params("parallel"),
        name="cast_bf16",
    )(w)


def _cast_rows(rows, steps):
    return -(-rows // (16 * steps)) * 16


class _SideCasts:
    def __init__(self, weights, layers, steps):
        self.layers = layers
        self.steps = steps
        self.shapes = [w.shape[1:] for w in weights]
        self.rows = [_cast_rows(s[0], steps) for s in self.shapes]
        assert all(s[0] >= r and s[0] % 16 == 0 for s, r in zip(self.shapes, self.rows))

    def __len__(self):
        return len(self.shapes)

    def out_shapes(self):
        return [jax.ShapeDtypeStruct(s, BF16) for s in self.shapes]

    def scratch_shapes(self):
        n = len(self)
        if n == 0:
            return []
        return ([pltpu.VMEM((2, r, s[1]), F32) for s, r in zip(self.shapes, self.rows)]
                + [pltpu.VMEM((r, s[1]), BF16) for s, r in zip(self.shapes, self.rows)]
                + [pltpu.SemaphoreType.DMA((n, 2)), pltpu.SemaphoreType.DMA((n,))])

    def run(self, step, srcs, dsts, scratch):
        n = len(self)
        if n == 0:
            return
        in_bufs, out_bufs, in_sem, out_sem = scratch[:n], scratch[n:2 * n], scratch[2 * n], scratch[2 * n + 1]

        def start_row(k, t):
            r, total = self.rows[k], self.shapes[k][0]
            return pl.multiple_of(jnp.minimum(t * r, total - r), 16)

        def fetch(k, t):
            return pltpu.make_async_copy(srcs[k].at[self.layers[k], pl.ds(start_row(k, t), self.rows[k]), :],
                                         in_bufs[k].at[t % 2], in_sem.at[k, t % 2])

        def store(k, t):
            return pltpu.make_async_copy(out_bufs[k], dsts[k].at[pl.ds(start_row(k, t), self.rows[k]), :],
                                         out_sem.at[k])

        @pl.when(step == 0)
        def _():
            for k in range(n):
                fetch(k, step).start()

        @pl.when(step + 1 < self.steps)
        def _():
            for k in range(n):
                fetch(k, step + 1).start()

        for k in range(n):
            fetch(k, step).wait()

            @pl.when(step > 0)
            def _():
                store(k, step - 1).wait()

            out_bufs[k][...] = in_bufs[k][step % 2].astype(BF16)
            store(k, step).start()

        @pl.when(step == self.steps - 1)
        def _():
            for k in range(n):
                store(k, step).wait()


GZ_BLK = GZ_OFF // CAST_ROWS
assert GZ_OFF % CAST_ROWS == 0 and GLA_GK_RANK % 16 == 0 and GLA_GK_RANK < CAST_ROWS


def _cast_w_in_body(a_ref, b_ref, main_ref, gz_ref):
    k = pl.program_id(1)

    @pl.when(k < GZ_BLK)
    def _():
        main_ref[0] = a_ref[0].astype(BF16)

    @pl.when(k >= GZ_BLK)
    def _():
        main_ref[0, :CAST_ROWS - GLA_GK_RANK] = a_ref[0, GLA_GK_RANK:].astype(BF16)
        main_ref[0, CAST_ROWS - GLA_GK_RANK:] = b_ref[0, :GLA_GK_RANK].astype(BF16)

    @pl.when(k == GZ_BLK)
    def _():
        gz_ref[0, :GLA_GK_RANK] = a_ref[0, :GLA_GK_RANK].astype(BF16)
        gz_ref[0, GLA_GK_RANK:] = jnp.zeros((LANE - GLA_GK_RANK, a_ref.shape[2]), BF16)


def _cast_w_in(w_in_t):
    depth, c, r = w_in_t.shape
    wide = c - GLA_GK_RANK
    last = pl.cdiv(c, CAST_ROWS) - 1
    return pl.pallas_call(
        _cast_w_in_body,
        grid=(depth, wide // CAST_ROWS),
        in_specs=[pl.BlockSpec((1, CAST_ROWS, r), lambda l, k: (l, k, 0)),
                  pl.BlockSpec((1, CAST_ROWS, r), lambda l, k: (l, jnp.clip(k + 1, GZ_BLK + 1, last), 0))],
        out_specs=[pl.BlockSpec((1, CAST_ROWS, r), lambda l, k: (l, k, 0)),
                   pl.BlockSpec((1, LANE, r), lambda l, k: (l, 0, 0))],
        out_shape=[jax.ShapeDtypeStruct((depth, wide, r), BF16),
                   jax.ShapeDtypeStruct((depth, LANE, r), BF16)],
        compiler_params=_params("parallel", "arbitrary"),
        name="cast_w_in",
    )(w_in_t, w_in_t)


def _ffn_body(*refs, final, n_in, n_out, split):
    x_refs = refs[:n_in]
    nw_ref, wg_ref, wu_ref, wd_ref, fw_ref = refs[n_in:n_in + 5]
    o_refs = refs[n_in + 5:n_in + 5 + n_out]
    scratch = refs[n_in + 5 + n_out:]
    xn_ref = scratch[0]
    acc_ref = o_refs[0] if n_out == 1 else scratch[1]
    i = pl.program_id(0)
    j = pl.program_id(1)
    tm = xn_ref.shape[0]

    if n_in == 2:
        x_buf, x_sem = scratch[-2:]

        def fetch(t, act):
            @pl.when(t < split)
            def _():
                act(pltpu.make_async_copy(x_refs[0].at[pl.ds(t * tm, tm), :], x_buf.at[t % 2], x_sem.at[t % 2]))

            @pl.when(t >= split)
            def _():
                act(pltpu.make_async_copy(x_refs[1].at[pl.ds((t - split) * tm, tm), :], x_buf.at[t % 2],
                                          x_sem.at[t % 2]))

        @pl.when(j == 0)
        def _():
            @pl.when(i == 0)
            def _():
                fetch(i, lambda c: c.start())

            @pl.when(i + 1 < pl.num_programs(0))
            def _():
                fetch(i + 1, lambda c: c.start())

            fetch(i, lambda c: c.wait())

    def load_x():
        return x_refs[0][...] if n_in == 1 else x_buf[i % 2]

    @pl.when(j == 0)
    def _():
        x = load_x()
        xn_ref[...] = ((x * _rms_scale(x)) * nw_ref[...]).astype(BF16)
        acc_ref[...] = jnp.zeros_like(acc_ref)

    xn = xn_ref[...]
    h = _dot(xn, wg_ref[...])
    u = _dot(xn, wu_ref[...])
    a = (h * jax.nn.sigmoid(h) * u).astype(BF16)
    acc_ref[...] += _dot(a, wd_ref[...])

    @pl.when(j == pl.num_programs(1) - 1)
    def _():
        y = load_x() + 0.5 * acc_ref[...]
        if final:
            y = (y * _rms_scale(y)) * fw_ref[...]
        if n_out == 1:
            o_refs[0][...] = y
        else:
            @pl.when(i < split)
            def _():
                o_refs[0][...] = y

            @pl.when(i >= split)
            def _():
                o_refs[1][...] = y


def _ffn(xs, nw, wg, wu, wd, fw, final, out_rows):
    d = xs[0].shape[1]
    n = sum(x.shape[0] for x in xs)
    assert n == sum(out_rows)
    f = wg.shape[1]
    tm = FFN_ROWS if len(out_rows) == 1 and n % FFN_ROWS == 0 else _row_tile(n)
    first = xs[0].shape[0] if len(xs) == 2 else out_rows[0]
    if first % tm:
        tm = _row_tile(n)
    assert first % tm == 0
    split = first // tm
    parts = [lambda i, j: (jnp.minimum(i, split - 1), 0), lambda i, j: (jnp.maximum(i - split, 0), 0)]
    whole = [lambda i, j: (i, 0)]
    outs = pl.pallas_call(
        functools.partial(_ffn_body, final=final, n_in=len(xs), n_out=len(out_rows), split=split),
        grid=(n // tm, f // FF_TILE),
        in_specs=([pl.BlockSpec(memory_space=pl.ANY)] * 2 if len(xs) == 2 else [pl.BlockSpec((tm, d), whole[0])]) + [
            pl.BlockSpec((1, d), lambda i, j: (0, 0)),
            pl.BlockSpec((d, FF_TILE), lambda i, j: (0, j)),
            pl.BlockSpec((d, FF_TILE), lambda i, j: (0, j)),
            pl.BlockSpec((FF_TILE, d), lambda i, j: (j, 0)),
            pl.BlockSpec((1, d), lambda i, j: (0, 0)),
        ],
        out_specs=[pl.BlockSpec((tm, d), m) for m in (parts if len(out_rows) == 2 else whole)],
        out_shape=[jax.ShapeDtypeStruct((r, d), F32) for r in out_rows],
        scratch_shapes=([pltpu.VMEM((tm, d), BF16)] + ([pltpu.VMEM((tm, d), F32)] if len(out_rows) == 2 else [])
                        + ([pltpu.VMEM((2, tm, d), F32), pltpu.SemaphoreType.DMA((2,))] if len(xs) == 2 else [])),
        compiler_params=_params("parallel" if len(xs) == len(out_rows) == 1 else "arbitrary", "arbitrary"),
        name="ffn",
    )(*xs, nw, wg, wu, wd, fw)
    return outs


N_MAIN_BLK = MAIN_W // IN_TILE
N_K_BLK = DIFF_QK_W // IN_TILE
N_V_BLK = DIFF_V_W // IN_TILE


assert N_K_BLK == 1 and N_V_BLK == 1


def _inproj_body(x_ref, nw_ref, w_ref, wgz_ref, wgk_ref, bgk_ref, *rest, layer, split, seq_s):
    main_ref, kvb_ref, vt_ref, g_ref, pk_ref, pv_ref, sk_ref, sv_ref, xn_ref, y_ref, sem = rest[-11:]
    i = pl.program_id(0)
    j = pl.program_id(1)
    n_i = pl.num_programs(0)
    tm = x_ref.shape[0]

    @pl.when(j == 0)
    def _():
        x = x_ref[...]
        xn = ((x * _rms_scale(x)) * nw_ref[...]).astype(BF16)
        xn_ref[...] = xn
        gz = _dot_nt(xn, wgz_ref[...])
        z = _dot(gz.astype(BF16), wgk_ref[...]) + bgk_ref[...]
        g_ref[...] = (jnp.minimum(z, 0.0) - jnp.log1p(jnp.exp(-jnp.abs(z)))) * (1.0 / GLA_GATE_NORMALIZER)

    y = _dot_nt(xn_ref[...], w_ref[...])

    @pl.when(j < N_MAIN_BLK)
    def _():
        main_ref[...] = y

    def row_copies(slot, tile, dst_p, dst_s, act):
        heads = [slice(h * DIFF_DV, (h + 1) * DIFF_DV) for h in range(DIFF_HEADS)]

        @pl.when(tile < split)
        def _():
            for h, cols in enumerate(heads):
                act(pltpu.make_async_copy(y_ref.at[slot, :, cols],
                                          dst_p.at[layer, 0, pl.ds(tile * tm, tm), h, :], sem.at[slot]))

        @pl.when(tile >= split)
        def _():
            for b in range(tm // seq_s):
                for h, cols in enumerate(heads):
                    act(pltpu.make_async_copy(y_ref.at[slot, b * seq_s:(b + 1) * seq_s, cols],
                                              dst_s.at[layer, (tile - split) * (tm // seq_s) + b, :, h, :],
                                              sem.at[slot]))

    def kv_step(slot, dst_p, dst_s):
        @pl.when(i > 0)
        def _():
            row_copies(slot, i - 1, dst_p, dst_s, lambda c: c.wait())

        y_ref[slot] = y
        kvb_ref[...] = y.astype(BF16)
        if slot == 1:
            vt_ref[0] = y.T.astype(BF16)
        row_copies(slot, i, dst_p, dst_s, lambda c: c.start())

        @pl.when(i == n_i - 1)
        def _():
            row_copies(slot, i, dst_p, dst_s, lambda c: c.wait())

    @pl.when(j == N_MAIN_BLK)
    def _():
        kv_step(0, pk_ref, sk_ref)

    @pl.when(j == N_MAIN_BLK + 1)
    def _():
        kv_step(1, pv_ref, sv_ref)


def _inproj(x, nw, w_main, w_gz, w_gk, b_gk, kv_out, *, layer, split, seq_s):
    n, d = x.shape
    tm = _row_tile(n)
    assert tm % seq_s == 0 and kv_out[0].shape[1] == 1
    nj = N_MAIN_BLK + 2
    k0 = N_MAIN_BLK
    anyspace = pl.BlockSpec(memory_space=pl.ANY)
    outs = pl.pallas_call(
        functools.partial(_inproj_body, layer=layer, split=split, seq_s=seq_s),
        grid=(n // tm, nj),
        in_specs=[
            pl.BlockSpec((tm, d), lambda i, j: (i, 0)),
            pl.BlockSpec((1, d), lambda i, j: (0, 0)),
            pl.BlockSpec((None, IN_TILE, d), lambda i, j: (layer, j, 0)),
            pl.BlockSpec((None, LANE, d), lambda i, j: (layer, 0, 0)),
            pl.BlockSpec((LANE, GLA_K_W), lambda i, j: (0, 0)),
            pl.BlockSpec((1, GLA_K_W), lambda i, j: (0, 0)),
        ] + [anyspace] * len(kv_out),
        out_specs=[
            pl.BlockSpec((tm, IN_TILE), lambda i, j: (i, jnp.minimum(j, k0 - 1))),
            pl.BlockSpec((tm, IN_TILE), lambda i, j: (i, jnp.clip(j - k0, 0, 1))),
            pl.BlockSpec((1, IN_TILE, tm), lambda i, j: (i, 0, 0)),
            pl.BlockSpec((tm, GLA_K_W), lambda i, j: (i, 0)),
            anyspace, anyspace, anyspace, anyspace,
        ],
        out_shape=[
            jax.ShapeDtypeStruct((n, MAIN_W), F32),
            jax.ShapeDtypeStruct((n, DIFF_QK_W + DIFF_V_W), BF16),
            jax.ShapeDtypeStruct((n // tm, DIFF_V_W, tm), BF16),
            jax.ShapeDtypeStruct((n, GLA_K_W), F32),
        ] + [jax.ShapeDtypeStruct(a.shape, a.dtype) for a in kv_out],
        scratch_shapes=[pltpu.VMEM((tm, d), BF16), pltpu.VMEM((2, tm, IN_TILE), F32),
                        pltpu.SemaphoreType.DMA((2,))],
        input_output_aliases={6 + k: 4 + k for k in range(len(kv_out))},
        compiler_params=_params("arbitrary", "arbitrary"),
        name="inproj",
    )(x, nw, w_main, w_gz, w_gk, b_gk, *kv_out)
    return outs[:4], outs[4:]


def _split3(a):
    hi = a.astype(BF16)
    r = a - hi.astype(F32)
    mid = r.astype(BF16)
    lo = (r - mid.astype(F32)).astype(BF16)
    return hi, mid, lo


def _gla_body(q_ref, k_ref, v_ref, r_ref, g_ref, s0_ref, nw_ref, mix_in_ref, o_ref, s_ref, st_ref, *, chunk):
    del mix_in_ref
    t = pl.program_id(1)
    rows = q_ref.shape[0]
    shift = chunk.bit_length() - 1
    assert chunk == 1 << shift

    @pl.when(t == 0)
    def _():
        for h in range(GLA_HEADS):
            st_ref[h] = s0_ref[0, h].T

    ri = lax.broadcasted_iota(jnp.int32, (rows, rows), 0)
    ci = lax.broadcasted_iota(jnp.int32, (rows, rows), 1)
    causal = jnp.logical_and((ri >> shift) == (ci >> shift), ci <= ri)
    tri = causal.astype(BF16)

    g_hi, g_mid, g_lo = _split3(g_ref[...])
    b_all = _dot(tri, g_hi) + _dot(tri, g_mid) + _dot(tri, g_lo)

    for h in range(GLA_HEADS):
        kc = slice(h * GLA_DK, (h + 1) * GLA_DK)
        vc = slice(h * GLA_DV, (h + 1) * GLA_DV)
        b = b_all[:, kc]
        k = k_ref[:, kc]
        v = v_ref[:, vc].astype(BF16)
        qe = (q_ref[:, kc] * (GLA_DK ** -0.5) * jnp.exp(b)).astype(BF16)
        ke = (k * jnp.exp(-b)).astype(BF16)
        a = jnp.where(causal, _dot_nt(qe, ke), 0.0).astype(BF16)
        o_in = _dot(a, v)
        for c in range(rows // chunk):
            sl = slice(c * chunk, (c + 1) * chunk)
            b_last = b[(c + 1) * chunk - 1:(c + 1) * chunk, :]
            kd = (k[sl] * jnp.exp(b_last - b[sl])).astype(BF16)
            st = st_ref[h]
            o = _dot_nt(qe[sl], st.astype(BF16)) + o_in[sl]
            st_ref[h] = st * jnp.exp(b_last) + _dot_tn(v[sl], kd)
            o = (o * _rms_scale(o)) * nw_ref[...]
            r = r_ref[sl, vc]
            o_ref[sl, vc] = (o * (r * jax.nn.sigmoid(r))).astype(BF16)

    @pl.when(t == pl.num_programs(1) - 1)
    def _():
        for h in range(GLA_HEADS):
            s_ref[0, h] = st_ref[h].T


def _gla(main, g, s0, nw, mix, *, layer, row_off, batch, seq, chunk, rows):
    assert seq % rows == 0 and rows % chunk == 0 and row_off % rows == 0
    nt = seq // rows
    rb0 = row_off // rows

    def rowblk(b, t):
        return rb0 + b * nt + t

    state_spec = pl.BlockSpec((1, GLA_HEADS, GLA_DK, GLA_DV), lambda b, t: (b, 0, 0, 0))
    state_in = pl.BlockSpec((None, 1, GLA_HEADS, GLA_DK, GLA_DV), lambda b, t: (layer, b, 0, 0, 0))
    out, s = pl.pallas_call(
        functools.partial(_gla_body, chunk=chunk),
        grid=(batch, nt),
        in_specs=[
            pl.BlockSpec((rows, GLA_K_W), lambda b, t: (rowblk(b, t), 0)),
            pl.BlockSpec((rows, GLA_K_W), lambda b, t: (rowblk(b, t), 1)),
            pl.BlockSpec((rows, GLA_V_W), lambda b, t: (rowblk(b, t), 2 * GLA_K_W // GLA_V_W)),
            pl.BlockSpec((rows, GLA_V_W), lambda b, t: (rowblk(b, t), 2 * GLA_K_W // GLA_V_W + 1)),
            pl.BlockSpec((rows, GLA_K_W), lambda b, t: (rowblk(b, t), 0)),
            state_in,
            pl.BlockSpec((1, GLA_DV), lambda b, t: (0, 0)),
            pl.BlockSpec(memory_space=pl.ANY),
        ],
        out_specs=[
            pl.BlockSpec((rows, GLA_V_W), lambda b, t: (rowblk(b, t), 0)),
            state_spec,
        ],
        out_shape=[
            jax.ShapeDtypeStruct(mix.shape, mix.dtype),
            jax.ShapeDtypeStruct((batch, GLA_HEADS, GLA_DK, GLA_DV), F32),
        ],
        scratch_shapes=[pltpu.VMEM((GLA_HEADS, GLA_DV, GLA_DK), F32)],
        input_output_aliases={7: 0},
        compiler_params=_params("parallel", "arbitrary"),
        name="gla",
    )(main, main, main, main, g, s0, nw, mix)
    return out, s


def _bucket_thresholds():
    half = N_BUCKETS // 2
    m = half // 2
    e = half - m
    thr = []
    for kk in range(1, e):
        n = m
        while n ** e * m ** kk < m ** e * MAX_DISTANCE ** kk:
            n += 1
        thr.append(n)
    return tuple(thr)


_BUCKET_THR = _bucket_thresholds()


def _t5_bucket(rel):
    half = N_BUCKETS // 2
    max_exact = half // 2
    n = jnp.abs(rel)
    large = jnp.full(rel.shape, max_exact, jnp.int32)
    for thr in _BUCKET_THR:
        large = large + (n >= thr).astype(jnp.int32)
    return jnp.where(rel > 0, half, 0) + jnp.where(n < max_exact, n, large)


def _bias_body(rb_ref, o_ref, *, q_off, k_off, k_step, key_major, scale):
    h = pl.program_id(0)
    t = pl.program_id(1)
    shape = o_ref.shape[2:]
    qpos = q_off + lax.broadcasted_iota(jnp.int32, shape, 1 if key_major else 0)
    kpos = k_off + t * k_step + lax.broadcasted_iota(jnp.int32, shape, 0 if key_major else 1)
    bucket = _t5_bucket(kpos - qpos)
    acc = jnp.zeros(shape, F32)
    for bkt in range(N_BUCKETS):
        acc = jnp.where(bucket == bkt, rb_ref[bkt, h], acc)
    visible = (kpos >> 6) <= (qpos >> 6)
    o_ref[0, 0] = jnp.where(visible, acc * scale, -jnp.inf)


assert CHUNK == 64


def _bias_table(rel_bias, *, tiles, rows, cols, q_off, k_off, k_step, key_major=False, scale=1.0):
    return pl.pallas_call(
        functools.partial(_bias_body, q_off=q_off, k_off=k_off, k_step=k_step, key_major=key_major,
                          scale=scale),
        grid=(DIFF_HEADS, tiles),
        in_specs=[pl.BlockSpec(memory_space=pltpu.SMEM)],
        out_specs=pl.BlockSpec((1, 1, rows, cols), lambda h, t: (h, t, 0, 0)),
        out_shape=jax.ShapeDtypeStruct((DIFF_HEADS, tiles, rows, cols), F32),
        compiler_params=_params("parallel", "parallel"),
        name="bias_table",
    )(rel_bias)


def _lambda(lam_ref, layer):
    lam = lam_ref[...]
    a = jnp.sum(lam[0:1, :] * lam[1:2, :], axis=-1, keepdims=True)
    b = jnp.sum(lam[2:3, :] * lam[3:4, :], axis=-1, keepdims=True)
    lam_init = 0.8 - 0.6 * math.exp(-0.3 * layer)
    return jnp.exp(a) - jnp.exp(b) + lam_init, lam_init


def _diff_finish(o1, l1, o2, l2, lam, lam_init, nw):
    o = o1 / l1 - lam * (o2 / l2)
    return ((o * _rms_scale(o)) * nw * (1.0 - lam_init)).astype(BF16)


def _attn_prompt_body(q_ref, k_ref, vt_ref, bias_ref, lam_ref, nw_ref, mix_in_ref, *rest, layer, blk, casts):
    del mix_in_ref
    n = len(casts)
    cast_srcs, o_ref, cast_dsts = rest[:n], rest[n], rest[n + 1:2 * n + 1]
    acc_ref, s_ref, p_ref = rest[2 * n + 1:2 * n + 4]
    casts.run(pl.program_id(0) * pl.num_programs(1) + pl.program_id(1), cast_srcs, cast_dsts, rest[2 * n + 4:])
    qi = pl.program_id(1)
    qt = (q_ref[...] * (DIFF_D ** -0.5 * LOG2E)).T.astype(BF16)
    qth = (qt[:DIFF_D], qt[DIFF_D:])
    acc_ref[...] = jnp.zeros_like(acc_ref)
    strips = [slice(c * SUB, (c + 1) * SUB) for c in range(blk // SUB)]

    def fold(a):
        return a.reshape(SUB // 8, 8, blk)

    p_ref[...] = jnp.zeros_like(p_ref)

    def step(kj, stats):
        kb = k_ref[pl.ds(pl.multiple_of(kj * blk, blk), blk), :]
        vt_prev = vt_ref[jnp.maximum(kj - 1, 0)]
        t = jnp.minimum(qi - kj, 2)
        for half in range(2):
            s_ref[half] = _dot(kb[:, half * DIFF_D:(half + 1) * DIFF_D], qth[half]) + bias_ref[0, t]
        for half in range(2):
            acc_ref[half] += _dot(vt_prev, p_ref[half])
        out = []
        for half in range(2):
            m, l = stats[half]
            bm = jnp.max(fold(s_ref[half, strips[0], :]), axis=0)
            for c in strips[1:]:
                bm = jnp.maximum(bm, jnp.max(fold(s_ref[half, c, :]), axis=0))
            m_new = jnp.maximum(m, jnp.max(bm, axis=0, keepdims=True))
            alpha = jnp.exp2(m - m_new)
            acc_ref[half] = alpha * acc_ref[half]
            ls = jnp.zeros((8, blk), F32)
            for c in strips:
                p = jnp.exp2(s_ref[half, c, :] - m_new)
                ls = ls + jnp.sum(fold(p), axis=0)
                p_ref[half, c, :] = p.astype(BF16)
            l = alpha * l + jnp.sum(ls, axis=0, keepdims=True)
            out.append((m_new, l))
        return tuple(out)

    init = tuple((jnp.full((1, blk), -jnp.inf, F32), jnp.zeros((1, blk), F32)) for _ in range(2))
    stats = lax.fori_loop(0, qi + 1, step, init)

    lam, lam_init = _lambda(lam_ref, layer)
    vt_last = vt_ref[qi]
    o1 = acc_ref[0] + _dot(vt_last, p_ref[0])
    o2 = acc_ref[1] + _dot(vt_last, p_ref[1])
    ot = o1 / stats[0][1] - lam * (o2 / stats[1][1])
    scale = lax.rsqrt(jnp.mean(ot * ot, axis=0, keepdims=True) + EPS)
    ot = (ot * scale) * nw_ref[...] * (1.0 - lam_init)
    o_ref[...] = ot.T.astype(BF16)


def _attn_prompt(main, kvb, vt, bias, lam, nw_col, mix, cast_weights=(), cast_layers=(), *, layer, seq):
    blk = vt.shape[2]
    assert seq % blk == 0 and blk >= MAX_DISTANCE and blk % CHUNK == 0
    q_col0 = DQ_OFF // (2 * DIFF_D)
    o_col0 = GLA_V_W // DIFF_DV
    casts = _SideCasts(cast_weights, cast_layers, DIFF_HEADS * (seq // blk))
    anyspace = pl.BlockSpec(memory_space=pl.ANY)
    outs = pl.pallas_call(
        functools.partial(_attn_prompt_body, layer=layer, blk=blk, casts=casts),
        grid=(DIFF_HEADS, seq // blk),
        in_specs=[
            pl.BlockSpec((blk, 2 * DIFF_D), lambda h, i: (i, q_col0 + h)),
            pl.BlockSpec((seq, 2 * DIFF_D), lambda h, i: (0, h)),
            pl.BlockSpec((seq // blk, DIFF_DV, blk), lambda h, i: (0, h, 0)),
            pl.BlockSpec((1, 3, blk, blk), lambda h, i: (h, 0, 0, 0)),
            pl.BlockSpec((4, DIFF_D), lambda h, i: (0, 0)),
            pl.BlockSpec((DIFF_DV, 1), lambda h, i: (0, 0)),
            anyspace,
        ] + [anyspace] * len(casts),
        out_specs=[pl.BlockSpec((blk, DIFF_DV), lambda h, i: (i, o_col0 + h))] + [anyspace] * len(casts),
        out_shape=[jax.ShapeDtypeStruct(mix.shape, mix.dtype)] + casts.out_shapes(),
        scratch_shapes=[pltpu.VMEM((2, DIFF_DV, blk), F32), pltpu.VMEM((2, blk, blk), F32),
                        pltpu.VMEM((2, blk, blk), BF16)] + casts.scratch_shapes(),
        input_output_aliases={6: 0},
        compiler_params=_params("arbitrary", "arbitrary"),
        name="attn_prompt",
    )(main, kvb, vt, bias, lam, nw_col, mix, *cast_weights)
    return outs[0], outs[1:]


def _attn_sample_body(q_ref, kn_ref, vn_ref, kc_hbm, vc_hbm, bias_ref, lam_ref, nw_ref, mix_in_ref,
                      o_ref, kbuf, vbuf, sem, *, layer, past):
    del mix_in_ref
    b = pl.program_id(0)
    h = pl.program_id(1)
    n_h = DIFF_HEADS
    step = b * n_h + h
    n_steps = pl.num_programs(0) * n_h
    slot = step % SAMPLE_SLOTS

    def copies(cstep):
        cb, ch, cslot = cstep // n_h, cstep % n_h, cstep % SAMPLE_SLOTS
        return (pltpu.make_async_copy(kc_hbm.at[layer, cb, :, ch, :], kbuf.at[cslot], sem.at[0, cslot]),
                pltpu.make_async_copy(vc_hbm.at[layer, cb, :, ch, :], vbuf.at[cslot], sem.at[1, cslot]))

    def start(cstep):
        @pl.when(cstep < n_steps)
        def _():
            for c in copies(cstep):
                c.start()

    @pl.when(step == 0)
    def _():
        for ahead in range(SAMPLE_SLOTS - 1):
            start(step + ahead)

    start(step + SAMPLE_SLOTS - 1)

    qs = (q_ref[...] * (DIFF_D ** -0.5)).astype(BF16)
    kn = kn_ref[...]
    bias_c = bias_ref[0, 0, :, :past]
    bias_n = bias_ref[0, 0, :, past:]
    for c in copies(step):
        c.wait()
    ps = []
    for half in range(2):
        cols = slice(half * DIFF_D, (half + 1) * DIFF_D)
        sc = _dot_nt(qs[:, cols], kbuf[slot, :, cols].astype(BF16)) + bias_c
        sn = _dot_nt(qs[:, cols], kn[:, cols]) + bias_n
        m = jnp.maximum(jnp.max(sc, axis=-1, keepdims=True), jnp.max(sn, axis=-1, keepdims=True))
        pc = jnp.exp(sc - m)
        pn = jnp.exp(sn - m)
        inv = 1.0 / (jnp.sum(pc, axis=-1, keepdims=True) + jnp.sum(pn, axis=-1, keepdims=True))
        ps.append((pc * inv, pn * inv))
    lam, lam_init = _lambda(lam_ref, layer)
    wc = (ps[0][0] - lam * ps[1][0]).astype(BF16)
    wn = (ps[0][1] - lam * ps[1][1]).astype(BF16)
    o = _dot(wc, vbuf[slot].astype(BF16)) + _dot(wn, vn_ref[...])
    o_ref[...] = ((o * _rms_scale(o)) * nw_ref[...] * (1.0 - lam_init)).astype(BF16)


def _attn_sample(main, kvb, cache_k, cache_v, bias, lam, nw, mix, *, layer, row_off, batch, seq):
    past = cache_k.shape[2]
    assert row_off % seq == 0
    rb0 = row_off // seq
    q_col0 = DQ_OFF // (2 * DIFF_D)
    v_col0 = DIFF_QK_W // DIFF_DV
    o_col0 = GLA_V_W // DIFF_DV
    return pl.pallas_call(
        functools.partial(_attn_sample_body, layer=layer, past=past),
        grid=(batch, DIFF_HEADS),
        in_specs=[
            pl.BlockSpec((seq, 2 * DIFF_D), lambda b, h: (rb0 + b, q_col0 + h)),
            pl.BlockSpec((seq, 2 * DIFF_D), lambda b, h: (rb0 + b, h)),
            pl.BlockSpec((seq, DIFF_DV), lambda b, h: (rb0 + b, v_col0 + h)),
            pl.BlockSpec(memory_space=pl.ANY),
            pl.BlockSpec(memory_space=pl.ANY),
            pl.BlockSpec((1, 1, seq, past + seq), lambda b, h: (h, 0, 0, 0)),
            pl.BlockSpec((4, DIFF_D), lambda b, h: (0, 0)),
            pl.BlockSpec((1, DIFF_DV), lambda b, h: (0, 0)),
            pl.BlockSpec(memory_space=pl.ANY),
        ],
        out_specs=pl.BlockSpec((seq, DIFF_DV), lambda b, h: (rb0 + b, o_col0 + h)),
        out_shape=jax.ShapeDtypeStruct(mix.shape, mix.dtype),
        scratch_shapes=[
            pltpu.VMEM((SAMPLE_SLOTS, past, 2 * DIFF_D), F32),
            pltpu.VMEM((SAMPLE_SLOTS, past, DIFF_DV), F32),
            pltpu.SemaphoreType.DMA((2, SAMPLE_SLOTS)),
        ],
        input_output_aliases={8: 0},
        compiler_params=_params("arbitrary", "arbitrary"),
        name="attn_sample",
    )(main, kvb, kvb, cache_k, cache_v, bias, lam, nw, mix)


def _outproj_body(o_ref, w_ref, x_ref, y_ref):
    y_ref[...] = x_ref[...] + _dot(o_ref[...], w_ref[...])


def _outproj(mix, w, x):
    n, d = x.shape
    tm = _row_tile(n)
    return pl.pallas_call(
        _outproj_body,
        grid=(n // tm,),
        in_specs=[
            pl.BlockSpec((tm, MIX_WIDTH), lambda i: (i, 0)),
            pl.BlockSpec((MIX_WIDTH, d), lambda i: (0, 0)),
            pl.BlockSpec((tm, d), lambda i: (i, 0)),
        ],
        out_specs=pl.BlockSpec((tm, d), lambda i: (i, 0)),
        out_shape=jax.ShapeDtypeStruct((n, d), F32),
        compiler_params=_params("parallel"),
        name="outproj",
    )(mix, w, x)


def kernel(x_prompt, x_sample, cache_k, cache_v, state_gla, ffn1_norm, ffn1_w_gate, ffn1_w_up, ffn1_w_down,
           mix_norm, w_in, gla_w_gk, gla_b_gk, gla_norm, diff_lambda, diff_norm, w_out,
           ffn2_norm, ffn2_w_gate, ffn2_w_up, ffn2_w_down, rel_bias, final_norm):
    pb, ps, d = x_prompt.shape
    sb, ss, _ = x_sample.shape
    depth = w_in.shape[0]
    past = cache_k.shape[2]
    assert pb == 1
    n_p, n_s = pb * ps, sb * ss
    xs = [x_prompt.reshape(n_p, d), x_sample.reshape(n_s, d)]

    blk = _row_tile(n_p + n_s)
    bias_p = _bias_table(rel_bias, tiles=3, rows=blk, cols=blk, q_off=0, k_off=0, k_step=-blk,
                         key_major=True, scale=LOG2E)
    bias_s = _bias_table(rel_bias, tiles=1, rows=ss, cols=past + ss, q_off=past, k_off=0, k_step=0)
    zero_state = jnp.zeros((1, pb, GLA_HEADS, GLA_DK, GLA_DV), F32)
    row = lambda a: a.reshape(1, -1)

    ffn1_w = (ffn1_w_gate, ffn1_w_up, ffn1_w_down)
    ffn2_w = (ffn2_w_gate, ffn2_w_up, ffn2_w_down)
    w_main, w_gz = _cast_w_in(jnp.swapaxes(w_in, 1, 2))
    f1 = {0: [_cast_bf16(w, 0) for w in ffn1_w]}
    later = [(w, l) for l in range(depth) for w in ((ffn1_w if l else ()) + ffn2_w + (w_out,))]
    f2, w_o = {}, {}

    kv_out = [jnp.zeros((depth, pb, ps, DIFF_HEADS, DIFF_DV), F32) for _ in range(2)]
    kv_out += [jnp.zeros((depth, sb, ss, DIFF_HEADS, DIFF_DV), F32) for _ in range(2)]
    p_states, s_states = [], []
    mix = jnp.zeros((n_p + n_s, MIX_WIDTH), BF16)
    for l in range(depth):
        w_gk = jnp.pad(gla_w_gk[l], ((0, LANE - GLA_GK_RANK), (0, 0))).astype(BF16)

        x, = _ffn(xs, row(ffn1_norm[l]), *f1[l], row(final_norm), False, [n_p + n_s])
        (main, kvb, vt, g), kv_out = _inproj(x, row(mix_norm[l]), w_main, w_gz, w_gk, row(gla_b_gk[l]), kv_out,
                                             layer=l, split=n_p // blk, seq_s=ss)

        mix, sp = _gla(main, g, zero_state, row(gla_norm[l]), mix,
                       layer=0, row_off=0, batch=pb, seq=ps, chunk=CHUNK, rows=min(GLA_ROWS, ps))
        mix, s_s = _gla(main, g, state_gla, row(gla_norm[l]), mix,
                        layer=l, row_off=n_p, batch=sb, seq=ss, chunk=ss, rows=ss)
        mix, cast = _attn_prompt(main, kvb, vt, bias_p, diff_lambda[l], diff_norm[l].reshape(-1, 1), mix,
                                 [w for w, _ in later] if l == 0 else [], [k for _, k in later] if l == 0 else [],
                                 layer=l, seq=ps)
        cast = list(cast)
        for k in range(depth if l == 0 else 0):
            if k:
                f1[k] = [cast.pop(0) for _ in ffn1_w]
            f2[k] = [cast.pop(0) for _ in ffn2_w]
            w_o[k] = cast.pop(0)
        mix = _attn_sample(main, kvb, cache_k, cache_v, bias_s, diff_lambda[l], row(diff_norm[l]), mix,
                           layer=l, row_off=n_p, batch=sb, seq=ss)

        x = _outproj(mix, w_o[l], x)
        last = l == depth - 1
        xs = _ffn([x], row(ffn2_norm[l]), *f2[l], row(final_norm), last, [n_p, n_s] if last else [n_p + n_s])
        p_states.append(sp)
        s_states.append(s_s)

    prompt_k, prompt_v, sample_k, sample_v = kv_out
    return (xs[0].reshape(pb, ps, d), xs[1].reshape(sb, ss, d), prompt_k, prompt_v, jnp.stack(p_states),
            sample_k, sample_v, jnp.stack(s_states))
```

```python
import functools
import math

import jax
import jax.numpy as jnp
from jax import lax
from jax.experimental import pallas as pl
from jax.experimental.pallas import tpu as pltpu

F32 = jnp.float32
BF16 = jnp.bfloat16

EPS = 1e-6
CHUNK = 64
GLA_HEADS = 4
GLA_DK = 128
GLA_DV = 256
GLA_GK_RANK = 16
GLA_GATE_NORMALIZER = 16.0
DIFF_HEADS = 4
DIFF_D = 128
DIFF_DV = 2 * DIFF_D
N_BUCKETS = 32
MAX_DISTANCE = 128

GLA_K_W = GLA_HEADS * GLA_DK
GLA_V_W = GLA_HEADS * GLA_DV
DIFF_QK_W = DIFF_HEADS * 2 * DIFF_D
DIFF_V_W = DIFF_HEADS * DIFF_DV
MIX_WIDTH = GLA_V_W + DIFF_V_W
GZ_OFF = 2 * GLA_K_W + 2 * GLA_V_W
MAIN_W = GZ_OFF + DIFF_QK_W
DQ_OFF = GZ_OFF

LANE = 128
LOG2E = math.log2(math.e)
SUB = 64
VMEM_LIMIT = 56 * 1024 * 1024
FFN_VMEM_LIMIT = 60 * 1024 * 1024

ROW_TILE = 512
FFN_ROWS = 1024
FF_TILE = 512
IN_TILE = 1024
GLA_ROWS = 256
SAMPLE_SLOTS = 3


def _dot(a, b):
    return jnp.dot(a, b, preferred_element_type=F32)


def _dot_nt(a, b):
    return lax.dot_general(a, b, (((1,), (1,)), ((), ())), preferred_element_type=F32)


def _dot_tn(a, b):
    return lax.dot_general(a, b, (((0,), (0,)), ((), ())), preferred_element_type=F32)


def _params(*sem, vmem=VMEM_LIMIT):
    return pltpu.CompilerParams(dimension_semantics=sem, vmem_limit_bytes=vmem)


def _row_tile(n):
    t = ROW_TILE
    while n % t:
        t //= 2
    return t


def _rms_scale(x):
    return lax.rsqrt(jnp.mean(x * x, axis=-1, keepdims=True) + EPS)


CAST_ROWS = 256


def _cast_rows(rows, steps):
    return -(-rows // (16 * steps)) * 16


class _SideCasts:
    def __init__(self, weights, layers, steps):
        self.layers = layers
        self.steps = steps
        self.shapes = [w.shape[1:] for w in weights]
        self.rows = [_cast_rows(s[0], steps) for s in self.shapes]
        assert all(s[0] >= r and s[0] % 16 == 0 for s, r in zip(self.shapes, self.rows))

    def __len__(self):
        return len(self.shapes)

    def out_shapes(self):
        return [jax.ShapeDtypeStruct(s, BF16) for s in self.shapes]

    def scratch_shapes(self):
        n = len(self)
        if n == 0:
            return []
        return ([pltpu.VMEM((2, r, s[1]), F32) for s, r in zip(self.shapes, self.rows)]
                + [pltpu.VMEM((r, s[1]), BF16) for s, r in zip(self.shapes, self.rows)]
                + [pltpu.SemaphoreType.DMA((n, 2)), pltpu.SemaphoreType.DMA((n,))])

    def run(self, step, srcs, dsts, scratch):
        n = len(self)
        if n == 0:
            return
        in_bufs, out_bufs, in_sem, out_sem = scratch[:n], scratch[n:2 * n], scratch[2 * n], scratch[2 * n + 1]

        def start_row(k, t):
            r, total = self.rows[k], self.shapes[k][0]
            return pl.multiple_of(jnp.minimum(t * r, total - r), 16)

        def fetch(k, t):
            return pltpu.make_async_copy(srcs[k].at[self.layers[k], pl.ds(start_row(k, t), self.rows[k]), :],
                                         in_bufs[k].at[t % 2], in_sem.at[k, t % 2])

        def store(k, t):
            return pltpu.make_async_copy(out_bufs[k], dsts[k].at[pl.ds(start_row(k, t), self.rows[k]), :],
                                         out_sem.at[k])

        @pl.when(step == 0)
        def _():
            for k in range(n):
                fetch(k, step).start()

        @pl.when(step + 1 < self.steps)
        def _():
            for k in range(n):
                fetch(k, step + 1).start()

        for k in range(n):
            fetch(k, step).wait()

            @pl.when(step > 0)
            def _():
                store(k, step - 1).wait()

            out_bufs[k][...] = in_bufs[k][step % 2].astype(BF16)
            store(k, step).start()

        @pl.when(step == self.steps - 1)
        def _():
            for k in range(n):
                store(k, step).wait()


GZ_BLK = GZ_OFF // CAST_ROWS
assert GZ_OFF % CAST_ROWS == 0 and GLA_GK_RANK % 16 == 0 and GLA_GK_RANK < CAST_ROWS


def _cast_w_in_body(a_ref, b_ref, main_ref, gz_ref):
    k = pl.program_id(1)

    @pl.when(k < GZ_BLK)
    def _():
        main_ref[0] = a_ref[0].astype(BF16)

    @pl.when(k >= GZ_BLK)
    def _():
        main_ref[0, :CAST_ROWS - GLA_GK_RANK] = a_ref[0, GLA_GK_RANK:].astype(BF16)
        main_ref[0, CAST_ROWS - GLA_GK_RANK:] = b_ref[0, :GLA_GK_RANK].astype(BF16)

    @pl.when(k == GZ_BLK)
    def _():
        gz_ref[0, :GLA_GK_RANK] = a_ref[0, :GLA_GK_RANK].astype(BF16)
        gz_ref[0, GLA_GK_RANK:] = jnp.zeros((LANE - GLA_GK_RANK, a_ref.shape[2]), BF16)


def _cast_w_in(w_in_t):
    depth, c, r = w_in_t.shape
    wide = c - GLA_GK_RANK
    last = pl.cdiv(c, CAST_ROWS) - 1
    return pl.pallas_call(
        _cast_w_in_body,
        grid=(depth, wide // CAST_ROWS),
        in_specs=[pl.BlockSpec((1, CAST_ROWS, r), lambda l, k: (l, k, 0)),
                  pl.BlockSpec((1, CAST_ROWS, r), lambda l, k: (l, jnp.clip(k + 1, GZ_BLK + 1, last), 0))],
        out_specs=[pl.BlockSpec((1, CAST_ROWS, r), lambda l, k: (l, k, 0)),
                   pl.BlockSpec((1, LANE, r), lambda l, k: (l, 0, 0))],
        out_shape=[jax.ShapeDtypeStruct((depth, wide, r), BF16),
                   jax.ShapeDtypeStruct((depth, LANE, r), BF16)],
        compiler_params=_params("parallel", "arbitrary"),
        name="cast_w_in",
    )(w_in_t, w_in_t)


def _ffn_body(*refs, final, n_in, n_out, split):
    x_refs = refs[:n_in]
    nw_ref, wg_ref, wu_ref, wd_ref, fw_ref = refs[n_in:n_in + 5]
    o_refs = refs[n_in + 5:n_in + 5 + n_out]
    scratch = refs[n_in + 5 + n_out:]
    xn_ref = scratch[0]
    i = pl.program_id(0)
    j = pl.program_id(1)
    n_i = pl.num_programs(0)
    tm = xn_ref.shape[0]

    if n_out == 1:
        acc_ref = o_refs[0]
    else:
        acc_ref, o_sem = scratch[1:3]

        def put(t, act):
            @pl.when(t < split)
            def _():
                act(pltpu.make_async_copy(acc_ref, o_refs[0].at[pl.ds(t * tm, tm), :], o_sem.at[0]))

            @pl.when(t >= split)
            def _():
                act(pltpu.make_async_copy(acc_ref, o_refs[1].at[pl.ds((t - split) * tm, tm), :], o_sem.at[0]))

        @pl.when(jnp.logical_and(j == 0, i >= 1))
        def _():
            put(i - 1, lambda c: c.wait())

    if n_in == 2:
        x_buf, x_sem = scratch[-2:]

        def fetch(t, act):
            @pl.when(t < split)
            def _():
                act(pltpu.make_async_copy(x_refs[0].at[pl.ds(t * tm, tm), :], x_buf.at[t % 2], x_sem.at[t % 2]))

            @pl.when(t >= split)
            def _():
                act(pltpu.make_async_copy(x_refs[1].at[pl.ds((t - split) * tm, tm), :], x_buf.at[t % 2],
                                          x_sem.at[t % 2]))

        @pl.when(j == 0)
        def _():
            @pl.when(i == 0)
            def _():
                fetch(i, lambda c: c.start())

            @pl.when(i + 1 < n_i)
            def _():
                fetch(i + 1, lambda c: c.start())

            fetch(i, lambda c: c.wait())

    def load_x():
        return x_refs[0][...] if n_in == 1 else x_buf[i % 2]

    @pl.when(j == 0)
    def _():
        x = load_x()
        xn_ref[...] = ((x * _rms_scale(x)) * nw_ref[...]).astype(BF16)
        acc_ref[...] = jnp.zeros_like(acc_ref)

    xn = xn_ref[...]
    h = _dot(xn, wg_ref[...])
    u = _dot(xn, wu_ref[...])
    a = (h * jax.nn.sigmoid(h) * u).astype(BF16)
    acc_ref[...] += _dot(a, wd_ref[...])

    @pl.when(j == pl.num_programs(1) - 1)
    def _():
        y = load_x() + 0.5 * acc_ref[...]
        if final:
            y = (y * _rms_scale(y)) * fw_ref[...]
        acc_ref[...] = y
        if n_out == 2:
            put(i, lambda c: c.start())

            @pl.when(i == n_i - 1)
            def _():
                put(i, lambda c: c.wait())


def _ffn(xs, nw, wg, wu, wd, fw, final, out_rows):
    d = xs[0].shape[1]
    n = sum(x.shape[0] for x in xs)
    assert n == sum(out_rows)
    f = wg.shape[1]
    tm = FFN_ROWS if n % FFN_ROWS == 0 else _row_tile(n)
    first = xs[0].shape[0] if len(xs) == 2 else out_rows[0]
    if first % tm:
        tm = _row_tile(n)
    assert first % tm == 0
    split = first // tm
    anyspace = pl.BlockSpec(memory_space=pl.ANY)
    tile = pl.BlockSpec((tm, d), lambda i, j: (i, 0))
    outs = pl.pallas_call(
        functools.partial(_ffn_body, final=final, n_in=len(xs), n_out=len(out_rows), split=split),
        grid=(n // tm, f // FF_TILE),
        in_specs=([anyspace] * 2 if len(xs) == 2 else [tile]) + [
            pl.BlockSpec((1, d), lambda i, j: (0, 0)),
            pl.BlockSpec((d, FF_TILE), lambda i, j: (0, j)),
            pl.BlockSpec((d, FF_TILE), lambda i, j: (0, j)),
            pl.BlockSpec((FF_TILE, d), lambda i, j: (j, 0)),
            pl.BlockSpec((1, d), lambda i, j: (0, 0)),
        ],
        out_specs=[anyspace] * 2 if len(out_rows) == 2 else [tile],
        out_shape=[jax.ShapeDtypeStruct((r, d), F32) for r in out_rows],
        scratch_shapes=([pltpu.VMEM((tm, d), BF16)]
                        + ([pltpu.VMEM((tm, d), F32), pltpu.SemaphoreType.DMA((1,))] if len(out_rows) == 2 else [])
                        + ([pltpu.VMEM((2, tm, d), F32), pltpu.SemaphoreType.DMA((2,))] if len(xs) == 2 else [])),
        compiler_params=_params("parallel" if len(xs) == len(out_rows) == 1 else "arbitrary", "arbitrary",
                                vmem=FFN_VMEM_LIMIT),
        name="ffn",
    )(*xs, nw, wg, wu, wd, fw)
    return outs


N_MAIN_BLK = MAIN_W // IN_TILE
N_K_BLK = DIFF_QK_W // IN_TILE
N_V_BLK = DIFF_V_W // IN_TILE


assert N_K_BLK == 1 and N_V_BLK == 1


def _inproj_body(x_ref, nw_ref, w_ref, wgz_ref, wgk_ref, bgk_ref, *rest, layer, split, seq_s):
    main_ref, kvb_ref, vt_ref, g_ref, pk_ref, pv_ref, sk_ref, sv_ref, xn_ref, y_ref, sem = rest[-11:]
    i = pl.program_id(0)
    j = pl.program_id(1)
    n_i = pl.num_programs(0)
    tm = x_ref.shape[0]

    @pl.when(j == 0)
    def _():
        x = x_ref[...]
        xn = ((x * _rms_scale(x)) * nw_ref[...]).astype(BF16)
        xn_ref[...] = xn
        gz = _dot_nt(xn, wgz_ref[...])
        z = _dot(gz.astype(BF16), wgk_ref[...]) + bgk_ref[...]
        g_ref[...] = (jnp.minimum(z, 0.0) - jnp.log1p(jnp.exp(-jnp.abs(z)))) * (1.0 / GLA_GATE_NORMALIZER)

    y = _dot_nt(xn_ref[...], w_ref[...])

    @pl.when(j < N_MAIN_BLK)
    def _():
        main_ref[...] = y

    def row_copies(slot, tile, dst_p, dst_s, act):
        heads = [slice(h * DIFF_DV, (h + 1) * DIFF_DV) for h in range(DIFF_HEADS)]

        @pl.when(tile < split)
        def _():
            for h, cols in enumerate(heads):
                act(pltpu.make_async_copy(y_ref.at[slot, :, cols],
                                          dst_p.at[layer, 0, pl.ds(tile * tm, tm), h, :], sem.at[slot]))

        @pl.when(tile >= split)
        def _():
            for b in range(tm // seq_s):
                for h, cols in enumerate(heads):
                    act(pltpu.make_async_copy(y_ref.at[slot, b * seq_s:(b + 1) * seq_s, cols],
                                              dst_s.at[layer, (tile - split) * (tm // seq_s) + b, :, h, :],
                                              sem.at[slot]))

    def kv_step(slot, dst_p, dst_s):
        @pl.when(i > 0)
        def _():
            row_copies(slot, i - 1, dst_p, dst_s, lambda c: c.wait())

        y_ref[slot] = y
        kvb_ref[...] = y.astype(BF16)
        if slot == 1:
            vt_ref[0] = y.T.astype(BF16)
        row_copies(slot, i, dst_p, dst_s, lambda c: c.start())

        @pl.when(i == n_i - 1)
        def _():
            row_copies(slot, i, dst_p, dst_s, lambda c: c.wait())

    @pl.when(j == N_MAIN_BLK)
    def _():
        kv_step(0, pk_ref, sk_ref)

    @pl.when(j == N_MAIN_BLK + 1)
    def _():
        kv_step(1, pv_ref, sv_ref)


def _inproj(x, nw, w_main, w_gz, w_gk, b_gk, kv_out, *, layer, split, seq_s):
    n, d = x.shape
    tm = _row_tile(n)
    assert tm % seq_s == 0 and kv_out[0].shape[1] == 1
    nj = N_MAIN_BLK + 2
    k0 = N_MAIN_BLK
    anyspace = pl.BlockSpec(memory_space=pl.ANY)
    outs = pl.pallas_call(
        functools.partial(_inproj_body, layer=layer, split=split, seq_s=seq_s),
        grid=(n // tm, nj),
        in_specs=[
            pl.BlockSpec((tm, d), lambda i, j: (i, 0)),
            pl.BlockSpec((1, d), lambda i, j: (0, 0)),
            pl.BlockSpec((None, IN_TILE, d), lambda i, j: (layer, j, 0)),
            pl.BlockSpec((None, LANE, d), lambda i, j: (layer, 0, 0)),
            pl.BlockSpec((LANE, GLA_K_W), lambda i, j: (0, 0)),
            pl.BlockSpec((1, GLA_K_W), lambda i, j: (0, 0)),
        ] + [anyspace] * len(kv_out),
        out_specs=[
            pl.BlockSpec((tm, IN_TILE), lambda i, j: (i, jnp.minimum(j, k0 - 1))),
            pl.BlockSpec((tm, IN_TILE), lambda i, j: (i, jnp.clip(j - k0, 0, 1))),
            pl.BlockSpec((1, IN_TILE, tm), lambda i, j: (i, 0, 0)),
            pl.BlockSpec((tm, GLA_K_W), lambda i, j: (i, 0)),
            anyspace, anyspace, anyspace, anyspace,
        ],
        out_shape=[
            jax.ShapeDtypeStruct((n, MAIN_W), F32),
            jax.ShapeDtypeStruct((n, DIFF_QK_W + DIFF_V_W), BF16),
            jax.ShapeDtypeStruct((n // tm, DIFF_V_W, tm), BF16),
            jax.ShapeDtypeStruct((n, GLA_K_W), F32),
        ] + [jax.ShapeDtypeStruct(a.shape, a.dtype) for a in kv_out],
        scratch_shapes=[pltpu.VMEM((tm, d), BF16), pltpu.VMEM((2, tm, IN_TILE), F32),
                        pltpu.SemaphoreType.DMA((2,))],
        input_output_aliases={6 + k: 4 + k for k in range(len(kv_out))},
        compiler_params=_params("arbitrary", "arbitrary"),
        name="inproj",
    )(x, nw, w_main, w_gz, w_gk, b_gk, *kv_out)
    return outs[:4], outs[4:]


def _split3(a):
    hi = a.astype(BF16)
    r = a - hi.astype(F32)
    mid = r.astype(BF16)
    lo = (r - mid.astype(F32)).astype(BF16)
    return hi, mid, lo


def _gla_body(q_ref, k_ref, v_ref, r_ref, g_ref, s0_ref, nw_ref, mix_in_ref, o_ref, s_ref, st_ref, *, chunk):
    del mix_in_ref
    t = pl.program_id(1)
    rows = q_ref.shape[0]
    shift = chunk.bit_length() - 1
    assert chunk == 1 << shift

    @pl.when(t == 0)
    def _():
        for h in range(GLA_HEADS):
            st_ref[h] = s0_ref[0, h].T

    ri = lax.broadcasted_iota(jnp.int32, (rows, rows), 0)
    ci = lax.broadcasted_iota(jnp.int32, (rows, rows), 1)
    causal = jnp.logical_and((ri >> shift) == (ci >> shift), ci <= ri)
    tri = causal.astype(BF16)

    g_hi, g_mid, g_lo = _split3(g_ref[...])
    b_all = _dot(tri, g_hi) + _dot(tri, g_mid) + _dot(tri, g_lo)

    for h in range(GLA_HEADS):
        kc = slice(h * GLA_DK, (h + 1) * GLA_DK)
        vc = slice(h * GLA_DV, (h + 1) * GLA_DV)
        b = b_all[:, kc]
        k = k_ref[:, kc]
        v = v_ref[:, vc].astype(BF16)
        qe = (q_ref[:, kc] * (GLA_DK ** -0.5) * jnp.exp(b)).astype(BF16)
        ke = (k * jnp.exp(-b)).astype(BF16)
        a = jnp.where(causal, _dot_nt(qe, ke), 0.0).astype(BF16)
        o_in = _dot(a, v)
        for c in range(rows // chunk):
            sl = slice(c * chunk, (c + 1) * chunk)
            b_last = b[(c + 1) * chunk - 1:(c + 1) * chunk, :]
            kd = (k[sl] * jnp.exp(b_last - b[sl])).astype(BF16)
            st = st_ref[h]
            o = _dot_nt(qe[sl], st.astype(BF16)) + o_in[sl]
            st_ref[h] = st * jnp.exp(b_last) + _dot_tn(v[sl], kd)
            o = (o * _rms_scale(o)) * nw_ref[...]
            r = r_ref[sl, vc]
            o_ref[sl, vc] = (o * (r * jax.nn.sigmoid(r))).astype(BF16)

    @pl.when(t == pl.num_programs(1) - 1)
    def _():
        for h in range(GLA_HEADS):
            s_ref[0, h] = st_ref[h].T


def _gla(main, g, s0, nw, mix, *, layer, row_off, batch, seq, chunk, rows):
    assert seq % rows == 0 and rows % chunk == 0 and row_off % rows == 0
    nt = seq // rows
    rb0 = row_off // rows

    def rowblk(b, t):
        return rb0 + b * nt + t

    state_spec = pl.BlockSpec((1, GLA_HEADS, GLA_DK, GLA_DV), lambda b, t: (b, 0, 0, 0))
    state_in = pl.BlockSpec((None, 1, GLA_HEADS, GLA_DK, GLA_DV), lambda b, t: (layer, b, 0, 0, 0))
    out, s = pl.pallas_call(
        functools.partial(_gla_body, chunk=chunk),
        grid=(batch, nt),
        in_specs=[
            pl.BlockSpec((rows, GLA_K_W), lambda b, t: (rowblk(b, t), 0)),
            pl.BlockSpec((rows, GLA_K_W), lambda b, t: (rowblk(b, t), 1)),
            pl.BlockSpec((rows, GLA_V_W), lambda b, t: (rowblk(b, t), 2 * GLA_K_W // GLA_V_W)),
            pl.BlockSpec((rows, GLA_V_W), lambda b, t: (rowblk(b, t), 2 * GLA_K_W // GLA_V_W + 1)),
            pl.BlockSpec((rows, GLA_K_W), lambda b, t: (rowblk(b, t), 0)),
            state_in,
            pl.BlockSpec((1, GLA_DV), lambda b, t: (0, 0)),
            pl.BlockSpec(memory_space=pl.ANY),
        ],
        out_specs=[
            pl.BlockSpec((rows, GLA_V_W), lambda b, t: (rowblk(b, t), 0)),
            state_spec,
        ],
        out_shape=[
            jax.ShapeDtypeStruct(mix.shape, mix.dtype),
            jax.ShapeDtypeStruct((batch, GLA_HEADS, GLA_DK, GLA_DV), F32),
        ],
        scratch_shapes=[pltpu.VMEM((GLA_HEADS, GLA_DV, GLA_DK), F32)],
        input_output_aliases={7: 0},
        compiler_params=_params("parallel", "arbitrary"),
        name="gla",
    )(main, main, main, main, g, s0, nw, mix)
    return out, s


def _bucket_thresholds():
    half = N_BUCKETS // 2
    m = half // 2
    e = half - m
    thr = []
    for kk in range(1, e):
        n = m
        while n ** e * m ** kk < m ** e * MAX_DISTANCE ** kk:
            n += 1
        thr.append(n)
    return tuple(thr)


_BUCKET_THR = _bucket_thresholds()


def _t5_bucket(rel):
    half = N_BUCKETS // 2
    max_exact = half // 2
    n = jnp.abs(rel)
    large = jnp.full(rel.shape, max_exact, jnp.int32)
    for thr in _BUCKET_THR:
        large = large + (n >= thr).astype(jnp.int32)
    return jnp.where(rel > 0, half, 0) + jnp.where(n < max_exact, n, large)


def _bias_body(rb_ref, *rest, q_off, k_off, k_step, key_major, scale, casts):
    n = len(casts)
    o_ref = rest[n]
    h = pl.program_id(0)
    t = pl.program_id(1)
    casts.run(h * pl.num_programs(1) + t, rest[:n], rest[n + 1:2 * n + 1], rest[2 * n + 1:])
    shape = o_ref.shape[2:]
    qpos = q_off + lax.broadcasted_iota(jnp.int32, shape, 1 if key_major else 0)
    kpos = k_off + t * k_step + lax.broadcasted_iota(jnp.int32, shape, 0 if key_major else 1)
    bucket = _t5_bucket(kpos - qpos)
    acc = jnp.zeros(shape, F32)
    for bkt in range(N_BUCKETS):
        acc = jnp.where(bucket == bkt, rb_ref[bkt, h], acc)
    visible = (kpos >> 6) <= (qpos >> 6)
    o_ref[0, 0] = jnp.where(visible, acc * scale, -jnp.inf)


assert CHUNK == 64


def _bias_table(rel_bias, cast_weights=(), cast_layers=(), *, tiles, rows, cols, q_off, k_off, k_step,
                key_major=False, scale=1.0):
    casts = _SideCasts(cast_weights, cast_layers, DIFF_HEADS * tiles)
    anyspace = pl.BlockSpec(memory_space=pl.ANY)
    outs = pl.pallas_call(
        functools.partial(_bias_body, q_off=q_off, k_off=k_off, k_step=k_step, key_major=key_major,
                          scale=scale, casts=casts),
        grid=(DIFF_HEADS, tiles),
        in_specs=[pl.BlockSpec(memory_space=pltpu.SMEM)] + [anyspace] * len(casts),
        out_specs=[pl.BlockSpec((1, 1, rows, cols), lambda h, t: (h, t, 0, 0))] + [anyspace] * len(casts),
        out_shape=[jax.ShapeDtypeStruct((DIFF_HEADS, tiles, rows, cols), F32)] + casts.out_shapes(),
        scratch_shapes=casts.scratch_shapes(),
        compiler_params=_params("arbitrary", "arbitrary"),
        name="bias_table",
    )(rel_bias, *cast_weights)
    return outs[0], outs[1:]


def _lambda(lam_ref, layer):
    lam = lam_ref[...]
    a = jnp.sum(lam[0:1, :] * lam[1:2, :], axis=-1, keepdims=True)
    b = jnp.sum(lam[2:3, :] * lam[3:4, :], axis=-1, keepdims=True)
    lam_init = 0.8 - 0.6 * math.exp(-0.3 * layer)
    return jnp.exp(a) - jnp.exp(b) + lam_init, lam_init


def _diff_finish(o1, l1, o2, l2, lam, lam_init, nw):
    o = o1 / l1 - lam * (o2 / l2)
    return ((o * _rms_scale(o)) * nw * (1.0 - lam_init)).astype(BF16)


def _attn_prompt_body(q_ref, k_ref, vt_ref, bias_ref, lam_ref, nw_ref, mix_in_ref, *rest, layer, blk, casts):
    del mix_in_ref
    n = len(casts)
    cast_srcs, o_ref, cast_dsts = rest[:n], rest[n], rest[n + 1:2 * n + 1]
    acc_ref, s_ref, p_ref = rest[2 * n + 1:2 * n + 4]
    casts.run(pl.program_id(0) * pl.num_programs(1) + pl.program_id(1), cast_srcs, cast_dsts, rest[2 * n + 4:])
    qi = pl.program_id(1)
    qt = (q_ref[...] * (DIFF_D ** -0.5 * LOG2E)).T.astype(BF16)
    qth = (qt[:DIFF_D], qt[DIFF_D:])
    acc_ref[...] = jnp.zeros_like(acc_ref)
    strips = [slice(c * SUB, (c + 1) * SUB) for c in range(blk // SUB)]

    def fold(a):
        return a.reshape(SUB // 8, 8, blk)

    p_ref[...] = jnp.zeros_like(p_ref)

    def step(kj, stats):
        kb = k_ref[pl.ds(pl.multiple_of(kj * blk, blk), blk), :]
        vt_prev = vt_ref[jnp.maximum(kj - 1, 0)]
        t = jnp.minimum(qi - kj, 2)
        for half in range(2):
            s_ref[half] = _dot(kb[:, half * DIFF_D:(half + 1) * DIFF_D], qth[half]) + bias_ref[0, t]
        for half in range(2):
            acc_ref[half] += _dot(vt_prev, p_ref[half])
        out = []
        for half in range(2):
            m, l = stats[half]
            bm = jnp.max(fold(s_ref[half, strips[0], :]), axis=0)
            for c in strips[1:]:
                bm = jnp.maximum(bm, jnp.max(fold(s_ref[half, c, :]), axis=0))
            m_new = jnp.maximum(m, jnp.max(bm, axis=0, keepdims=True))
            alpha = jnp.exp2(m - m_new)
            acc_ref[half] = alpha * acc_ref[half]
            ls = jnp.zeros((8, blk), F32)
            for c in strips:
                p = jnp.exp2(s_ref[half, c, :] - m_new)
                ls = ls + jnp.sum(fold(p), axis=0)
                p_ref[half, c, :] = p.astype(BF16)
            l = alpha * l + jnp.sum(ls, axis=0, keepdims=True)
            out.append((m_new, l))
        return tuple(out)

    init = tuple((jnp.full((1, blk), -jnp.inf, F32), jnp.zeros((1, blk), F32)) for _ in range(2))
    stats = lax.fori_loop(0, qi + 1, step, init)

    lam, lam_init = _lambda(lam_ref, layer)
    vt_last = vt_ref[qi]
    o1 = acc_ref[0] + _dot(vt_last, p_ref[0])
    o2 = acc_ref[1] + _dot(vt_last, p_ref[1])
    ot = o1 / stats[0][1] - lam * (o2 / stats[1][1])
    scale = lax.rsqrt(jnp.mean(ot * ot, axis=0, keepdims=True) + EPS)
    ot = (ot * scale) * nw_ref[...] * (1.0 - lam_init)
    o_ref[...] = ot.T.astype(BF16)


def _attn_prompt(main, kvb, vt, bias, lam, nw_col, mix, cast_weights=(), cast_layers=(), *, layer, seq):
    blk = vt.shape[2]
    assert seq % blk == 0 and blk >= MAX_DISTANCE and blk % CHUNK == 0
    q_col0 = DQ_OFF // (2 * DIFF_D)
    o_col0 = GLA_V_W // DIFF_DV
    casts = _SideCasts(cast_weights, cast_layers, DIFF_HEADS * (seq // blk))
    anyspace = pl.BlockSpec(memory_space=pl.ANY)
    outs = pl.pallas_call(
        functools.partial(_attn_prompt_body, layer=layer, blk=blk, casts=casts),
        grid=(DIFF_HEADS, seq // blk),
        in_specs=[
            pl.BlockSpec((blk, 2 * DIFF_D), lambda h, i: (i, q_col0 + h)),
            pl.BlockSpec((seq, 2 * DIFF_D), lambda h, i: (0, h)),
            pl.BlockSpec((seq // blk, DIFF_DV, blk), lambda h, i: (0, h, 0)),
            pl.BlockSpec((1, 3, blk, blk), lambda h, i: (h, 0, 0, 0)),
            pl.BlockSpec((4, DIFF_D), lambda h, i: (0, 0)),
            pl.BlockSpec((DIFF_DV, 1), lambda h, i: (0, 0)),
            anyspace,
        ] + [anyspace] * len(casts),
        out_specs=[pl.BlockSpec((blk, DIFF_DV), lambda h, i: (i, o_col0 + h))] + [anyspace] * len(casts),
        out_shape=[jax.ShapeDtypeStruct(mix.shape, mix.dtype)] + casts.out_shapes(),
        scratch_shapes=[pltpu.VMEM((2, DIFF_DV, blk), F32), pltpu.VMEM((2, blk, blk), F32),
                        pltpu.VMEM((2, blk, blk), BF16)] + casts.scratch_shapes(),
        input_output_aliases={6: 0},
        compiler_params=_params("arbitrary", "arbitrary"),
        name="attn_prompt",
    )(main, kvb, vt, bias, lam, nw_col, mix, *cast_weights)
    return outs[0], outs[1:]


def _attn_sample_body(q_ref, kn_ref, vn_ref, kc_hbm, vc_hbm, bias_ref, lam_ref, nw_ref, mix_in_ref,
                      o_ref, kbuf, vbuf, sem, *, layer, past):
    del mix_in_ref
    b = pl.program_id(0)
    h = pl.program_id(1)
    n_h = DIFF_HEADS
    step = b * n_h + h
    n_steps = pl.num_programs(0) * n_h
    slot = step % SAMPLE_SLOTS

    def copies(cstep):
        cb, ch, cslot = cstep // n_h, cstep % n_h, cstep % SAMPLE_SLOTS
        return (pltpu.make_async_copy(kc_hbm.at[layer, cb, :, ch, :], kbuf.at[cslot], sem.at[0, cslot]),
                pltpu.make_async_copy(vc_hbm.at[layer, cb, :, ch, :], vbuf.at[cslot], sem.at[1, cslot]))

    def start(cstep):
        @pl.when(cstep < n_steps)
        def _():
            for c in copies(cstep):
                c.start()

    @pl.when(step == 0)
    def _():
        for ahead in range(SAMPLE_SLOTS - 1):
            start(step + ahead)

    start(step + SAMPLE_SLOTS - 1)

    qs = (q_ref[...] * (DIFF_D ** -0.5)).astype(BF16)
    kn = kn_ref[...]
    bias_c = bias_ref[0, 0, :, :past]
    bias_n = bias_ref[0, 0, :, past:]
    for c in copies(step):
        c.wait()
    ps = []
    for half in range(2):
        cols = slice(half * DIFF_D, (half + 1) * DIFF_D)
        sc = _dot_nt(qs[:, cols], kbuf[slot, :, cols].astype(BF16)) + bias_c
        sn = _dot_nt(qs[:, cols], kn[:, cols]) + bias_n
        m = jnp.maximum(jnp.max(sc, axis=-1, keepdims=True), jnp.max(sn, axis=-1, keepdims=True))
        pc = jnp.exp(sc - m)
        pn = jnp.exp(sn - m)
        inv = 1.0 / (jnp.sum(pc, axis=-1, keepdims=True) + jnp.sum(pn, axis=-1, keepdims=True))
        ps.append((pc * inv, pn * inv))
    lam, lam_init = _lambda(lam_ref, layer)
    wc = (ps[0][0] - lam * ps[1][0]).astype(BF16)
    wn = (ps[0][1] - lam * ps[1][1]).astype(BF16)
    o = _dot(wc, vbuf[slot].astype(BF16)) + _dot(wn, vn_ref[...])
    o_ref[...] = ((o * _rms_scale(o)) * nw_ref[...] * (1.0 - lam_init)).astype(BF16)


def _attn_sample(main, kvb, cache_k, cache_v, bias, lam, nw, mix, *, layer, row_off, batch, seq):
    past = cache_k.shape[2]
    assert row_off % seq == 0
    rb0 = row_off // seq
    q_col0 = DQ_OFF // (2 * DIFF_D)
    v_col0 = DIFF_QK_W // DIFF_DV
    o_col0 = GLA_V_W // DIFF_DV
    return pl.pallas_call(
        functools.partial(_attn_sample_body, layer=layer, past=past),
        grid=(batch, DIFF_HEADS),
        in_specs=[
            pl.BlockSpec((seq, 2 * DIFF_D), lambda b, h: (rb0 + b, q_col0 + h)),
            pl.BlockSpec((seq, 2 * DIFF_D), lambda b, h: (rb0 + b, h)),
            pl.BlockSpec((seq, DIFF_DV), lambda b, h: (rb0 + b, v_col0 + h)),
            pl.BlockSpec(memory_space=pl.ANY),
            pl.BlockSpec(memory_space=pl.ANY),
            pl.BlockSpec((1, 1, seq, past + seq), lambda b, h: (h, 0, 0, 0)),
            pl.BlockSpec((4, DIFF_D), lambda b, h: (0, 0)),
            pl.BlockSpec((1, DIFF_DV), lambda b, h: (0, 0)),
            pl.BlockSpec(memory_space=pl.ANY),
        ],
        out_specs=pl.BlockSpec((seq, DIFF_DV), lambda b, h: (rb0 + b, o_col0 + h)),
        out_shape=jax.ShapeDtypeStruct(mix.shape, mix.dtype),
        scratch_shapes=[
            pltpu.VMEM((SAMPLE_SLOTS, past, 2 * DIFF_D), F32),
            pltpu.VMEM((SAMPLE_SLOTS, past, DIFF_DV), F32),
            pltpu.SemaphoreType.DMA((2, SAMPLE_SLOTS)),
        ],
        input_output_aliases={8: 0},
        compiler_params=_params("arbitrary", "arbitrary"),
        name="attn_sample",
    )(main, kvb, kvb, cache_k, cache_v, bias, lam, nw, mix)


def _outproj_body(o_ref, w_ref, x_ref, y_ref):
    y_ref[...] = x_ref[...] + _dot(o_ref[...], w_ref[...])


def _outproj(mix, w, x):
    n, d = x.shape
    tm = _row_tile(n)
    return pl.pallas_call(
        _outproj_body,
        grid=(n // tm,),
        in_specs=[
            pl.BlockSpec((tm, MIX_WIDTH), lambda i: (i, 0)),
            pl.BlockSpec((MIX_WIDTH, d), lambda i: (0, 0)),
            pl.BlockSpec((tm, d), lambda i: (i, 0)),
        ],
        out_specs=pl.BlockSpec((tm, d), lambda i: (i, 0)),
        out_shape=jax.ShapeDtypeStruct((n, d), F32),
        compiler_params=_params("parallel"),
        name="outproj",
    )(mix, w, x)


def kernel(x_prompt, x_sample, cache_k, cache_v, state_gla, ffn1_norm, ffn1_w_gate, ffn1_w_up, ffn1_w_down,
           mix_norm, w_in, gla_w_gk, gla_b_gk, gla_norm, diff_lambda, diff_norm, w_out,
           ffn2_norm, ffn2_w_gate, ffn2_w_up, ffn2_w_down, rel_bias, final_norm):
    pb, ps, d = x_prompt.shape
    sb, ss, _ = x_sample.shape
    depth = w_in.shape[0]
    past = cache_k.shape[2]
    assert pb == 1
    n_p, n_s = pb * ps, sb * ss
    xs = [x_prompt.reshape(n_p, d), x_sample.reshape(n_s, d)]

    blk = _row_tile(n_p + n_s)
    ffn1_w = (ffn1_w_gate, ffn1_w_up, ffn1_w_down)
    ffn2_w = (ffn2_w_gate, ffn2_w_up, ffn2_w_down)
    bias_p, first = _bias_table(rel_bias, ffn1_w, [0] * len(ffn1_w), tiles=3, rows=blk, cols=blk, q_off=0, k_off=0,
                                k_step=-blk, key_major=True, scale=LOG2E)
    bias_s, _ = _bias_table(rel_bias, tiles=1, rows=ss, cols=past + ss, q_off=past, k_off=0, k_step=0)
    zero_state = jnp.zeros((1, pb, GLA_HEADS, GLA_DK, GLA_DV), F32)
    row = lambda a: a.reshape(1, -1)

    w_main, w_gz = _cast_w_in(jnp.swapaxes(w_in, 1, 2))
    f1 = {0: list(first)}
    later = [(w, l) for l in range(depth) for w in ((ffn1_w if l else ()) + ffn2_w + (w_out,))]
    f2, w_o = {}, {}

    kv_out = [jnp.zeros((depth, pb, ps, DIFF_HEADS, DIFF_DV), F32) for _ in range(2)]
    kv_out += [jnp.zeros((depth, sb, ss, DIFF_HEADS, DIFF_DV), F32) for _ in range(2)]
    p_states, s_states = [], []
    mix = jnp.zeros((n_p + n_s, MIX_WIDTH), BF16)
    for l in range(depth):
        w_gk = jnp.pad(gla_w_gk[l], ((0, LANE - GLA_GK_RANK), (0, 0))).astype(BF16)

        x, = _ffn(xs, row(ffn1_norm[l]), *f1[l], row(final_norm), False, [n_p + n_s])
        (main, kvb, vt, g), kv_out = _inproj(x, row(mix_norm[l]), w_main, w_gz, w_gk, row(gla_b_gk[l]), kv_out,
                                             layer=l, split=n_p // blk, seq_s=ss)

        mix, sp = _gla(main, g, zero_state, row(gla_norm[l]), mix,
                       layer=0, row_off=0, batch=pb, seq=ps, chunk=CHUNK, rows=min(GLA_ROWS, ps))
        mix, s_s = _gla(main, g, state_gla, row(gla_norm[l]), mix,
                        layer=l, row_off=n_p, batch=sb, seq=ss, chunk=ss, rows=ss)
        mix, cast = _attn_prompt(main, kvb, vt, bias_p, diff_lambda[l], diff_norm[l].reshape(-1, 1), mix,
                                 [w for w, _ in later] if l == 0 else [], [k for _, k in later] if l == 0 else [],
                                 layer=l, seq=ps)
        cast = list(cast)
        for k in range(depth if l == 0 else 0):
            if k:
                f1[k] = [cast.pop(0) for _ in ffn1_w]
            f2[k] = [cast.pop(0) for _ in ffn2_w]
            w_o[k] = cast.pop(0)
        mix = _attn_sample(main, kvb, cache_k, cache_v, bias_s, diff_lambda[l], row(diff_norm[l]), mix,
                           layer=l, row_off=n_p, batch=sb, seq=ss)

        x = _outproj(mix, w_o[l], x)
        last = l == depth - 1
        xs = _ffn([x], row(ffn2_norm[l]), *f2[l], row(final_norm), last, [n_p, n_s] if last else [n_p + n_s])
        p_states.append(sp)
        s_states.append(s_s)

    prompt_k, prompt_v, sample_k, sample_v = kv_out
    return (xs[0].reshape(pb, ps, d), xs[1].reshape(sb, ss, d), prompt_k, prompt_v, jnp.stack(p_states),
            sample_k, sample_v, jnp.stack(s_states))
```

```python
import functools
import math

import jax
import jax.numpy as jnp
from jax import lax
from jax.experimental import pallas as pl
from jax.experimental.pallas import tpu as pltpu

F32 = jnp.float32
BF16 = jnp.bfloat16

EPS = 1e-6
CHUNK = 64
GLA_HEADS = 4
GLA_DK = 128
GLA_DV = 256
GLA_GK_RANK = 16
GLA_GATE_NORMALIZER = 16.0
DIFF_HEADS = 4
DIFF_D = 128
DIFF_DV = 2 * DIFF_D
N_BUCKETS = 32
MAX_DISTANCE = 128

GLA_K_W = GLA_HEADS * GLA_DK
GLA_V_W = GLA_HEADS * GLA_DV
DIFF_QK_W = DIFF_HEADS * 2 * DIFF_D
DIFF_V_W = DIFF_HEADS * DIFF_DV
MIX_WIDTH = GLA_V_W + DIFF_V_W
GZ_OFF = 2 * GLA_K_W + 2 * GLA_V_W
MAIN_W = GZ_OFF + DIFF_QK_W
DQ_OFF = GZ_OFF

LANE = 128
BF16_ROWS = 16
LOG2E = math.log2(math.e)
SUB = 64
VMEM_LIMIT = 56 * 1024 * 1024
FFN_VMEM_LIMIT = 60 * 1024 * 1024

ROW_TILE = 512
FFN_ROWS = 1024
FF_TILE = 512
IN_TILE = 1024
GLA_ROWS = 512
SAMPLE_SLOTS = 3


def _dot(a, b):
    return jnp.dot(a, b, preferred_element_type=F32)


def _dot_nt(a, b):
    return lax.dot_general(a, b, (((1,), (1,)), ((), ())), preferred_element_type=F32)


def _dot_tn(a, b):
    return lax.dot_general(a, b, (((0,), (0,)), ((), ())), preferred_element_type=F32)


def _params(*sem, vmem=VMEM_LIMIT):
    return pltpu.CompilerParams(dimension_semantics=sem, vmem_limit_bytes=vmem)


def _row_tile(n):
    t = ROW_TILE
    while n % t:
        t //= 2
    return t


def _rms_scale(x):
    return lax.rsqrt(jnp.mean(x * x, axis=-1, keepdims=True) + EPS)


CAST_ROWS = 256


def _cast_rows(rows, steps):
    return -(-rows // (BF16_ROWS * steps)) * BF16_ROWS


class _SideCasts:
    def __init__(self, weights, layers, steps):
        self.layers = layers
        self.steps = steps
        self.shapes = [w.shape[1:] for w in weights]
        self.rows = [_cast_rows(s[0], steps) for s in self.shapes]
        assert all(s[0] >= r and s[0] % BF16_ROWS == 0 for s, r in zip(self.shapes, self.rows))

    def __len__(self):
        return len(self.shapes)

    def out_shapes(self):
        return [jax.ShapeDtypeStruct(s, BF16) for s in self.shapes]

    def scratch_shapes(self):
        n = len(self)
        if n == 0:
            return []
        return ([pltpu.VMEM((2, r, s[1]), F32) for s, r in zip(self.shapes, self.rows)]
                + [pltpu.VMEM((r, s[1]), BF16) for s, r in zip(self.shapes, self.rows)]
                + [pltpu.SemaphoreType.DMA((n, 2)), pltpu.SemaphoreType.DMA((n,))])

    def run(self, step, srcs, dsts, scratch):
        n = len(self)
        if n == 0:
            return
        in_bufs, out_bufs, in_sem, out_sem = scratch[:n], scratch[n:2 * n], scratch[2 * n], scratch[2 * n + 1]

        def start_row(k, t):
            r, total = self.rows[k], self.shapes[k][0]
            return pl.multiple_of(jnp.minimum(t * r, total - r), BF16_ROWS)

        def fetch(k, t):
            return pltpu.make_async_copy(srcs[k].at[self.layers[k], pl.ds(start_row(k, t), self.rows[k]), :],
                                         in_bufs[k].at[t % 2], in_sem.at[k, t % 2])

        def store(k, t):
            return pltpu.make_async_copy(out_bufs[k], dsts[k].at[pl.ds(start_row(k, t), self.rows[k]), :],
                                         out_sem.at[k])

        @pl.when(step == 0)
        def _():
            for k in range(n):
                fetch(k, step).start()

        @pl.when(step + 1 < self.steps)
        def _():
            for k in range(n):
                fetch(k, step + 1).start()

        for k in range(n):
            fetch(k, step).wait()

            @pl.when(step > 0)
            def _():
                store(k, step - 1).wait()

            out_bufs[k][...] = in_bufs[k][step % 2].astype(BF16)
            store(k, step).start()

        @pl.when(step == self.steps - 1)
        def _():
            for k in range(n):
                store(k, step).wait()


GZ_BLK = GZ_OFF // CAST_ROWS
assert GZ_OFF % CAST_ROWS == 0 and GLA_GK_RANK % BF16_ROWS == 0 and GLA_GK_RANK < CAST_ROWS


def _cast_w_in_body(a_ref, b_ref, main_ref, gz_ref):
    k = pl.program_id(1)

    @pl.when(k < GZ_BLK)
    def _():
        main_ref[0] = a_ref[0].astype(BF16)

    @pl.when(k >= GZ_BLK)
    def _():
        main_ref[0, :CAST_ROWS - GLA_GK_RANK] = a_ref[0, GLA_GK_RANK:].astype(BF16)
        main_ref[0, CAST_ROWS - GLA_GK_RANK:] = b_ref[0, :GLA_GK_RANK].astype(BF16)

    @pl.when(k == GZ_BLK)
    def _():
        gz_ref[0, :GLA_GK_RANK] = a_ref[0, :GLA_GK_RANK].astype(BF16)
        gz_ref[0, GLA_GK_RANK:] = jnp.zeros((LANE - GLA_GK_RANK, a_ref.shape[2]), BF16)


def _cast_w_in(w_in_t):
    depth, c, r = w_in_t.shape
    wide = c - GLA_GK_RANK
    last = pl.cdiv(c, CAST_ROWS) - 1
    return pl.pallas_call(
        _cast_w_in_body,
        grid=(depth, wide // CAST_ROWS),
        in_specs=[pl.BlockSpec((1, CAST_ROWS, r), lambda l, k: (l, k, 0)),
                  pl.BlockSpec((1, CAST_ROWS, r), lambda l, k: (l, jnp.clip(k + 1, GZ_BLK + 1, last), 0))],
        out_specs=[pl.BlockSpec((1, CAST_ROWS, r), lambda l, k: (l, k, 0)),
                   pl.BlockSpec((1, LANE, r), lambda l, k: (l, 0, 0))],
        out_shape=[jax.ShapeDtypeStruct((depth, wide, r), BF16),
                   jax.ShapeDtypeStruct((depth, LANE, r), BF16)],
        compiler_params=_params("parallel", "arbitrary"),
        name="cast_w_in",
    )(w_in_t, w_in_t)


def _ffn_body(*refs, final, n_in, n_out, split):
    x_refs = refs[:n_in]
    nw_ref, wg_ref, wu_ref, wd_ref, fw_ref = refs[n_in:n_in + 5]
    o_refs = refs[n_in + 5:n_in + 5 + n_out]
    scratch = refs[n_in + 5 + n_out:]
    xn_ref = scratch[0]
    i = pl.program_id(0)
    j = pl.program_id(1)
    n_i = pl.num_programs(0)
    tm = xn_ref.shape[0]

    if n_out == 1:
        acc_ref = o_refs[0]
    else:
        acc_ref, o_sem = scratch[1:3]

        def put(t, act):
            @pl.when(t < split)
            def _():
                act(pltpu.make_async_copy(acc_ref, o_refs[0].at[pl.ds(t * tm, tm), :], o_sem.at[0]))

            @pl.when(t >= split)
            def _():
                act(pltpu.make_async_copy(acc_ref, o_refs[1].at[pl.ds((t - split) * tm, tm), :], o_sem.at[0]))

        @pl.when(jnp.logical_and(j == 0, i >= 1))
        def _():
            put(i - 1, lambda c: c.wait())

    if n_in == 2:
        x_buf, x_sem = scratch[-2:]

        def fetch(t, act):
            @pl.when(t < split)
            def _():
                act(pltpu.make_async_copy(x_refs[0].at[pl.ds(t * tm, tm), :], x_buf.at[t % 2], x_sem.at[t % 2]))

            @pl.when(t >= split)
            def _():
                act(pltpu.make_async_copy(x_refs[1].at[pl.ds((t - split) * tm, tm), :], x_buf.at[t % 2],
                                          x_sem.at[t % 2]))

        @pl.when(j == 0)
        def _():
            @pl.when(i == 0)
            def _():
                fetch(i, lambda c: c.start())

            @pl.when(i + 1 < n_i)
            def _():
                fetch(i + 1, lambda c: c.start())

            fetch(i, lambda c: c.wait())

    def load_x():
        return x_refs[0][...] if n_in == 1 else x_buf[i % 2]

    @pl.when(j == 0)
    def _():
        x = load_x()
        xn_ref[...] = ((x * _rms_scale(x)) * nw_ref[...]).astype(BF16)
        acc_ref[...] = jnp.zeros_like(acc_ref)

    xn = xn_ref[...]
    h = _dot(xn, wg_ref[...])
    u = _dot(xn, wu_ref[...])
    a = (h * jax.nn.sigmoid(h) * u).astype(BF16)
    acc_ref[...] += _dot(a, wd_ref[...])

    @pl.when(j == pl.num_programs(1) - 1)
    def _():
        y = load_x() + 0.5 * acc_ref[...]
        if final:
            y = (y * _rms_scale(y)) * fw_ref[...]
        acc_ref[...] = y
        if n_out == 2:
            put(i, lambda c: c.start())

            @pl.when(i == n_i - 1)
            def _():
                put(i, lambda c: c.wait())


def _ffn(xs, nw, wg, wu, wd, fw, final, out_rows):
    d = xs[0].shape[1]
    n = sum(x.shape[0] for x in xs)
    assert n == sum(out_rows)
    f = wg.shape[1]
    tm = FFN_ROWS if n % FFN_ROWS == 0 else _row_tile(n)
    first = xs[0].shape[0] if len(xs) == 2 else out_rows[0]
    if first % tm:
        tm = _row_tile(n)
    assert first % tm == 0
    split = first // tm
    anyspace = pl.BlockSpec(memory_space=pl.ANY)
    tile = pl.BlockSpec((tm, d), lambda i, j: (i, 0))
    outs = pl.pallas_call(
        functools.partial(_ffn_body, final=final, n_in=len(xs), n_out=len(out_rows), split=split),
        grid=(n // tm, f // FF_TILE),
        in_specs=([anyspace] * 2 if len(xs) == 2 else [tile]) + [
            pl.BlockSpec((1, d), lambda i, j: (0, 0)),
            pl.BlockSpec((d, FF_TILE), lambda i, j: (0, j)),
            pl.BlockSpec((d, FF_TILE), lambda i, j: (0, j)),
            pl.BlockSpec((FF_TILE, d), lambda i, j: (j, 0)),
            pl.BlockSpec((1, d), lambda i, j: (0, 0)),
        ],
        out_specs=[anyspace] * 2 if len(out_rows) == 2 else [tile],
        out_shape=[jax.ShapeDtypeStruct((r, d), F32) for r in out_rows],
        scratch_shapes=([pltpu.VMEM((tm, d), BF16)]
                        + ([pltpu.VMEM((tm, d), F32), pltpu.SemaphoreType.DMA((1,))] if len(out_rows) == 2 else [])
                        + ([pltpu.VMEM((2, tm, d), F32), pltpu.SemaphoreType.DMA((2,))] if len(xs) == 2 else [])),
        compiler_params=_params("parallel" if len(xs) == len(out_rows) == 1 else "arbitrary", "arbitrary",
                                vmem=FFN_VMEM_LIMIT),
        name="ffn",
    )(*xs, nw, wg, wu, wd, fw)
    return outs


N_MAIN_BLK = MAIN_W // IN_TILE
N_K_BLK = DIFF_QK_W // IN_TILE
N_V_BLK = DIFF_V_W // IN_TILE


assert N_K_BLK == 1 and N_V_BLK == 1


def _inproj_body(x_ref, nw_ref, w_ref, wgz_ref, wgk_ref, bgk_ref, *rest, layer, split, seq_s):
    main_ref, kvb_ref, vt_ref, g_ref, pk_ref, pv_ref, sk_ref, sv_ref, xn_ref, y_ref, sem = rest[-11:]
    i = pl.program_id(0)
    j = pl.program_id(1)
    n_i = pl.num_programs(0)
    tm = x_ref.shape[0]

    @pl.when(j == 0)
    def _():
        x = x_ref[...]
        xn = ((x * _rms_scale(x)) * nw_ref[...]).astype(BF16)
        xn_ref[...] = xn
        gz = _dot_nt(xn, wgz_ref[...])
        z = _dot(gz.astype(BF16), wgk_ref[...]) + bgk_ref[...]
        g_ref[...] = (jnp.minimum(z, 0.0) - jnp.log1p(jnp.exp(-jnp.abs(z)))) * (1.0 / GLA_GATE_NORMALIZER)

    y = _dot_nt(xn_ref[...], w_ref[...])

    @pl.when(j < N_MAIN_BLK)
    def _():
        main_ref[...] = y

    def row_copies(slot, tile, dst_p, dst_s, act):
        heads = [slice(h * DIFF_DV, (h + 1) * DIFF_DV) for h in range(DIFF_HEADS)]

        @pl.when(tile < split)
        def _():
            for h, cols in enumerate(heads):
                act(pltpu.make_async_copy(y_ref.at[slot, :, cols],
                                          dst_p.at[layer, 0, pl.ds(tile * tm, tm), h, :], sem.at[slot]))

        @pl.when(tile >= split)
        def _():
            for b in range(tm // seq_s):
                for h, cols in enumerate(heads):
                    act(pltpu.make_async_copy(y_ref.at[slot, b * seq_s:(b + 1) * seq_s, cols],
                                              dst_s.at[layer, (tile - split) * (tm // seq_s) + b, :, h, :],
                                              sem.at[slot]))

    def kv_step(slot, dst_p, dst_s):
        @pl.when(i > 0)
        def _():
            row_copies(slot, i - 1, dst_p, dst_s, lambda c: c.wait())

        y_ref[slot] = y
        kvb_ref[...] = y.astype(BF16)
        if slot == 1:
            vt_ref[0] = y.T.astype(BF16)
        row_copies(slot, i, dst_p, dst_s, lambda c: c.start())

        @pl.when(i == n_i - 1)
        def _():
            row_copies(slot, i, dst_p, dst_s, lambda c: c.wait())

    @pl.when(j == N_MAIN_BLK)
    def _():
        kv_step(0, pk_ref, sk_ref)

    @pl.when(j == N_MAIN_BLK + 1)
    def _():
        kv_step(1, pv_ref, sv_ref)


def _inproj(x, nw, w_main, w_gz, w_gk, b_gk, kv_out, *, layer, split, seq_s):
    n, d = x.shape
    tm = _row_tile(n)
    assert tm % seq_s == 0 and kv_out[0].shape[1] == 1
    nj = N_MAIN_BLK + 2
    k0 = N_MAIN_BLK
    anyspace = pl.BlockSpec(memory_space=pl.ANY)
    outs = pl.pallas_call(
        functools.partial(_inproj_body, layer=layer, split=split, seq_s=seq_s),
        grid=(n // tm, nj),
        in_specs=[
            pl.BlockSpec((tm, d), lambda i, j: (i, 0)),
            pl.BlockSpec((1, d), lambda i, j: (0, 0)),
            pl.BlockSpec((None, IN_TILE, d), lambda i, j: (layer, j, 0)),
            pl.BlockSpec((None, LANE, d), lambda i, j: (layer, 0, 0)),
            pl.BlockSpec((LANE, GLA_K_W), lambda i, j: (0, 0)),
            pl.BlockSpec((1, GLA_K_W), lambda i, j: (0, 0)),
        ] + [anyspace] * len(kv_out),
        out_specs=[
            pl.BlockSpec((tm, IN_TILE), lambda i, j: (i, jnp.minimum(j, k0 - 1))),
            pl.BlockSpec((tm, IN_TILE), lambda i, j: (i, jnp.clip(j - k0, 0, 1))),
            pl.BlockSpec((1, IN_TILE, tm), lambda i, j: (i, 0, 0)),
            pl.BlockSpec((tm, GLA_K_W), lambda i, j: (i, 0)),
            anyspace, anyspace, anyspace, anyspace,
        ],
        out_shape=[
            jax.ShapeDtypeStruct((n, MAIN_W), F32),
            jax.ShapeDtypeStruct((n, DIFF_QK_W + DIFF_V_W), BF16),
            jax.ShapeDtypeStruct((n // tm, DIFF_V_W, tm), BF16),
            jax.ShapeDtypeStruct((n, GLA_K_W), F32),
        ] + [jax.ShapeDtypeStruct(a.shape, a.dtype) for a in kv_out],
        scratch_shapes=[pltpu.VMEM((tm, d), BF16), pltpu.VMEM((2, tm, IN_TILE), F32),
                        pltpu.SemaphoreType.DMA((2,))],
        input_output_aliases={6 + k: 4 + k for k in range(len(kv_out))},
        compiler_params=_params("arbitrary", "arbitrary"),
        name="inproj",
    )(x, nw, w_main, w_gz, w_gk, b_gk, *kv_out)
    return outs[:4], outs[4:]


def _split3(a):
    hi = a.astype(BF16)
    r = a - hi.astype(F32)
    mid = r.astype(BF16)
    lo = (r - mid.astype(F32)).astype(BF16)
    return hi, mid, lo


def _gla_body(q_ref, k_ref, v_ref, r_ref, g_ref, s0_ref, nw_ref, mix_in_ref, o_ref, s_ref, st_ref, *, chunk):
    del mix_in_ref
    t = pl.program_id(1)
    rows = q_ref.shape[0]
    shift = chunk.bit_length() - 1
    assert chunk == 1 << shift

    @pl.when(t == 0)
    def _():
        for h in range(GLA_HEADS):
            st_ref[h] = s0_ref[0, h].T

    ri = lax.broadcasted_iota(jnp.int32, (rows, rows), 0)
    ci = lax.broadcasted_iota(jnp.int32, (rows, rows), 1)
    causal = jnp.logical_and((ri >> shift) == (ci >> shift), ci <= ri)
    tri = causal.astype(BF16)

    g_hi, g_mid, g_lo = _split3(g_ref[...])
    b_all = _dot(tri, g_hi) + _dot(tri, g_mid) + _dot(tri, g_lo)

    for h in range(GLA_HEADS):
        kc = slice(h * GLA_DK, (h + 1) * GLA_DK)
        vc = slice(h * GLA_DV, (h + 1) * GLA_DV)
        b = b_all[:, kc]
        k = k_ref[:, kc]
        v = v_ref[:, vc].astype(BF16)
        qe = (q_ref[:, kc] * (GLA_DK ** -0.5) * jnp.exp(b)).astype(BF16)
        ke = (k * jnp.exp(-b)).astype(BF16)
        a = jnp.where(causal, _dot_nt(qe, ke), 0.0).astype(BF16)
        o_in = _dot(a, v)
        for c in range(rows // chunk):
            sl = slice(c * chunk, (c + 1) * chunk)
            b_last = b[(c + 1) * chunk - 1:(c + 1) * chunk, :]
            kd = (k[sl] * jnp.exp(b_last - b[sl])).astype(BF16)
            st = st_ref[h]
            o = _dot_nt(qe[sl], st.astype(BF16)) + o_in[sl]
            st_ref[h] = st * jnp.exp(b_last) + _dot_tn(v[sl], kd)
            o = (o * _rms_scale(o)) * nw_ref[...]
            r = r_ref[sl, vc]
            o_ref[sl, vc] = (o * (r * jax.nn.sigmoid(r))).astype(BF16)

    @pl.when(t == pl.num_programs(1) - 1)
    def _():
        for h in range(GLA_HEADS):
            s_ref[0, h] = st_ref[h].T


def _gla(main, g, s0, nw, mix, *, layer, row_off, batch, seq, chunk, rows):
    assert seq % rows == 0 and rows % chunk == 0 and row_off % rows == 0
    nt = seq // rows
    rb0 = row_off // rows

    def rowblk(b, t):
        return rb0 + b * nt + t

    state_spec = pl.BlockSpec((1, GLA_HEADS, GLA_DK, GLA_DV), lambda b, t: (b, 0, 0, 0))
    state_in = pl.BlockSpec((None, 1, GLA_HEADS, GLA_DK, GLA_DV), lambda b, t: (layer, b, 0, 0, 0))
    out, s = pl.pallas_call(
        functools.partial(_gla_body, chunk=chunk),
        grid=(batch, nt),
        in_specs=[
            pl.BlockSpec((rows, GLA_K_W), lambda b, t: (rowblk(b, t), 0)),
            pl.BlockSpec((rows, GLA_K_W), lambda b, t: (rowblk(b, t), 1)),
            pl.BlockSpec((rows, GLA_V_W), lambda b, t: (rowblk(b, t), 2 * GLA_K_W // GLA_V_W)),
            pl.BlockSpec((rows, GLA_V_W), lambda b, t: (rowblk(b, t), 2 * GLA_K_W // GLA_V_W + 1)),
            pl.BlockSpec((rows, GLA_K_W), lambda b, t: (rowblk(b, t), 0)),
            state_in,
            pl.BlockSpec((1, GLA_DV), lambda b, t: (0, 0)),
            pl.BlockSpec(memory_space=pl.ANY),
        ],
        out_specs=[
            pl.BlockSpec((rows, GLA_V_W), lambda b, t: (rowblk(b, t), 0)),
            state_spec,
        ],
        out_shape=[
            jax.ShapeDtypeStruct(mix.shape, mix.dtype),
            jax.ShapeDtypeStruct((batch, GLA_HEADS, GLA_DK, GLA_DV), F32),
        ],
        scratch_shapes=[pltpu.VMEM((GLA_HEADS, GLA_DV, GLA_DK), F32)],
        input_output_aliases={7: 0},
        compiler_params=_params("parallel", "arbitrary"),
        name="gla",
    )(main, main, main, main, g, s0, nw, mix)
    return out, s


def _bucket_thresholds():
    half = N_BUCKETS // 2
    m = half // 2
    e = half - m
    thr = []
    for kk in range(1, e):
        n = m
        while n ** e * m ** kk < m ** e * MAX_DISTANCE ** kk:
            n += 1
        thr.append(n)
    return tuple(thr)


_BUCKET_THR = _bucket_thresholds()


def _t5_bucket(rel):
    half = N_BUCKETS // 2
    max_exact = half // 2
    n = jnp.abs(rel)
    large = jnp.full(rel.shape, max_exact, jnp.int32)
    for thr in _BUCKET_THR:
        large = large + (n >= thr).astype(jnp.int32)
    return jnp.where(rel > 0, half, 0) + jnp.where(n < max_exact, n, large)


def _bias_body(rb_ref, *rest, q_off, k_off, k_step, key_major, scale, casts):
    n = len(casts)
    o_ref = rest[n]
    h = pl.program_id(0)
    t = pl.program_id(1)
    casts.run(h * pl.num_programs(1) + t, rest[:n], rest[n + 1:2 * n + 1], rest[2 * n + 1:])
    shape = o_ref.shape[2:]
    qpos = q_off + lax.broadcasted_iota(jnp.int32, shape, 1 if key_major else 0)
    kpos = k_off + t * k_step + lax.broadcasted_iota(jnp.int32, shape, 0 if key_major else 1)
    bucket = _t5_bucket(kpos - qpos)
    acc = jnp.zeros(shape, F32)
    for bkt in range(N_BUCKETS):
        acc = jnp.where(bucket == bkt, rb_ref[bkt, h], acc)
    visible = (kpos >> 6) <= (qpos >> 6)
    o_ref[0, 0] = jnp.where(visible, acc * scale, -jnp.inf)


assert CHUNK == 64


def _bias_table(rel_bias, cast_weights=(), cast_layers=(), *, tiles, rows, cols, q_off, k_off, k_step,
                key_major=False, scale=1.0):
    casts = _SideCasts(cast_weights, cast_layers, DIFF_HEADS * tiles)
    anyspace = pl.BlockSpec(memory_space=pl.ANY)
    outs = pl.pallas_call(
        functools.partial(_bias_body, q_off=q_off, k_off=k_off, k_step=k_step, key_major=key_major,
                          scale=scale, casts=casts),
        grid=(DIFF_HEADS, tiles),
        in_specs=[pl.BlockSpec(memory_space=pltpu.SMEM)] + [anyspace] * len(casts),
        out_specs=[pl.BlockSpec((1, 1, rows, cols), lambda h, t: (h, t, 0, 0))] + [anyspace] * len(casts),
        out_shape=[jax.ShapeDtypeStruct((DIFF_HEADS, tiles, rows, cols), F32)] + casts.out_shapes(),
        scratch_shapes=casts.scratch_shapes(),
        compiler_params=_params("arbitrary", "arbitrary"),
        name="bias_table",
    )(rel_bias, *cast_weights)
    return outs[0], outs[1:]


def _lambda(lam_ref, layer):
    lam = lam_ref[...]
    a = jnp.sum(lam[0:1, :] * lam[1:2, :], axis=-1, keepdims=True)
    b = jnp.sum(lam[2:3, :] * lam[3:4, :], axis=-1, keepdims=True)
    lam_init = 0.8 - 0.6 * math.exp(-0.3 * layer)
    return jnp.exp(a) - jnp.exp(b) + lam_init, lam_init


def _attn_prompt_body(q_ref, k_ref, vt_ref, bias_ref, lam_ref, nw_ref, mix_in_ref, *rest, layer, blk, casts):
    del mix_in_ref
    n = len(casts)
    cast_srcs, o_ref, cast_dsts = rest[:n], rest[n], rest[n + 1:2 * n + 1]
    acc_ref, s_ref, p_ref = rest[2 * n + 1:2 * n + 4]
    casts.run(pl.program_id(0) * pl.num_programs(1) + pl.program_id(1), cast_srcs, cast_dsts, rest[2 * n + 4:])
    qi = pl.program_id(1)
    qt = (q_ref[...] * (DIFF_D ** -0.5 * LOG2E)).T.astype(BF16)
    qth = (qt[:DIFF_D], qt[DIFF_D:])
    acc_ref[...] = jnp.zeros_like(acc_ref)
    strips = [slice(c * SUB, (c + 1) * SUB) for c in range(blk // SUB)]

    def fold(a):
        return a.reshape(SUB // 8, 8, blk)

    p_ref[...] = jnp.zeros_like(p_ref)

    def step(kj, stats):
        kb = k_ref[pl.ds(pl.multiple_of(kj * blk, blk), blk), :]
        vt_prev = vt_ref[jnp.maximum(kj - 1, 0)]
        t = jnp.minimum(qi - kj, 2)
        for half in range(2):
            s_ref[half] = _dot(kb[:, half * DIFF_D:(half + 1) * DIFF_D], qth[half]) + bias_ref[0, t]
        for half in range(2):
            acc_ref[half] += _dot(vt_prev, p_ref[half])
        out = []
        for half in range(2):
            m, l = stats[half]
            bm = jnp.max(fold(s_ref[half, strips[0], :]), axis=0)
            for c in strips[1:]:
                bm = jnp.maximum(bm, jnp.max(fold(s_ref[half, c, :]), axis=0))
            m_new = jnp.maximum(m, jnp.max(bm, axis=0, keepdims=True))
            alpha = jnp.exp2(m - m_new)
            acc_ref[half] = alpha * acc_ref[half]
            ls = jnp.zeros((8, blk), F32)
            for c in strips:
                p = jnp.exp2(s_ref[half, c, :] - m_new)
                ls = ls + jnp.sum(fold(p), axis=0)
                p_ref[half, c, :] = p.astype(BF16)
            l = alpha * l + jnp.sum(ls, axis=0, keepdims=True)
            out.append((m_new, l))
        return tuple(out)

    init = tuple((jnp.full((1, blk), -jnp.inf, F32), jnp.zeros((1, blk), F32)) for _ in range(2))
    stats = lax.fori_loop(0, qi + 1, step, init)

    lam, lam_init = _lambda(lam_ref, layer)
    vt_last = vt_ref[qi]
    o1 = acc_ref[0] + _dot(vt_last, p_ref[0])
    o2 = acc_ref[1] + _dot(vt_last, p_ref[1])
    ot = o1 / stats[0][1] - lam * (o2 / stats[1][1])
    scale = lax.rsqrt(jnp.mean(ot * ot, axis=0, keepdims=True) + EPS)
    ot = (ot * scale) * nw_ref[...] * (1.0 - lam_init)
    o_ref[...] = ot.T.astype(BF16)


def _attn_prompt(main, kvb, vt, bias, lam, nw_col, mix, cast_weights=(), cast_layers=(), *, layer, seq):
    blk = vt.shape[2]
    assert seq % blk == 0 and blk >= MAX_DISTANCE and blk % CHUNK == 0
    q_col0 = DQ_OFF // (2 * DIFF_D)
    o_col0 = GLA_V_W // DIFF_DV
    casts = _SideCasts(cast_weights, cast_layers, DIFF_HEADS * (seq // blk))
    anyspace = pl.BlockSpec(memory_space=pl.ANY)
    outs = pl.pallas_call(
        functools.partial(_attn_prompt_body, layer=layer, blk=blk, casts=casts),
        grid=(DIFF_HEADS, seq // blk),
        in_specs=[
            pl.BlockSpec((blk, 2 * DIFF_D), lambda h, i: (i, q_col0 + h)),
            pl.BlockSpec((seq, 2 * DIFF_D), lambda h, i: (0, h)),
            pl.BlockSpec((seq // blk, DIFF_DV, blk), lambda h, i: (0, h, 0)),
            pl.BlockSpec((1, 3, blk, blk), lambda h, i: (h, 0, 0, 0)),
            pl.BlockSpec((4, DIFF_D), lambda h, i: (0, 0)),
            pl.BlockSpec((DIFF_DV, 1), lambda h, i: (0, 0)),
            anyspace,
        ] + [anyspace] * len(casts),
        out_specs=[pl.BlockSpec((blk, DIFF_DV), lambda h, i: (i, o_col0 + h))] + [anyspace] * len(casts),
        out_shape=[jax.ShapeDtypeStruct(mix.shape, mix.dtype)] + casts.out_shapes(),
        scratch_shapes=[pltpu.VMEM((2, DIFF_DV, blk), F32), pltpu.VMEM((2, blk, blk), F32),
                        pltpu.VMEM((2, blk, blk), BF16)] + casts.scratch_shapes(),
        input_output_aliases={6: 0},
        compiler_params=_params("arbitrary", "arbitrary"),
        name="attn_prompt",
    )(main, kvb, vt, bias, lam, nw_col, mix, *cast_weights)
    return outs[0], outs[1:]


def _attn_sample_body(q_ref, kn_ref, vn_ref, kc_hbm, vc_hbm, bias_ref, lam_ref, nw_ref, mix_in_ref,
                      o_ref, kbuf, vbuf, sem, *, layer, past):
    del mix_in_ref
    b = pl.program_id(0)
    h = pl.program_id(1)
    n_h = DIFF_HEADS
    step = b * n_h + h
    n_steps = pl.num_programs(0) * n_h
    slot = step % SAMPLE_SLOTS

    def copies(cstep):
        cb, ch, cslot = cstep // n_h, cstep % n_h, cstep % SAMPLE_SLOTS
        return (pltpu.make_async_copy(kc_hbm.at[layer, cb, :, ch, :], kbuf.at[cslot], sem.at[0, cslot]),
                pltpu.make_async_copy(vc_hbm.at[layer, cb, :, ch, :], vbuf.at[cslot], sem.at[1, cslot]))

    def start(cstep):
        @pl.when(cstep < n_steps)
        def _():
            for c in copies(cstep):
                c.start()

    @pl.when(step == 0)
    def _():
        for ahead in range(SAMPLE_SLOTS - 1):
            start(step + ahead)

    start(step + SAMPLE_SLOTS - 1)

    qs = (q_ref[...] * (DIFF_D ** -0.5)).astype(BF16)
    kn = kn_ref[...]
    bias_c = bias_ref[0, 0, :, :past]
    bias_n = bias_ref[0, 0, :, past:]
    for c in copies(step):
        c.wait()
    ps = []
    for half in range(2):
        cols = slice(half * DIFF_D, (half + 1) * DIFF_D)
        sc = _dot_nt(qs[:, cols], kbuf[slot, :, cols].astype(BF16)) + bias_c
        sn = _dot_nt(qs[:, cols], kn[:, cols]) + bias_n
        m = jnp.maximum(jnp.max(sc, axis=-1, keepdims=True), jnp.max(sn, axis=-1, keepdims=True))
        pc = jnp.exp(sc - m)
        pn = jnp.exp(sn - m)
        inv = 1.0 / (jnp.sum(pc, axis=-1, keepdims=True) + jnp.sum(pn, axis=-1, keepdims=True))
        ps.append((pc * inv, pn * inv))
    lam, lam_init = _lambda(lam_ref, layer)
    wc = (ps[0][0] - lam * ps[1][0]).astype(BF16)
    wn = (ps[0][1] - lam * ps[1][1]).astype(BF16)
    o = _dot(wc, vbuf[slot].astype(BF16)) + _dot(wn, vn_ref[...])
    o_ref[...] = ((o * _rms_scale(o)) * nw_ref[...] * (1.0 - lam_init)).astype(BF16)


def _attn_sample(main, kvb, cache_k, cache_v, bias, lam, nw, mix, *, layer, row_off, batch, seq):
    past = cache_k.shape[2]
    assert row_off % seq == 0
    rb0 = row_off // seq
    q_col0 = DQ_OFF // (2 * DIFF_D)
    v_col0 = DIFF_QK_W // DIFF_DV
    o_col0 = GLA_V_W // DIFF_DV
    return pl.pallas_call(
        functools.partial(_attn_sample_body, layer=layer, past=past),
        grid=(batch, DIFF_HEADS),
        in_specs=[
            pl.BlockSpec((seq, 2 * DIFF_D), lambda b, h: (rb0 + b, q_col0 + h)),
            pl.BlockSpec((seq, 2 * DIFF_D), lambda b, h: (rb0 + b, h)),
            pl.BlockSpec((seq, DIFF_DV), lambda b, h: (rb0 + b, v_col0 + h)),
            pl.BlockSpec(memory_space=pl.ANY),
            pl.BlockSpec(memory_space=pl.ANY),
            pl.BlockSpec((1, 1, seq, past + seq), lambda b, h: (h, 0, 0, 0)),
            pl.BlockSpec((4, DIFF_D), lambda b, h: (0, 0)),
            pl.BlockSpec((1, DIFF_DV), lambda b, h: (0, 0)),
            pl.BlockSpec(memory_space=pl.ANY),
        ],
        out_specs=pl.BlockSpec((seq, DIFF_DV), lambda b, h: (rb0 + b, o_col0 + h)),
        out_shape=jax.ShapeDtypeStruct(mix.shape, mix.dtype),
        scratch_shapes=[
            pltpu.VMEM((SAMPLE_SLOTS, past, 2 * DIFF_D), F32),
            pltpu.VMEM((SAMPLE_SLOTS, past, DIFF_DV), F32),
            pltpu.SemaphoreType.DMA((2, SAMPLE_SLOTS)),
        ],
        input_output_aliases={8: 0},
        compiler_params=_params("arbitrary", "arbitrary"),
        name="attn_sample",
    )(main, kvb, kvb, cache_k, cache_v, bias, lam, nw, mix)


def _outproj_body(o_ref, w_ref, x_ref, y_ref):
    y_ref[...] = x_ref[...] + _dot(o_ref[...], w_ref[...])


def _outproj(mix, w, x):
    n, d = x.shape
    tm = _row_tile(n)
    return pl.pallas_call(
        _outproj_body,
        grid=(n // tm,),
        in_specs=[
            pl.BlockSpec((tm, MIX_WIDTH), lambda i: (i, 0)),
            pl.BlockSpec((MIX_WIDTH, d), lambda i: (0, 0)),
            pl.BlockSpec((tm, d), lambda i: (i, 0)),
        ],
        out_specs=pl.BlockSpec((tm, d), lambda i: (i, 0)),
        out_shape=jax.ShapeDtypeStruct((n, d), F32),
        compiler_params=_params("parallel"),
        name="outproj",
    )(mix, w, x)


def kernel(x_prompt, x_sample, cache_k, cache_v, state_gla, ffn1_norm, ffn1_w_gate, ffn1_w_up, ffn1_w_down,
           mix_norm, w_in, gla_w_gk, gla_b_gk, gla_norm, diff_lambda, diff_norm, w_out,
           ffn2_norm, ffn2_w_gate, ffn2_w_up, ffn2_w_down, rel_bias, final_norm):
    pb, ps, d = x_prompt.shape
    sb, ss, _ = x_sample.shape
    depth = w_in.shape[0]
    past = cache_k.shape[2]
    assert pb == 1
    n_p, n_s = pb * ps, sb * ss
    xs = [x_prompt.reshape(n_p, d), x_sample.reshape(n_s, d)]

    blk = _row_tile(n_p + n_s)
    ffn1_w = (ffn1_w_gate, ffn1_w_up, ffn1_w_down)
    ffn2_w = (ffn2_w_gate, ffn2_w_up, ffn2_w_down)
    bias_p, first = _bias_table(rel_bias, ffn1_w, [0] * len(ffn1_w), tiles=3, rows=blk, cols=blk, q_off=0, k_off=0,
                                k_step=-blk, key_major=True, scale=LOG2E)
    bias_s, _ = _bias_table(rel_bias, tiles=1, rows=ss, cols=past + ss, q_off=past, k_off=0, k_step=0)
    zero_state = jnp.zeros((1, pb, GLA_HEADS, GLA_DK, GLA_DV), F32)
    row = lambda a: a.reshape(1, -1)

    w_main, w_gz = _cast_w_in(jnp.swapaxes(w_in, 1, 2))
    f1 = {0: list(first)}
    later = [(w, l) for l in range(depth) for w in ((ffn1_w if l else ()) + ffn2_w + (w_out,))]
    f2, w_o = {}, {}

    kv_out = [jnp.zeros((depth, pb, ps, DIFF_HEADS, DIFF_DV), F32) for _ in range(2)]
    kv_out += [jnp.zeros((depth, sb, ss, DIFF_HEADS, DIFF_DV), F32) for _ in range(2)]
    p_states, s_states = [], []
    mix = jnp.zeros((n_p + n_s, MIX_WIDTH), BF16)
    for l in range(depth):
        w_gk = jnp.pad(gla_w_gk[l], ((0, LANE - GLA_GK_RANK), (0, 0))).astype(BF16)

        x, = _ffn(xs, row(ffn1_norm[l]), *f1[l], row(final_norm), False, [n_p + n_s])
        (main, kvb, vt, g), kv_out = _inproj(x, row(mix_norm[l]), w_main, w_gz, w_gk, row(gla_b_gk[l]), kv_out,
                                             layer=l, split=n_p // blk, seq_s=ss)

        mix, sp = _gla(main, g, zero_state, row(gla_norm[l]), mix,
                       layer=0, row_off=0, batch=pb, seq=ps, chunk=CHUNK, rows=min(GLA_ROWS, ps))
        mix, s_s = _gla(main, g, state_gla, row(gla_norm[l]), mix,
                        layer=l, row_off=n_p, batch=sb, seq=ss, chunk=ss, rows=ss)
        mix, cast = _attn_prompt(main, kvb, vt, bias_p, diff_lambda[l], diff_norm[l].reshape(-1, 1), mix,
                                 [w for w, _ in later] if l == 0 else [], [k for _, k in later] if l == 0 else [],
                                 layer=l, seq=ps)
        cast = list(cast)
        for k in range(depth if l == 0 else 0):
            if k:
                f1[k] = [cast.pop(0) for _ in ffn1_w]
            f2[k] = [cast.pop(0) for _ in ffn2_w]
            w_o[k] = cast.pop(0)
        mix = _attn_sample(main, kvb, cache_k, cache_v, bias_s, diff_lambda[l], row(diff_norm[l]), mix,
                           layer=l, row_off=n_p, batch=sb, seq=ss)

        x = _outproj(mix, w_o[l], x)
        last = l == depth - 1
        xs = _ffn([x], row(ffn2_norm[l]), *f2[l], row(final_norm), last, [n_p, n_s] if last else [n_p + n_s])
        p_states.append(sp)
        s_states.append(s_s)

    prompt_k, prompt_v, sample_k, sample_v = kv_out
    return (xs[0].reshape(pb, ps, d), xs[1].reshape(sb, ss, d), prompt_k, prompt_v, jnp.stack(p_states),
            sample_k, sample_v, jnp.stack(s_states))
```

```python
import functools
import math

import jax
import jax.numpy as jnp
from jax import lax
from jax.experimental import pallas as pl
from jax.experimental.pallas import tpu as pltpu

F32 = jnp.float32
BF16 = jnp.bfloat16

EPS = 1e-6
CHUNK = 64
GLA_HEADS = 4
GLA_DK = 128
GLA_DV = 256
GLA_GK_RANK = 16
GLA_GATE_NORMALIZER = 16.0
DIFF_HEADS = 4
DIFF_D = 128
DIFF_DV = 2 * DIFF_D
N_BUCKETS = 32
MAX_DISTANCE = 128

GLA_K_W = GLA_HEADS * GLA_DK
GLA_V_W = GLA_HEADS * GLA_DV
DIFF_QK_W = DIFF_HEADS * 2 * DIFF_D
DIFF_V_W = DIFF_HEADS * DIFF_DV
MIX_WIDTH = GLA_V_W + DIFF_V_W
GZ_OFF = 2 * GLA_K_W + 2 * GLA_V_W
MAIN_W = GZ_OFF + DIFF_QK_W
DQ_OFF = GZ_OFF

LANE = 128
BF16_ROWS = 16
LOG2E = math.log2(math.e)
SUB = 64
VMEM_LIMIT = 56 * 1024 * 1024
FFN_VMEM_LIMIT = 60 * 1024 * 1024

ROW_TILE = 512
FFN_ROWS = 1024
FF_TILE = 512
IN_TILE = 1024
GLA_ROWS = 512
SAMPLE_SLOTS = 4


def _dot(a, b):
    return jnp.dot(a, b, preferred_element_type=F32)


def _dot_nt(a, b):
    return lax.dot_general(a, b, (((1,), (1,)), ((), ())), preferred_element_type=F32)


def _dot_tn(a, b):
    return lax.dot_general(a, b, (((0,), (0,)), ((), ())), preferred_element_type=F32)


def _params(*sem, vmem=VMEM_LIMIT):
    return pltpu.CompilerParams(dimension_semantics=sem, vmem_limit_bytes=vmem)


def _row_tile(n):
    t = ROW_TILE
    while n % t:
        t //= 2
    return t


def _rms_scale(x):
    return lax.rsqrt(jnp.mean(x * x, axis=-1, keepdims=True) + EPS)


CAST_ROWS = 512


def _cast_rows(rows, steps):
    return -(-rows // (BF16_ROWS * steps)) * BF16_ROWS


class _SideCasts:
    def __init__(self, weights, layers, steps):
        self.layers = layers
        self.steps = steps
        self.shapes = [w.shape[1:] for w in weights]
        self.rows = [_cast_rows(s[0], steps) for s in self.shapes]
        assert all(s[0] >= r and s[0] % BF16_ROWS == 0 for s, r in zip(self.shapes, self.rows))

    def __len__(self):
        return len(self.shapes)

    def out_shapes(self):
        return [jax.ShapeDtypeStruct(s, BF16) for s in self.shapes]

    def scratch_shapes(self):
        n = len(self)
        if n == 0:
            return []
        return ([pltpu.VMEM((2, r, s[1]), F32) for s, r in zip(self.shapes, self.rows)]
                + [pltpu.VMEM((r, s[1]), BF16) for s, r in zip(self.shapes, self.rows)]
                + [pltpu.SemaphoreType.DMA((n, 2)), pltpu.SemaphoreType.DMA((n,))])

    def run(self, step, srcs, dsts, scratch):
        n = len(self)
        if n == 0:
            return
        in_bufs, out_bufs, in_sem, out_sem = scratch[:n], scratch[n:2 * n], scratch[2 * n], scratch[2 * n + 1]

        def start_row(k, t):
            r, total = self.rows[k], self.shapes[k][0]
            return pl.multiple_of(jnp.minimum(t * r, total - r), BF16_ROWS)

        def fetch(k, t):
            return pltpu.make_async_copy(srcs[k].at[self.layers[k], pl.ds(start_row(k, t), self.rows[k]), :],
                                         in_bufs[k].at[t % 2], in_sem.at[k, t % 2])

        def store(k, t):
            return pltpu.make_async_copy(out_bufs[k], dsts[k].at[pl.ds(start_row(k, t), self.rows[k]), :],
                                         out_sem.at[k])

        @pl.when(step == 0)
        def _():
            for k in range(n):
                fetch(k, step).start()

        @pl.when(step + 1 < self.steps)
        def _():
            for k in range(n):
                fetch(k, step + 1).start()

        for k in range(n):
            fetch(k, step).wait()

            @pl.when(step > 0)
            def _():
                store(k, step - 1).wait()

            out_bufs[k][...] = in_bufs[k][step % 2].astype(BF16)
            store(k, step).start()

        @pl.when(step == self.steps - 1)
        def _():
            for k in range(n):
                store(k, step).wait()


GZ_BLK = GZ_OFF // CAST_ROWS
assert GZ_OFF % CAST_ROWS == 0 and GLA_GK_RANK % BF16_ROWS == 0 and GLA_GK_RANK < CAST_ROWS


def _cast_w_in_body(a_ref, b_ref, main_ref, gz_ref):
    k = pl.program_id(1)

    @pl.when(k < GZ_BLK)
    def _():
        main_ref[0] = a_ref[0].astype(BF16)

    @pl.when(k >= GZ_BLK)
    def _():
        main_ref[0, :CAST_ROWS - GLA_GK_RANK] = a_ref[0, GLA_GK_RANK:].astype(BF16)
        main_ref[0, CAST_ROWS - GLA_GK_RANK:] = b_ref[0, :GLA_GK_RANK].astype(BF16)

    @pl.when(k == GZ_BLK)
    def _():
        gz_ref[0, :GLA_GK_RANK] = a_ref[0, :GLA_GK_RANK].astype(BF16)
        gz_ref[0, GLA_GK_RANK:] = jnp.zeros((LANE - GLA_GK_RANK, a_ref.shape[2]), BF16)


def _cast_w_in(w_in_t):
    depth, c, r = w_in_t.shape
    wide = c - GLA_GK_RANK
    last = pl.cdiv(c, CAST_ROWS) - 1
    return pl.pallas_call(
        _cast_w_in_body,
        grid=(depth, wide // CAST_ROWS),
        in_specs=[pl.BlockSpec((1, CAST_ROWS, r), lambda l, k: (l, k, 0)),
                  pl.BlockSpec((1, CAST_ROWS, r), lambda l, k: (l, jnp.clip(k + 1, GZ_BLK + 1, last), 0))],
        out_specs=[pl.BlockSpec((1, CAST_ROWS, r), lambda l, k: (l, k, 0)),
                   pl.BlockSpec((1, LANE, r), lambda l, k: (l, 0, 0))],
        out_shape=[jax.ShapeDtypeStruct((depth, wide, r), BF16),
                   jax.ShapeDtypeStruct((depth, LANE, r), BF16)],
        compiler_params=_params("parallel", "arbitrary"),
        name="cast_w_in",
    )(w_in_t, w_in_t)


def _ffn_body(*refs, final, n_in, n_out, split):
    x_refs = refs[:n_in]
    nw_ref, wg_ref, wu_ref, wd_ref, fw_ref = refs[n_in:n_in + 5]
    o_refs = refs[n_in + 5:n_in + 5 + n_out]
    scratch = refs[n_in + 5 + n_out:]
    xn_ref = scratch[0]
    i = pl.program_id(0)
    j = pl.program_id(1)
    n_i = pl.num_programs(0)
    tm = xn_ref.shape[0]

    if n_out == 1:
        acc_ref = o_refs[0]
    else:
        acc_ref, o_sem = scratch[1:3]

        def put(t, act):
            @pl.when(t < split)
            def _():
                act(pltpu.make_async_copy(acc_ref, o_refs[0].at[pl.ds(t * tm, tm), :], o_sem.at[0]))

            @pl.when(t >= split)
            def _():
                act(pltpu.make_async_copy(acc_ref, o_refs[1].at[pl.ds((t - split) * tm, tm), :], o_sem.at[0]))

        @pl.when(jnp.logical_and(j == 0, i >= 1))
        def _():
            put(i - 1, lambda c: c.wait())

    if n_in == 2:
        x_buf, x_sem = scratch[-2:]

        def fetch(t, act):
            @pl.when(t < split)
            def _():
                act(pltpu.make_async_copy(x_refs[0].at[pl.ds(t * tm, tm), :], x_buf.at[t % 2], x_sem.at[t % 2]))

            @pl.when(t >= split)
            def _():
                act(pltpu.make_async_copy(x_refs[1].at[pl.ds((t - split) * tm, tm), :], x_buf.at[t % 2],
                                          x_sem.at[t % 2]))

        @pl.when(j == 0)
        def _():
            @pl.when(i == 0)
            def _():
                fetch(i, lambda c: c.start())

            @pl.when(i + 1 < n_i)
            def _():
                fetch(i + 1, lambda c: c.start())

            fetch(i, lambda c: c.wait())

    def load_x():
        return x_refs[0][...] if n_in == 1 else x_buf[i % 2]

    @pl.when(j == 0)
    def _():
        x = load_x()
        xn_ref[...] = ((x * _rms_scale(x)) * nw_ref[...]).astype(BF16)
        acc_ref[...] = jnp.zeros_like(acc_ref)

    xn = xn_ref[...]
    h = _dot(xn, wg_ref[...])
    u = _dot(xn, wu_ref[...])
    a = (h * jax.nn.sigmoid(h) * u).astype(BF16)
    acc_ref[...] += _dot(a, wd_ref[...])

    @pl.when(j == pl.num_programs(1) - 1)
    def _():
        y = load_x() + 0.5 * acc_ref[...]
        if final:
            y = (y * _rms_scale(y)) * fw_ref[...]
        acc_ref[...] = y
        if n_out == 2:
            put(i, lambda c: c.start())

            @pl.when(i == n_i - 1)
            def _():
                put(i, lambda c: c.wait())


def _ffn(xs, nw, wg, wu, wd, fw, final, out_rows):
    d = xs[0].shape[1]
    n = sum(x.shape[0] for x in xs)
    assert n == sum(out_rows)
    f = wg.shape[1]
    tm = FFN_ROWS if n % FFN_ROWS == 0 else _row_tile(n)
    first = xs[0].shape[0] if len(xs) == 2 else out_rows[0]
    if first % tm:
        tm = _row_tile(n)
    assert first % tm == 0
    split = first // tm
    anyspace = pl.BlockSpec(memory_space=pl.ANY)
    tile = pl.BlockSpec((tm, d), lambda i, j: (i, 0))
    outs = pl.pallas_call(
        functools.partial(_ffn_body, final=final, n_in=len(xs), n_out=len(out_rows), split=split),
        grid=(n // tm, f // FF_TILE),
        in_specs=([anyspace] * 2 if len(xs) == 2 else [tile]) + [
            pl.BlockSpec((1, d), lambda i, j: (0, 0)),
            pl.BlockSpec((d, FF_TILE), lambda i, j: (0, j)),
            pl.BlockSpec((d, FF_TILE), lambda i, j: (0, j)),
            pl.BlockSpec((FF_TILE, d), lambda i, j: (j, 0)),
            pl.BlockSpec((1, d), lambda i, j: (0, 0)),
        ],
        out_specs=[anyspace] * 2 if len(out_rows) == 2 else [tile],
        out_shape=[jax.ShapeDtypeStruct((r, d), F32) for r in out_rows],
        scratch_shapes=([pltpu.VMEM((tm, d), BF16)]
                        + ([pltpu.VMEM((tm, d), F32), pltpu.SemaphoreType.DMA((1,))] if len(out_rows) == 2 else [])
                        + ([pltpu.VMEM((2, tm, d), F32), pltpu.SemaphoreType.DMA((2,))] if len(xs) == 2 else [])),
        compiler_params=_params("parallel" if len(xs) == len(out_rows) == 1 else "arbitrary", "arbitrary",
                                vmem=FFN_VMEM_LIMIT),
        name="ffn",
    )(*xs, nw, wg, wu, wd, fw)
    return outs


N_MAIN_BLK = MAIN_W // IN_TILE
N_K_BLK = DIFF_QK_W // IN_TILE
N_V_BLK = DIFF_V_W // IN_TILE


assert N_K_BLK == 1 and N_V_BLK == 1


def _inproj_body(x_ref, nw_ref, w_ref, wgz_ref, wgk_ref, bgk_ref, *rest, layer, split, seq_s):
    main_ref, kvb_ref, vt_ref, g_ref, pk_ref, pv_ref, sk_ref, sv_ref, xn_ref, y_ref, sem = rest[-11:]
    i = pl.program_id(0)
    j = pl.program_id(1)
    n_i = pl.num_programs(0)
    tm = x_ref.shape[0]

    @pl.when(j == 0)
    def _():
        x = x_ref[...]
        xn = ((x * _rms_scale(x)) * nw_ref[...]).astype(BF16)
        xn_ref[...] = xn
        gz = _dot_nt(xn, wgz_ref[...])
        z = _dot(gz.astype(BF16), wgk_ref[...]) + bgk_ref[...]
        g_ref[...] = (jnp.minimum(z, 0.0) - jnp.log1p(jnp.exp(-jnp.abs(z)))) * (1.0 / GLA_GATE_NORMALIZER)

    y = _dot_nt(xn_ref[...], w_ref[...])

    @pl.when(j < N_MAIN_BLK)
    def _():
        main_ref[...] = y

    def row_copies(slot, tile, dst_p, dst_s, act):
        heads = [slice(h * DIFF_DV, (h + 1) * DIFF_DV) for h in range(DIFF_HEADS)]

        @pl.when(tile < split)
        def _():
            for h, cols in enumerate(heads):
                act(pltpu.make_async_copy(y_ref.at[slot, :, cols],
                                          dst_p.at[layer, 0, pl.ds(tile * tm, tm), h, :], sem.at[slot]))

        @pl.when(tile >= split)
        def _():
            for b in range(tm // seq_s):
                for h, cols in enumerate(heads):
                    act(pltpu.make_async_copy(y_ref.at[slot, b * seq_s:(b + 1) * seq_s, cols],
                                              dst_s.at[layer, (tile - split) * (tm // seq_s) + b, :, h, :],
                                              sem.at[slot]))

    def kv_step(slot, dst_p, dst_s):
        @pl.when(i > 0)
        def _():
            row_copies(slot, i - 1, dst_p, dst_s, lambda c: c.wait())

        y_ref[slot] = y
        kvb_ref[...] = y.astype(BF16)
        if slot == 1:
            vt_ref[0] = y.T.astype(BF16)
        row_copies(slot, i, dst_p, dst_s, lambda c: c.start())

        @pl.when(i == n_i - 1)
        def _():
            row_copies(slot, i, dst_p, dst_s, lambda c: c.wait())

    @pl.when(j == N_MAIN_BLK)
    def _():
        kv_step(0, pk_ref, sk_ref)

    @pl.when(j == N_MAIN_BLK + 1)
    def _():
        kv_step(1, pv_ref, sv_ref)


def _inproj(x, nw, w_main, w_gz, w_gk, b_gk, kv_out, *, layer, split, seq_s):
    n, d = x.shape
    tm = _row_tile(n)
    assert tm % seq_s == 0 and kv_out[0].shape[1] == 1
    nj = N_MAIN_BLK + 2
    k0 = N_MAIN_BLK
    anyspace = pl.BlockSpec(memory_space=pl.ANY)
    outs = pl.pallas_call(
        functools.partial(_inproj_body, layer=layer, split=split, seq_s=seq_s),
        grid=(n // tm, nj),
        in_specs=[
            pl.BlockSpec((tm, d), lambda i, j: (i, 0)),
            pl.BlockSpec((1, d), lambda i, j: (0, 0)),
            pl.BlockSpec((None, IN_TILE, d), lambda i, j: (layer, j, 0)),
            pl.BlockSpec((None, LANE, d), lambda i, j: (layer, 0, 0)),
            pl.BlockSpec((LANE, GLA_K_W), lambda i, j: (0, 0)),
            pl.BlockSpec((1, GLA_K_W), lambda i, j: (0, 0)),
        ] + [anyspace] * len(kv_out),
        out_specs=[
            pl.BlockSpec((tm, IN_TILE), lambda i, j: (i, jnp.minimum(j, k0 - 1))),
            pl.BlockSpec((tm, IN_TILE), lambda i, j: (i, jnp.clip(j - k0, 0, 1))),
            pl.BlockSpec((1, IN_TILE, tm), lambda i, j: (i, 0, 0)),
            pl.BlockSpec((tm, GLA_K_W), lambda i, j: (i, 0)),
            anyspace, anyspace, anyspace, anyspace,
        ],
        out_shape=[
            jax.ShapeDtypeStruct((n, MAIN_W), F32),
            jax.ShapeDtypeStruct((n, DIFF_QK_W + DIFF_V_W), BF16),
            jax.ShapeDtypeStruct((n // tm, DIFF_V_W, tm), BF16),
            jax.ShapeDtypeStruct((n, GLA_K_W), F32),
        ] + [jax.ShapeDtypeStruct(a.shape, a.dtype) for a in kv_out],
        scratch_shapes=[pltpu.VMEM((tm, d), BF16), pltpu.VMEM((2, tm, IN_TILE), F32),
                        pltpu.SemaphoreType.DMA((2,))],
        input_output_aliases={6 + k: 4 + k for k in range(len(kv_out))},
        compiler_params=_params("arbitrary", "arbitrary"),
        name="inproj",
    )(x, nw, w_main, w_gz, w_gk, b_gk, *kv_out)
    return outs[:4], outs[4:]


def _split3(a):
    hi = a.astype(BF16)
    r = a - hi.astype(F32)
    mid = r.astype(BF16)
    lo = (r - mid.astype(F32)).astype(BF16)
    return hi, mid, lo


def _gla_body(q_ref, k_ref, v_ref, r_ref, g_ref, s0_ref, nw_ref, mix_in_ref, o_ref, s_ref, st_ref, *, chunk):
    del mix_in_ref
    t = pl.program_id(1)
    rows = q_ref.shape[0]
    shift = chunk.bit_length() - 1
    assert chunk == 1 << shift

    @pl.when(t == 0)
    def _():
        for h in range(GLA_HEADS):
            st_ref[h] = s0_ref[0, h].T

    ri = lax.broadcasted_iota(jnp.int32, (rows, rows), 0)
    ci = lax.broadcasted_iota(jnp.int32, (rows, rows), 1)
    causal = jnp.logical_and((ri >> shift) == (ci >> shift), ci <= ri)
    tri = causal.astype(BF16)

    g_hi, g_mid, g_lo = _split3(g_ref[...])
    b_all = _dot(tri, g_hi) + _dot(tri, g_mid) + _dot(tri, g_lo)

    for h in range(GLA_HEADS):
        kc = slice(h * GLA_DK, (h + 1) * GLA_DK)
        vc = slice(h * GLA_DV, (h + 1) * GLA_DV)
        b = b_all[:, kc]
        k = k_ref[:, kc]
        v = v_ref[:, vc].astype(BF16)
        qe = (q_ref[:, kc] * (GLA_DK ** -0.5) * jnp.exp(b)).astype(BF16)
        ke = (k * jnp.exp(-b)).astype(BF16)
        a = jnp.where(causal, _dot_nt(qe, ke), 0.0).astype(BF16)
        o_in = _dot(a, v)
        for c in range(rows // chunk):
            sl = slice(c * chunk, (c + 1) * chunk)
            b_last = b[(c + 1) * chunk - 1:(c + 1) * chunk, :]
            kd = (k[sl] * jnp.exp(b_last - b[sl])).astype(BF16)
            st = st_ref[h]
            o = _dot_nt(qe[sl], st.astype(BF16)) + o_in[sl]
            st_ref[h] = st * jnp.exp(b_last) + _dot_tn(v[sl], kd)
            o = (o * _rms_scale(o)) * nw_ref[...]
            r = r_ref[sl, vc]
            o_ref[sl, vc] = (o * (r * jax.nn.sigmoid(r))).astype(BF16)

    @pl.when(t == pl.num_programs(1) - 1)
    def _():
        for h in range(GLA_HEADS):
            s_ref[0, h] = st_ref[h].T


def _gla(main, g, s0, nw, mix, *, layer, row_off, batch, seq, chunk, rows):
    assert seq % rows == 0 and rows % chunk == 0 and row_off % rows == 0
    nt = seq // rows
    rb0 = row_off // rows

    def rowblk(b, t):
        return rb0 + b * nt + t

    state_spec = pl.BlockSpec((1, GLA_HEADS, GLA_DK, GLA_DV), lambda b, t: (b, 0, 0, 0))
    state_in = pl.BlockSpec((None, 1, GLA_HEADS, GLA_DK, GLA_DV), lambda b, t: (layer, b, 0, 0, 0))
    out, s = pl.pallas_call(
        functools.partial(_gla_body, chunk=chunk),
        grid=(batch, nt),
        in_specs=[
            pl.BlockSpec((rows, GLA_K_W), lambda b, t: (rowblk(b, t), 0)),
            pl.BlockSpec((rows, GLA_K_W), lambda b, t: (rowblk(b, t), 1)),
            pl.BlockSpec((rows, GLA_V_W), lambda b, t: (rowblk(b, t), 2 * GLA_K_W // GLA_V_W)),
            pl.BlockSpec((rows, GLA_V_W), lambda b, t: (rowblk(b, t), 2 * GLA_K_W // GLA_V_W + 1)),
            pl.BlockSpec((rows, GLA_K_W), lambda b, t: (rowblk(b, t), 0)),
            state_in,
            pl.BlockSpec((1, GLA_DV), lambda b, t: (0, 0)),
            pl.BlockSpec(memory_space=pl.ANY),
        ],
        out_specs=[
            pl.BlockSpec((rows, GLA_V_W), lambda b, t: (rowblk(b, t), 0)),
            state_spec,
        ],
        out_shape=[
            jax.ShapeDtypeStruct(mix.shape, mix.dtype),
            jax.ShapeDtypeStruct((batch, GLA_HEADS, GLA_DK, GLA_DV), F32),
        ],
        scratch_shapes=[pltpu.VMEM((GLA_HEADS, GLA_DV, GLA_DK), F32)],
        input_output_aliases={7: 0},
        compiler_params=_params("parallel", "arbitrary"),
        name="gla",
    )(main, main, main, main, g, s0, nw, mix)
    return out, s


def _bucket_thresholds():
    half = N_BUCKETS // 2
    m = half // 2
    e = half - m
    thr = []
    for kk in range(1, e):
        n = m
        while n ** e * m ** kk < m ** e * MAX_DISTANCE ** kk:
            n += 1
        thr.append(n)
    return tuple(thr)


_BUCKET_THR = _bucket_thresholds()


def _t5_bucket(rel):
    half = N_BUCKETS // 2
    max_exact = half // 2
    n = jnp.abs(rel)
    large = jnp.full(rel.shape, max_exact, jnp.int32)
    for thr in _BUCKET_THR:
        large = large + (n >= thr).astype(jnp.int32)
    return jnp.where(rel > 0, half, 0) + jnp.where(n < max_exact, n, large)


def _bias_body(rb_ref, *rest, q_off, k_off, k_step, key_major, scale, casts):
    n = len(casts)
    o_ref = rest[n]
    h = pl.program_id(0)
    t = pl.program_id(1)
    casts.run(h * pl.num_programs(1) + t, rest[:n], rest[n + 1:2 * n + 1], rest[2 * n + 1:])
    shape = o_ref.shape[2:]
    qpos = q_off + lax.broadcasted_iota(jnp.int32, shape, 1 if key_major else 0)
    kpos = k_off + t * k_step + lax.broadcasted_iota(jnp.int32, shape, 0 if key_major else 1)
    bucket = _t5_bucket(kpos - qpos)
    acc = jnp.zeros(shape, F32)
    for bkt in range(N_BUCKETS):
        acc = jnp.where(bucket == bkt, rb_ref[bkt, h], acc)
    visible = (kpos >> 6) <= (qpos >> 6)
    o_ref[0, 0] = jnp.where(visible, acc * scale, -jnp.inf)


assert CHUNK == 64


def _bias_table(rel_bias, cast_weights=(), cast_layers=(), *, tiles, rows, cols, q_off, k_off, k_step,
                key_major=False, scale=1.0):
    casts = _SideCasts(cast_weights, cast_layers, DIFF_HEADS * tiles)
    anyspace = pl.BlockSpec(memory_space=pl.ANY)
    outs = pl.pallas_call(
        functools.partial(_bias_body, q_off=q_off, k_off=k_off, k_step=k_step, key_major=key_major,
                          scale=scale, casts=casts),
        grid=(DIFF_HEADS, tiles),
        in_specs=[pl.BlockSpec(memory_space=pltpu.SMEM)] + [anyspace] * len(casts),
        out_specs=[pl.BlockSpec((1, 1, rows, cols), lambda h, t: (h, t, 0, 0))] + [anyspace] * len(casts),
        out_shape=[jax.ShapeDtypeStruct((DIFF_HEADS, tiles, rows, cols), F32)] + casts.out_shapes(),
        scratch_shapes=casts.scratch_shapes(),
        compiler_params=_params("arbitrary", "arbitrary"),
        name="bias_table",
    )(rel_bias, *cast_weights)
    return outs[0], outs[1:]


def _lambda(lam_ref, layer):
    lam = lam_ref[...]
    a = jnp.sum(lam[0:1, :] * lam[1:2, :], axis=-1, keepdims=True)
    b = jnp.sum(lam[2:3, :] * lam[3:4, :], axis=-1, keepdims=True)
    lam_init = 0.8 - 0.6 * math.exp(-0.3 * layer)
    return jnp.exp(a) - jnp.exp(b) + lam_init, lam_init


def _attn_prompt_body(q_ref, k_ref, vt_ref, bias_ref, lam_ref, nw_ref, mix_in_ref, *rest, layer, blk, casts):
    del mix_in_ref
    n = len(casts)
    cast_srcs, o_ref, cast_dsts = rest[:n], rest[n], rest[n + 1:2 * n + 1]
    acc_ref, s_ref, p_ref = rest[2 * n + 1:2 * n + 4]
    casts.run(pl.program_id(0) * pl.num_programs(1) + pl.program_id(1), cast_srcs, cast_dsts, rest[2 * n + 4:])
    qi = pl.program_id(1)
    qt = (q_ref[...] * (DIFF_D ** -0.5 * LOG2E)).T.astype(BF16)
    qth = (qt[:DIFF_D], qt[DIFF_D:])
    acc_ref[...] = jnp.zeros_like(acc_ref)
    strips = [slice(c * SUB, (c + 1) * SUB) for c in range(blk // SUB)]

    def fold(a):
        return a.reshape(SUB // 8, 8, blk)

    p_ref[...] = jnp.zeros_like(p_ref)

    def step(kj, stats):
        kb = k_ref[pl.ds(pl.multiple_of(kj * blk, blk), blk), :]
        vt_prev = vt_ref[jnp.maximum(kj - 1, 0)]
        t = jnp.minimum(qi - kj, 2)
        for half in range(2):
            s_ref[half] = _dot(kb[:, half * DIFF_D:(half + 1) * DIFF_D], qth[half]) + bias_ref[0, t]
        for half in range(2):
            acc_ref[half] += _dot(vt_prev, p_ref[half])
        out = []
        for half in range(2):
            m, l = stats[half]
            bm = jnp.max(fold(s_ref[half, strips[0], :]), axis=0)
            for c in strips[1:]:
                bm = jnp.maximum(bm, jnp.max(fold(s_ref[half, c, :]), axis=0))
            m_new = jnp.maximum(m, jnp.max(bm, axis=0, keepdims=True))
            alpha = jnp.exp2(m - m_new)
            acc_ref[half] = alpha * acc_ref[half]
            ls = jnp.zeros((8, blk), F32)
            for c in strips:
                p = jnp.exp2(s_ref[half, c, :] - m_new)
                ls = ls + jnp.sum(fold(p), axis=0)
                p_ref[half, c, :] = p.astype(BF16)
            l = alpha * l + jnp.sum(ls, axis=0, keepdims=True)
            out.append((m_new, l))
        return tuple(out)

    init = tuple((jnp.full((1, blk), -jnp.inf, F32), jnp.zeros((1, blk), F32)) for _ in range(2))
    stats = lax.fori_loop(0, qi + 1, step, init)

    lam, lam_init = _lambda(lam_ref, layer)
    vt_last = vt_ref[qi]
    o1 = acc_ref[0] + _dot(vt_last, p_ref[0])
    o2 = acc_ref[1] + _dot(vt_last, p_ref[1])
    ot = o1 / stats[0][1] - lam * (o2 / stats[1][1])
    scale = lax.rsqrt(jnp.mean(ot * ot, axis=0, keepdims=True) + EPS)
    ot = (ot * scale) * nw_ref[...] * (1.0 - lam_init)
    o_ref[...] = ot.T.astype(BF16)


def _attn_prompt(main, kvb, vt, bias, lam, nw_col, mix, cast_weights=(), cast_layers=(), *, layer, seq):
    blk = vt.shape[2]
    assert seq % blk == 0 and blk >= MAX_DISTANCE and blk % CHUNK == 0
    q_col0 = DQ_OFF // (2 * DIFF_D)
    o_col0 = GLA_V_W // DIFF_DV
    casts = _SideCasts(cast_weights, cast_layers, DIFF_HEADS * (seq // blk))
    anyspace = pl.BlockSpec(memory_space=pl.ANY)
    outs = pl.pallas_call(
        functools.partial(_attn_prompt_body, layer=layer, blk=blk, casts=casts),
        grid=(DIFF_HEADS, seq // blk),
        in_specs=[
            pl.BlockSpec((blk, 2 * DIFF_D), lambda h, i: (i, q_col0 + h)),
            pl.BlockSpec((seq, 2 * DIFF_D), lambda h, i: (0, h)),
            pl.BlockSpec((seq // blk, DIFF_DV, blk), lambda h, i: (0, h, 0)),
            pl.BlockSpec((1, 3, blk, blk), lambda h, i: (h, 0, 0, 0)),
            pl.BlockSpec((4, DIFF_D), lambda h, i: (0, 0)),
            pl.BlockSpec((DIFF_DV, 1), lambda h, i: (0, 0)),
            anyspace,
        ] + [anyspace] * len(casts),
        out_specs=[pl.BlockSpec((blk, DIFF_DV), lambda h, i: (i, o_col0 + h))] + [anyspace] * len(casts),
        out_shape=[jax.ShapeDtypeStruct(mix.shape, mix.dtype)] + casts.out_shapes(),
        scratch_shapes=[pltpu.VMEM((2, DIFF_DV, blk), F32), pltpu.VMEM((2, blk, blk), F32),
                        pltpu.VMEM((2, blk, blk), BF16)] + casts.scratch_shapes(),
        input_output_aliases={6: 0},
        compiler_params=_params("arbitrary", "arbitrary"),
        name="attn_prompt",
    )(main, kvb, vt, bias, lam, nw_col, mix, *cast_weights)
    return outs[0], outs[1:]


def _attn_sample_body(q_ref, kn_ref, vn_ref, kc_hbm, vc_hbm, bias_ref, lam_ref, nw_ref, mix_in_ref,
                      o_ref, kbuf, vbuf, sem, *, layer, past):
    del mix_in_ref
    b = pl.program_id(0)
    h = pl.program_id(1)
    n_h = DIFF_HEADS
    step = b * n_h + h
    n_steps = pl.num_programs(0) * n_h
    slot = step % SAMPLE_SLOTS

    def copies(cstep):
        cb, ch, cslot = cstep // n_h, cstep % n_h, cstep % SAMPLE_SLOTS
        return (pltpu.make_async_copy(kc_hbm.at[layer, cb, :, ch, :], kbuf.at[cslot], sem.at[0, cslot]),
                pltpu.make_async_copy(vc_hbm.at[layer, cb, :, ch, :], vbuf.at[cslot], sem.at[1, cslot]))

    def start(cstep):
        @pl.when(cstep < n_steps)
        def _():
            for c in copies(cstep):
                c.start()

    @pl.when(step == 0)
    def _():
        for ahead in range(SAMPLE_SLOTS - 1):
            start(step + ahead)

    start(step + SAMPLE_SLOTS - 1)

    qs = (q_ref[...] * (DIFF_D ** -0.5)).astype(BF16)
    kn = kn_ref[...]
    bias_c = bias_ref[0, 0, :, :past]
    bias_n = bias_ref[0, 0, :, past:]
    for c in copies(step):
        c.wait()
    ps = []
    for half in range(2):
        cols = slice(half * DIFF_D, (half + 1) * DIFF_D)
        sc = _dot_nt(qs[:, cols], kbuf[slot, :, cols].astype(BF16)) + bias_c
        sn = _dot_nt(qs[:, cols], kn[:, cols]) + bias_n
        m = jnp.maximum(jnp.max(sc, axis=-1, keepdims=True), jnp.max(sn, axis=-1, keepdims=True))
        pc = jnp.exp(sc - m)
        pn = jnp.exp(sn - m)
        inv = 1.0 / (jnp.sum(pc, axis=-1, keepdims=True) + jnp.sum(pn, axis=-1, keepdims=True))
        ps.append((pc * inv, pn * inv))
    lam, lam_init = _lambda(lam_ref, layer)
    wc = (ps[0][0] - lam * ps[1][0]).astype(BF16)
    wn = (ps[0][1] - lam * ps[1][1]).astype(BF16)
    o = _dot(wc, vbuf[slot].astype(BF16)) + _dot(wn, vn_ref[...])
    o_ref[...] = ((o * _rms_scale(o)) * nw_ref[...] * (1.0 - lam_init)).astype(BF16)


def _attn_sample(main, kvb, cache_k, cache_v, bias, lam, nw, mix, *, layer, row_off, batch, seq):
    past = cache_k.shape[2]
    assert row_off % seq == 0
    rb0 = row_off // seq
    q_col0 = DQ_OFF // (2 * DIFF_D)
    v_col0 = DIFF_QK_W // DIFF_DV
    o_col0 = GLA_V_W // DIFF_DV
    return pl.pallas_call(
        functools.partial(_attn_sample_body, layer=layer, past=past),
        grid=(batch, DIFF_HEADS),
        in_specs=[
            pl.BlockSpec((seq, 2 * DIFF_D), lambda b, h: (rb0 + b, q_col0 + h)),
            pl.BlockSpec((seq, 2 * DIFF_D), lambda b, h: (rb0 + b, h)),
            pl.BlockSpec((seq, DIFF_DV), lambda b, h: (rb0 + b, v_col0 + h)),
            pl.BlockSpec(memory_space=pl.ANY),
            pl.BlockSpec(memory_space=pl.ANY),
            pl.BlockSpec((1, 1, seq, past + seq), lambda b, h: (h, 0, 0, 0)),
            pl.BlockSpec((4, DIFF_D), lambda b, h: (0, 0)),
            pl.BlockSpec((1, DIFF_DV), lambda b, h: (0, 0)),
            pl.BlockSpec(memory_space=pl.ANY),
        ],
        out_specs=pl.BlockSpec((seq, DIFF_DV), lambda b, h: (rb0 + b, o_col0 + h)),
        out_shape=jax.ShapeDtypeStruct(mix.shape, mix.dtype),
        scratch_shapes=[
            pltpu.VMEM((SAMPLE_SLOTS, past, 2 * DIFF_D), F32),
            pltpu.VMEM((SAMPLE_SLOTS, past, DIFF_DV), F32),
            pltpu.SemaphoreType.DMA((2, SAMPLE_SLOTS)),
        ],
        input_output_aliases={8: 0},
        compiler_params=_params("arbitrary", "arbitrary"),
        name="attn_sample",
    )(main, kvb, kvb, cache_k, cache_v, bias, lam, nw, mix)


def _outproj_body(o_ref, w_ref, x_ref, y_ref):
    y_ref[...] = x_ref[...] + _dot(o_ref[...], w_ref[...])


def _outproj(mix, w, x):
    n, d = x.shape
    tm = _row_tile(n)
    return pl.pallas_call(
        _outproj_body,
        grid=(n // tm,),
        in_specs=[
            pl.BlockSpec((tm, MIX_WIDTH), lambda i: (i, 0)),
            pl.BlockSpec((MIX_WIDTH, d), lambda i: (0, 0)),
            pl.BlockSpec((tm, d), lambda i: (i, 0)),
        ],
        out_specs=pl.BlockSpec((tm, d), lambda i: (i, 0)),
        out_shape=jax.ShapeDtypeStruct((n, d), F32),
        compiler_params=_params("parallel"),
        name="outproj",
    )(mix, w, x)


def kernel(x_prompt, x_sample, cache_k, cache_v, state_gla, ffn1_norm, ffn1_w_gate, ffn1_w_up, ffn1_w_down,
           mix_norm, w_in, gla_w_gk, gla_b_gk, gla_norm, diff_lambda, diff_norm, w_out,
           ffn2_norm, ffn2_w_gate, ffn2_w_up, ffn2_w_down, rel_bias, final_norm):
    pb, ps, d = x_prompt.shape
    sb, ss, _ = x_sample.shape
    depth = w_in.shape[0]
    past = cache_k.shape[2]
    assert pb == 1
    n_p, n_s = pb * ps, sb * ss
    xs = [x_prompt.reshape(n_p, d), x_sample.reshape(n_s, d)]

    blk = _row_tile(n_p + n_s)
    ffn1_w = (ffn1_w_gate, ffn1_w_up, ffn1_w_down)
    ffn2_w = (ffn2_w_gate, ffn2_w_up, ffn2_w_down)
    bias_p, first = _bias_table(rel_bias, ffn1_w, [0] * len(ffn1_w), tiles=3, rows=blk, cols=blk, q_off=0, k_off=0,
                                k_step=-blk, key_major=True, scale=LOG2E)
    bias_s, _ = _bias_table(rel_bias, tiles=1, rows=ss, cols=past + ss, q_off=past, k_off=0, k_step=0)
    zero_state = jnp.zeros((1, pb, GLA_HEADS, GLA_DK, GLA_DV), F32)
    row = lambda a: a.reshape(1, -1)

    w_main, w_gz = _cast_w_in(jnp.swapaxes(w_in, 1, 2))
    f1 = {0: list(first)}
    later = [(w, l) for l in range(depth) for w in ((ffn1_w if l else ()) + ffn2_w + (w_out,))]
    f2, w_o = {}, {}

    kv_out = [jnp.zeros((depth, pb, ps, DIFF_HEADS, DIFF_DV), F32) for _ in range(2)]
    kv_out += [jnp.zeros((depth, sb, ss, DIFF_HEADS, DIFF_DV), F32) for _ in range(2)]
    p_states, s_states = [], []
    mix = jnp.zeros((n_p + n_s, MIX_WIDTH), BF16)
    for l in range(depth):
        w_gk = jnp.pad(gla_w_gk[l], ((0, LANE - GLA_GK_RANK), (0, 0))).astype(BF16)

        x, = _ffn(xs, row(ffn1_norm[l]), *f1[l], row(final_norm), False, [n_p + n_s])
        (main, kvb, vt, g), kv_out = _inproj(x, row(mix_norm[l]), w_main, w_gz, w_gk, row(gla_b_gk[l]), kv_out,
                                             layer=l, split=n_p // blk, seq_s=ss)

        mix, sp = _gla(main, g, zero_state, row(gla_norm[l]), mix,
                       layer=0, row_off=0, batch=pb, seq=ps, chunk=CHUNK, rows=min(GLA_ROWS, ps))
        mix, s_s = _gla(main, g, state_gla, row(gla_norm[l]), mix,
                        layer=l, row_off=n_p, batch=sb, seq=ss, chunk=ss, rows=ss)
        mix, cast = _attn_prompt(main, kvb, vt, bias_p, diff_lambda[l], diff_norm[l].reshape(-1, 1), mix,
                                 [w for w, _ in later] if l == 0 else [], [k for _, k in later] if l == 0 else [],
                                 layer=l, seq=ps)
        cast = list(cast)
        for k in range(depth if l == 0 else 0):
            if k:
                f1[k] = [cast.pop(0) for _ in ffn1_w]
            f2[k] = [cast.pop(0) for _ in ffn2_w]
            w_o[k] = cast.pop(0)
        mix = _attn_sample(main, kvb, cache_k, cache_v, bias_s, diff_lambda[l], row(diff_norm[l]), mix,
                           layer=l, row_off=n_p, batch=sb, seq=ss)

        x = _outproj(mix, w_o[l], x)
        last = l == depth - 1
        xs = _ffn([x], row(ffn2_norm[l]), *f2[l], row(final_norm), last, [n_p, n_s] if last else [n_p + n_s])
        p_states.append(sp)
        s_states.append(s_s)

    prompt_k, prompt_v, sample_k, sample_v = kv_out
    return (xs[0].reshape(pb, ps, d), xs[1].reshape(sb, ss, d), prompt_k, prompt_v, jnp.stack(p_states),
            sample_k, sample_v, jnp.stack(s_states))
```

```python
import functools
import math

import jax
import jax.numpy as jnp
from jax import lax
from jax.experimental import pallas as pl
from jax.experimental.pallas import tpu as pltpu

F32 = jnp.float32
BF16 = jnp.bfloat16

EPS = 1e-6
CHUNK = 64
GLA_HEADS = 4
GLA_DK = 128
GLA_DV = 256
GLA_GK_RANK = 16
GLA_GATE_NORMALIZER = 16.0
DIFF_HEADS = 4
DIFF_D = 128
DIFF_DV = 2 * DIFF_D
N_BUCKETS = 32
MAX_DISTANCE = 128

GLA_K_W = GLA_HEADS * GLA_DK
GLA_V_W = GLA_HEADS * GLA_DV
DIFF_QK_W = DIFF_HEADS * 2 * DIFF_D
DIFF_V_W = DIFF_HEADS * DIFF_DV
MIX_WIDTH = GLA_V_W + DIFF_V_W
GZ_OFF = 2 * GLA_K_W + 2 * GLA_V_W
MAIN_W = GZ_OFF + DIFF_QK_W
DQ_OFF = GZ_OFF

LANE = 128
BF16_ROWS = 16
LOG2E = math.log2(math.e)
SUB = 64
VMEM_LIMIT = 56 * 1024 * 1024

ROW_TILE = 512
FFN_ROWS = 1024
FF_TILE = 512
IN_TILE = 1024
GLA_ROWS = 512
SAMPLE_SLOTS = 3


def _dot(a, b):
    return jnp.dot(a, b, preferred_element_type=F32)


def _dot_nt(a, b):
    return lax.dot_general(a, b, (((1,), (1,)), ((), ())), preferred_element_type=F32)


def _dot_tn(a, b):
    return lax.dot_general(a, b, (((0,), (0,)), ((), ())), preferred_element_type=F32)


def _params(*sem):
    return pltpu.CompilerParams(dimension_semantics=sem, vmem_limit_bytes=VMEM_LIMIT)


def _row_tile(n):
    t = ROW_TILE
    while n % t:
        t //= 2
    return t


def _rms_scale(x):
    return lax.rsqrt(jnp.mean(x * x, axis=-1, keepdims=True) + EPS)


CAST_ROWS = 256


def _cast_rows(rows, steps):
    return -(-rows // (BF16_ROWS * steps)) * BF16_ROWS


class _SideCasts:
    def __init__(self, weights, layers, steps):
        self.layers = layers
        self.steps = steps
        self.shapes = [w.shape[1:] for w in weights]
        self.rows = [_cast_rows(s[0], steps) for s in self.shapes]
        assert all(s[0] >= r and s[0] % BF16_ROWS == 0 for s, r in zip(self.shapes, self.rows))

    def __len__(self):
        return len(self.shapes)

    def out_shapes(self):
        return [jax.ShapeDtypeStruct(s, BF16) for s in self.shapes]

    def scratch_shapes(self):
        n = len(self)
        if n == 0:
            return []
        return ([pltpu.VMEM((2, r, s[1]), F32) for s, r in zip(self.shapes, self.rows)]
                + [pltpu.VMEM((r, s[1]), BF16) for s, r in zip(self.shapes, self.rows)]
                + [pltpu.SemaphoreType.DMA((n, 2)), pltpu.SemaphoreType.DMA((n,))])

    def run(self, step, srcs, dsts, scratch):
        n = len(self)
        if n == 0:
            return
        in_bufs, out_bufs, in_sem, out_sem = scratch[:n], scratch[n:2 * n], scratch[2 * n], scratch[2 * n + 1]

        def start_row(k, t):
            r, total = self.rows[k], self.shapes[k][0]
            return pl.multiple_of(jnp.minimum(t * r, total - r), BF16_ROWS)

        def fetch(k, t):
            return pltpu.make_async_copy(srcs[k].at[self.layers[k], pl.ds(start_row(k, t), self.rows[k]), :],
                                         in_bufs[k].at[t % 2], in_sem.at[k, t % 2])

        def store(k, t):
            return pltpu.make_async_copy(out_bufs[k], dsts[k].at[pl.ds(start_row(k, t), self.rows[k]), :],
                                         out_sem.at[k])

        @pl.when(step == 0)
        def _():
            for k in range(n):
                fetch(k, step).start()

        @pl.when(step + 1 < self.steps)
        def _():
            for k in range(n):
                fetch(k, step + 1).start()

        for k in range(n):
            fetch(k, step).wait()

            @pl.when(step > 0)
            def _():
                store(k, step - 1).wait()

            out_bufs[k][...] = in_bufs[k][step % 2].astype(BF16)
            store(k, step).start()

        @pl.when(step == self.steps - 1)
        def _():
            for k in range(n):
                store(k, step).wait()


GZ_BLK = GZ_OFF // CAST_ROWS
assert GZ_OFF % CAST_ROWS == 0 and GLA_GK_RANK % BF16_ROWS == 0 and GLA_GK_RANK < CAST_ROWS


def _cast_w_in_body(a_ref, b_ref, main_ref, gz_ref):
    k = pl.program_id(1)

    @pl.when(k < GZ_BLK)
    def _():
        main_ref[0] = a_ref[0].astype(BF16)

    @pl.when(k >= GZ_BLK)
    def _():
        main_ref[0, :CAST_ROWS - GLA_GK_RANK] = a_ref[0, GLA_GK_RANK:].astype(BF16)
        main_ref[0, CAST_ROWS - GLA_GK_RANK:] = b_ref[0, :GLA_GK_RANK].astype(BF16)

    @pl.when(k == GZ_BLK)
    def _():
        gz_ref[0, :GLA_GK_RANK] = a_ref[0, :GLA_GK_RANK].astype(BF16)
        gz_ref[0, GLA_GK_RANK:] = jnp.zeros((LANE - GLA_GK_RANK, a_ref.shape[2]), BF16)


def _cast_w_in(w_in_t):
    depth, c, r = w_in_t.shape
    wide = c - GLA_GK_RANK
    last = pl.cdiv(c, CAST_ROWS) - 1
    return pl.pallas_call(
        _cast_w_in_body,
        grid=(depth, wide // CAST_ROWS),
        in_specs=[pl.BlockSpec((1, CAST_ROWS, r), lambda l, k: (l, k, 0)),
                  pl.BlockSpec((1, CAST_ROWS, r), lambda l, k: (l, jnp.clip(k + 1, GZ_BLK + 1, last), 0))],
        out_specs=[pl.BlockSpec((1, CAST_ROWS, r), lambda l, k: (l, k, 0)),
                   pl.BlockSpec((1, LANE, r), lambda l, k: (l, 0, 0))],
        out_shape=[jax.ShapeDtypeStruct((depth, wide, r), BF16),
                   jax.ShapeDtypeStruct((depth, LANE, r), BF16)],
        compiler_params=_params("parallel", "arbitrary"),
        name="cast_w_in",
    )(w_in_t, w_in_t)


def _ffn_body(*refs, final, n_in, n_out, split):
    x_refs = refs[:n_in]
    nw_ref, wg_ref, wu_ref, wd_ref, fw_ref = refs[n_in:n_in + 5]
    o_refs = refs[n_in + 5:n_in + 5 + n_out]
    scratch = refs[n_in + 5 + n_out:]
    xn_ref = scratch[0]
    i = pl.program_id(0)
    j = pl.program_id(1)
    n_i = pl.num_programs(0)
    tm = xn_ref.shape[0]

    acc_ref = o_refs[0] if n_out == 1 else scratch[1]

    if n_in == 2:
        x_buf, x_sem = scratch[-2:]

        def fetch(t, act):
            @pl.when(t < split)
            def _():
                act(pltpu.make_async_copy(x_refs[0].at[pl.ds(t * tm, tm), :], x_buf.at[t % 2], x_sem.at[t % 2]))

            @pl.when(t >= split)
            def _():
                act(pltpu.make_async_copy(x_refs[1].at[pl.ds((t - split) * tm, tm), :], x_buf.at[t % 2],
                                          x_sem.at[t % 2]))

        @pl.when(j == 0)
        def _():
            @pl.when(i == 0)
            def _():
                fetch(i, lambda c: c.start())

            @pl.when(i + 1 < n_i)
            def _():
                fetch(i + 1, lambda c: c.start())

            fetch(i, lambda c: c.wait())

    def load_x():
        return x_refs[0][...] if n_in == 1 else x_buf[i % 2]

    @pl.when(j == 0)
    def _():
        x = load_x()
        xn_ref[...] = ((x * _rms_scale(x)) * nw_ref[...]).astype(BF16)
        acc_ref[...] = jnp.zeros_like(acc_ref)

    xn = xn_ref[...]
    h = _dot(xn, wg_ref[...])
    u = _dot(xn, wu_ref[...])
    a = (h * jax.nn.sigmoid(h) * u).astype(BF16)
    acc_ref[...] += _dot(a, wd_ref[...])

    @pl.when(j == pl.num_programs(1) - 1)
    def _():
        y = load_x() + 0.5 * acc_ref[...]
        if final:
            y = (y * _rms_scale(y)) * fw_ref[...]
        if n_out == 1:
            o_refs[0][...] = y
        else:
            @pl.when(i < split)
            def _():
                o_refs[0][...] = y

            @pl.when(i >= split)
            def _():
                o_refs[1][...] = y


def _ffn(xs, nw, wg, wu, wd, fw, final, out_rows):
    d = xs[0].shape[1]
    n = sum(x.shape[0] for x in xs)
    assert n == sum(out_rows)
    f = wg.shape[1]
    tm = FFN_ROWS if len(out_rows) == 1 and n % FFN_ROWS == 0 else _row_tile(n)
    first = xs[0].shape[0] if len(xs) == 2 else out_rows[0]
    if first % tm:
        tm = _row_tile(n)
    assert first % tm == 0
    split = first // tm
    anyspace = pl.BlockSpec(memory_space=pl.ANY)
    tile = pl.BlockSpec((tm, d), lambda i, j: (i, 0))
    outs = pl.pallas_call(
        functools.partial(_ffn_body, final=final, n_in=len(xs), n_out=len(out_rows), split=split),
        grid=(n // tm, f // FF_TILE),
        in_specs=([anyspace] * 2 if len(xs) == 2 else [tile]) + [
            pl.BlockSpec((1, d), lambda i, j: (0, 0)),
            pl.BlockSpec((d, FF_TILE), lambda i, j: (0, j)),
            pl.BlockSpec((d, FF_TILE), lambda i, j: (0, j)),
            pl.BlockSpec((FF_TILE, d), lambda i, j: (j, 0)),
            pl.BlockSpec((1, d), lambda i, j: (0, 0)),
        ],
        out_specs=([pl.BlockSpec((tm, d), lambda i, j: (jnp.minimum(i, split - 1), 0)),
                    pl.BlockSpec((tm, d), lambda i, j: (jnp.maximum(i - split, 0), 0))]
                   if len(out_rows) == 2 else [tile]),
        out_shape=[jax.ShapeDtypeStruct((r, d), F32) for r in out_rows],
        scratch_shapes=([pltpu.VMEM((tm, d), BF16)]
                        + ([pltpu.VMEM((tm, d), F32)] if len(out_rows) == 2 else [])
                        + ([pltpu.VMEM((2, tm, d), F32), pltpu.SemaphoreType.DMA((2,))] if len(xs) == 2 else [])),
        compiler_params=_params("parallel" if len(xs) == len(out_rows) == 1 else "arbitrary", "arbitrary"),
        name="ffn",
    )(*xs, nw, wg, wu, wd, fw)
    return outs


N_MAIN_BLK = MAIN_W // IN_TILE
N_K_BLK = DIFF_QK_W // IN_TILE
N_V_BLK = DIFF_V_W // IN_TILE


assert N_K_BLK == 1 and N_V_BLK == 1


def _inproj_body(x_ref, nw_ref, w_ref, wgz_ref, wgk_ref, bgk_ref, *rest, layer, split, seq_s):
    main_ref, kvb_ref, vt_ref, g_ref, pk_ref, pv_ref, sk_ref, sv_ref, xn_ref, y_ref, sem = rest[-11:]
    i = pl.program_id(0)
    j = pl.program_id(1)
    n_i = pl.num_programs(0)
    tm = x_ref.shape[0]

    @pl.when(j == 0)
    def _():
        x = x_ref[...]
        xn = ((x * _rms_scale(x)) * nw_ref[...]).astype(BF16)
        xn_ref[...] = xn
        gz = _dot_nt(xn, wgz_ref[...])
        z = _dot(gz.astype(BF16), wgk_ref[...]) + bgk_ref[...]
        g_ref[...] = (jnp.minimum(z, 0.0) - jnp.log1p(jnp.exp(-jnp.abs(z)))) * (1.0 / GLA_GATE_NORMALIZER)

    y = _dot_nt(xn_ref[...], w_ref[...])

    @pl.when(j < N_MAIN_BLK)
    def _():
        main_ref[...] = y

    def row_copies(slot, tile, dst_p, dst_s, act):
        heads = [slice(h * DIFF_DV, (h + 1) * DIFF_DV) for h in range(DIFF_HEADS)]

        @pl.when(tile < split)
        def _():
            for h, cols in enumerate(heads):
                act(pltpu.make_async_copy(y_ref.at[slot, :, cols],
                                          dst_p.at[layer, 0, pl.ds(tile * tm, tm), h, :], sem.at[slot]))

        @pl.when(tile >= split)
        def _():
            for b in range(tm // seq_s):
                for h, cols in enumerate(heads):
                    act(pltpu.make_async_copy(y_ref.at[slot, b * seq_s:(b + 1) * seq_s, cols],
                                              dst_s.at[layer, (tile - split) * (tm // seq_s) + b, :, h, :],
                                              sem.at[slot]))

    def kv_step(slot, dst_p, dst_s):
        @pl.when(i > 0)
        def _():
            row_copies(slot, i - 1, dst_p, dst_s, lambda c: c.wait())

        y_ref[slot] = y
        kvb_ref[...] = y.astype(BF16)
        if slot == 1:
            vt_ref[0] = y.T.astype(BF16)
        row_copies(slot, i, dst_p, dst_s, lambda c: c.start())

        @pl.when(i == n_i - 1)
        def _():
            row_copies(slot, i, dst_p, dst_s, lambda c: c.wait())

    @pl.when(j == N_MAIN_BLK)
    def _():
        kv_step(0, pk_ref, sk_ref)

    @pl.when(j == N_MAIN_BLK + 1)
    def _():
        kv_step(1, pv_ref, sv_ref)


def _inproj(x, nw, w_main, w_gz, w_gk, b_gk, kv_out, *, layer, split, seq_s):
    n, d = x.shape
    tm = _row_tile(n)
    assert tm % seq_s == 0 and kv_out[0].shape[1] == 1
    nj = N_MAIN_BLK + 2
    k0 = N_MAIN_BLK
    anyspace = pl.BlockSpec(memory_space=pl.ANY)
    outs = pl.pallas_call(
        functools.partial(_inproj_body, layer=layer, split=split, seq_s=seq_s),
        grid=(n // tm, nj),
        in_specs=[
            pl.BlockSpec((tm, d), lambda i, j: (i, 0)),
            pl.BlockSpec((1, d), lambda i, j: (0, 0)),
            pl.BlockSpec((None, IN_TILE, d), lambda i, j: (layer, j, 0)),
            pl.BlockSpec((None, LANE, d), lambda i, j: (layer, 0, 0)),
            pl.BlockSpec((LANE, GLA_K_W), lambda i, j: (0, 0)),
            pl.BlockSpec((1, GLA_K_W), lambda i, j: (0, 0)),
        ] + [anyspace] * len(kv_out),
        out_specs=[
            pl.BlockSpec((tm, IN_TILE), lambda i, j: (i, jnp.minimum(j, k0 - 1))),
            pl.BlockSpec((tm, IN_TILE), lambda i, j: (i, jnp.clip(j - k0, 0, 1))),
            pl.BlockSpec((1, IN_TILE, tm), lambda i, j: (i, 0, 0)),
            pl.BlockSpec((tm, GLA_K_W), lambda i, j: (i, 0)),
            anyspace, anyspace, anyspace, anyspace,
        ],
        out_shape=[
            jax.ShapeDtypeStruct((n, MAIN_W), F32),
            jax.ShapeDtypeStruct((n, DIFF_QK_W + DIFF_V_W), BF16),
            jax.ShapeDtypeStruct((n // tm, DIFF_V_W, tm), BF16),
            jax.ShapeDtypeStruct((n, GLA_K_W), F32),
        ] + [jax.ShapeDtypeStruct(a.shape, a.dtype) for a in kv_out],
        scratch_shapes=[pltpu.VMEM((tm, d), BF16), pltpu.VMEM((2, tm, IN_TILE), F32),
                        pltpu.SemaphoreType.DMA((2,))],
        input_output_aliases={6 + k: 4 + k for k in range(len(kv_out))},
        compiler_params=_params("arbitrary", "arbitrary"),
        name="inproj",
    )(x, nw, w_main, w_gz, w_gk, b_gk, *kv_out)
    return outs[:4], outs[4:]


def _split3(a):
    hi = a.astype(BF16)
    r = a - hi.astype(F32)
    mid = r.astype(BF16)
    lo = (r - mid.astype(F32)).astype(BF16)
    return hi, mid, lo


def _gla_body(q_ref, k_ref, v_ref, r_ref, g_ref, s0_ref, nw_ref, mix_in_ref, o_ref, s_ref, st_ref, *, chunk):
    del mix_in_ref
    t = pl.program_id(1)
    rows = q_ref.shape[0]
    shift = chunk.bit_length() - 1
    assert chunk == 1 << shift

    @pl.when(t == 0)
    def _():
        for h in range(GLA_HEADS):
            st_ref[h] = s0_ref[0, h].T

    ri = lax.broadcasted_iota(jnp.int32, (rows, rows), 0)
    ci = lax.broadcasted_iota(jnp.int32, (rows, rows), 1)
    causal = jnp.logical_and((ri >> shift) == (ci >> shift), ci <= ri)
    tri = causal.astype(BF16)

    g_hi, g_mid, g_lo = _split3(g_ref[...])
    b_all = _dot(tri, g_hi) + _dot(tri, g_mid) + _dot(tri, g_lo)

    for h in range(GLA_HEADS):
        kc = slice(h * GLA_DK, (h + 1) * GLA_DK)
        vc = slice(h * GLA_DV, (h + 1) * GLA_DV)
        b = b_all[:, kc]
        k = k_ref[:, kc]
        v = v_ref[:, vc].astype(BF16)
        qe = (q_ref[:, kc] * (GLA_DK ** -0.5) * jnp.exp(b)).astype(BF16)
        ke = (k * jnp.exp(-b)).astype(BF16)
        a = jnp.where(causal, _dot_nt(qe, ke), 0.0).astype(BF16)
        o_in = _dot(a, v)
        for c in range(rows // chunk):
            sl = slice(c * chunk, (c + 1) * chunk)
            b_last = b[(c + 1) * chunk - 1:(c + 1) * chunk, :]
            kd = (k[sl] * jnp.exp(b_last - b[sl])).astype(BF16)
            st = st_ref[h]
            o = _dot_nt(qe[sl], st.astype(BF16)) + o_in[sl]
            st_ref[h] = st * jnp.exp(b_last) + _dot_tn(v[sl], kd)
            o = (o * _rms_scale(o)) * nw_ref[...]
            r = r_ref[sl, vc]
            o_ref[sl, vc] = (o * (r * jax.nn.sigmoid(r))).astype(BF16)

    @pl.when(t == pl.num_programs(1) - 1)
    def _():
        for h in range(GLA_HEADS):
            s_ref[0, h] = st_ref[h].T


def _gla(main, g, s0, nw, mix, *, layer, row_off, batch, seq, chunk, rows):
    assert seq % rows == 0 and rows % chunk == 0 and row_off % rows == 0
    nt = seq // rows
    rb0 = row_off // rows

    def rowblk(b, t):
        return rb0 + b * nt + t

    state_spec = pl.BlockSpec((1, GLA_HEADS, GLA_DK, GLA_DV), lambda b, t: (b, 0, 0, 0))
    state_in = pl.BlockSpec((None, 1, GLA_HEADS, GLA_DK, GLA_DV), lambda b, t: (layer, b, 0, 0, 0))
    out, s = pl.pallas_call(
        functools.partial(_gla_body, chunk=chunk),
        grid=(batch, nt),
        in_specs=[
            pl.BlockSpec((rows, GLA_K_W), lambda b, t: (rowblk(b, t), 0)),
            pl.BlockSpec((rows, GLA_K_W), lambda b, t: (rowblk(b, t), 1)),
            pl.BlockSpec((rows, GLA_V_W), lambda b, t: (rowblk(b, t), 2 * GLA_K_W // GLA_V_W)),
            pl.BlockSpec((rows, GLA_V_W), lambda b, t: (rowblk(b, t), 2 * GLA_K_W // GLA_V_W + 1)),
            pl.BlockSpec((rows, GLA_K_W), lambda b, t: (rowblk(b, t), 0)),
            state_in,
            pl.BlockSpec((1, GLA_DV), lambda b, t: (0, 0)),
            pl.BlockSpec(memory_space=pl.ANY),
        ],
        out_specs=[
            pl.BlockSpec((rows, GLA_V_W), lambda b, t: (rowblk(b, t), 0)),
            state_spec,
        ],
        out_shape=[
            jax.ShapeDtypeStruct(mix.shape, mix.dtype),
            jax.ShapeDtypeStruct((batch, GLA_HEADS, GLA_DK, GLA_DV), F32),
        ],
        scratch_shapes=[pltpu.VMEM((GLA_HEADS, GLA_DV, GLA_DK), F32)],
        input_output_aliases={7: 0},
        compiler_params=_params("parallel", "arbitrary"),
        name="gla",
    )(main, main, main, main, g, s0, nw, mix)
    return out, s


def _bucket_thresholds():
    half = N_BUCKETS // 2
    m = half // 2
    e = half - m
    thr = []
    for kk in range(1, e):
        n = m
        while n ** e * m ** kk < m ** e * MAX_DISTANCE ** kk:
            n += 1
        thr.append(n)
    return tuple(thr)


_BUCKET_THR = _bucket_thresholds()


def _t5_bucket(rel):
    half = N_BUCKETS // 2
    max_exact = half // 2
    n = jnp.abs(rel)
    large = jnp.full(rel.shape, max_exact, jnp.int32)
    for thr in _BUCKET_THR:
        large = large + (n >= thr).astype(jnp.int32)
    return jnp.where(rel > 0, half, 0) + jnp.where(n < max_exact, n, large)


def _bias_body(rb_ref, *rest, q_off, k_off, k_step, key_major, scale, casts):
    n = len(casts)
    o_ref = rest[n]
    h = pl.program_id(0)
    t = pl.program_id(1)
    casts.run(h * pl.num_programs(1) + t, rest[:n], rest[n + 1:2 * n + 1], rest[2 * n + 1:])
    shape = o_ref.shape[2:]
    qpos = q_off + lax.broadcasted_iota(jnp.int32, shape, 1 if key_major else 0)
    kpos = k_off + t * k_step + lax.broadcasted_iota(jnp.int32, shape, 0 if key_major else 1)
    bucket = _t5_bucket(kpos - qpos)
    acc = jnp.zeros(shape, F32)
    for bkt in range(N_BUCKETS):
        acc = jnp.where(bucket == bkt, rb_ref[bkt, h], acc)
    visible = (kpos >> 6) <= (qpos >> 6)
    o_ref[0, 0] = jnp.where(visible, acc * scale, -jnp.inf)


assert CHUNK == 64


def _bias_table(rel_bias, cast_weights=(), cast_layers=(), *, tiles, rows, cols, q_off, k_off, k_step,
                key_major=False, scale=1.0):
    casts = _SideCasts(cast_weights, cast_layers, DIFF_HEADS * tiles)
    anyspace = pl.BlockSpec(memory_space=pl.ANY)
    outs = pl.pallas_call(
        functools.partial(_bias_body, q_off=q_off, k_off=k_off, k_step=k_step, key_major=key_major,
                          scale=scale, casts=casts),
        grid=(DIFF_HEADS, tiles),
        in_specs=[pl.BlockSpec(memory_space=pltpu.SMEM)] + [anyspace] * len(casts),
        out_specs=[pl.BlockSpec((1, 1, rows, cols), lambda h, t: (h, t, 0, 0))] + [anyspace] * len(casts),
        out_shape=[jax.ShapeDtypeStruct((DIFF_HEADS, tiles, rows, cols), F32)] + casts.out_shapes(),
        scratch_shapes=casts.scratch_shapes(),
        compiler_params=_params("arbitrary", "arbitrary"),
        name="bias_table",
    )(rel_bias, *cast_weights)
    return outs[0], outs[1:]


def _lambda(lam_ref, layer):
    lam = lam_ref[...]
    a = jnp.sum(lam[0:1, :] * lam[1:2, :], axis=-1, keepdims=True)
    b = jnp.sum(lam[2:3, :] * lam[3:4, :], axis=-1, keepdims=True)
    lam_init = 0.8 - 0.6 * math.exp(-0.3 * layer)
    return jnp.exp(a) - jnp.exp(b) + lam_init, lam_init


def _attn_prompt_body(q_ref, k_ref, vt_ref, bias_ref, lam_ref, nw_ref, mix_in_ref, *rest, layer, blk, casts):
    del mix_in_ref
    n = len(casts)
    cast_srcs, o_ref, cast_dsts = rest[:n], rest[n], rest[n + 1:2 * n + 1]
    acc_ref, s_ref, p_ref = rest[2 * n + 1:2 * n + 4]
    casts.run(pl.program_id(0) * pl.num_programs(1) + pl.program_id(1), cast_srcs, cast_dsts, rest[2 * n + 4:])
    qi = pl.program_id(1)
    qt = (q_ref[...] * (DIFF_D ** -0.5 * LOG2E)).T.astype(BF16)
    qth = (qt[:DIFF_D], qt[DIFF_D:])
    acc_ref[...] = jnp.zeros_like(acc_ref)
    strips = [slice(c * SUB, (c + 1) * SUB) for c in range(blk // SUB)]

    def fold(a):
        return a.reshape(SUB // 8, 8, blk)

    p_ref[...] = jnp.zeros_like(p_ref)

    def step(kj, stats):
        kb = k_ref[pl.ds(pl.multiple_of(kj * blk, blk), blk), :]
        vt_prev = vt_ref[jnp.maximum(kj - 1, 0)]
        t = jnp.minimum(qi - kj, 2)
        for half in range(2):
            s_ref[half] = _dot(kb[:, half * DIFF_D:(half + 1) * DIFF_D], qth[half]) + bias_ref[0, t]
        for half in range(2):
            acc_ref[half] += _dot(vt_prev, p_ref[half])
        out = []
        for half in range(2):
            m, l = stats[half]
            bm = jnp.max(fold(s_ref[half, strips[0], :]), axis=0)
            for c in strips[1:]:
                bm = jnp.maximum(bm, jnp.max(fold(s_ref[half, c, :]), axis=0))
            m_new = jnp.maximum(m, jnp.max(bm, axis=0, keepdims=True))
            alpha = jnp.exp2(m - m_new)
            acc_ref[half] = alpha * acc_ref[half]
            ls = jnp.zeros((8, blk), F32)
            for c in strips:
                p = jnp.exp2(s_ref[half, c, :] - m_new)
                ls = ls + jnp.sum(fold(p), axis=0)
                p_ref[half, c, :] = p.astype(BF16)
            l = alpha * l + jnp.sum(ls, axis=0, keepdims=True)
            out.append((m_new, l))
        return tuple(out)

    init = tuple((jnp.full((1, blk), -jnp.inf, F32), jnp.zeros((1, blk), F32)) for _ in range(2))
    stats = lax.fori_loop(0, qi + 1, step, init)

    lam, lam_init = _lambda(lam_ref, layer)
    vt_last = vt_ref[qi]
    o1 = acc_ref[0] + _dot(vt_last, p_ref[0])
    o2 = acc_ref[1] + _dot(vt_last, p_ref[1])
    ot = o1 / stats[0][1] - lam * (o2 / stats[1][1])
    scale = lax.rsqrt(jnp.mean(ot * ot, axis=0, keepdims=True) + EPS)
    ot = (ot * scale) * nw_ref[...] * (1.0 - lam_init)
    o_ref[...] = ot.T.astype(BF16)


def _attn_prompt(main, kvb, vt, bias, lam, nw_col, mix, cast_weights=(), cast_layers=(), *, layer, seq):
    blk = vt.shape[2]
    assert seq % blk == 0 and blk >= MAX_DISTANCE and blk % CHUNK == 0
    q_col0 = DQ_OFF // (2 * DIFF_D)
    o_col0 = GLA_V_W // DIFF_DV
    casts = _SideCasts(cast_weights, cast_layers, DIFF_HEADS * (seq // blk))
    anyspace = pl.BlockSpec(memory_space=pl.ANY)
    outs = pl.pallas_call(
        functools.partial(_attn_prompt_body, layer=layer, blk=blk, casts=casts),
        grid=(DIFF_HEADS, seq // blk),
        in_specs=[
            pl.BlockSpec((blk, 2 * DIFF_D), lambda h, i: (i, q_col0 + h)),
            pl.BlockSpec((seq, 2 * DIFF_D), lambda h, i: (0, h)),
            pl.BlockSpec((seq // blk, DIFF_DV, blk), lambda h, i: (0, h, 0)),
            pl.BlockSpec((1, 3, blk, blk), lambda h, i: (h, 0, 0, 0)),
            pl.BlockSpec((4, DIFF_D), lambda h, i: (0, 0)),
            pl.BlockSpec((DIFF_DV, 1), lambda h, i: (0, 0)),
            anyspace,
        ] + [anyspace] * len(casts),
        out_specs=[pl.BlockSpec((blk, DIFF_DV), lambda h, i: (i, o_col0 + h))] + [anyspace] * len(casts),
        out_shape=[jax.ShapeDtypeStruct(mix.shape, mix.dtype)] + casts.out_shapes(),
        scratch_shapes=[pltpu.VMEM((2, DIFF_DV, blk), F32), pltpu.VMEM((2, blk, blk), F32),
                        pltpu.VMEM((2, blk, blk), BF16)] + casts.scratch_shapes(),
        input_output_aliases={6: 0},
        compiler_params=_params("arbitrary", "arbitrary"),
        name="attn_prompt",
    )(main, kvb, vt, bias, lam, nw_col, mix, *cast_weights)
    return outs[0], outs[1:]


def _attn_sample_body(q_ref, kn_ref, vn_ref, kc_hbm, vc_hbm, bias_ref, lam_ref, nw_ref, mix_in_ref,
                      o_ref, kbuf, vbuf, sem, *, layer, past):
    del mix_in_ref
    b = pl.program_id(0)
    h = pl.program_id(1)
    n_h = DIFF_HEADS
    step = b * n_h + h
    n_steps = pl.num_programs(0) * n_h
    slot = step % SAMPLE_SLOTS

    def copies(cstep):
        cb, ch, cslot = cstep // n_h, cstep % n_h, cstep % SAMPLE_SLOTS
        return (pltpu.make_async_copy(kc_hbm.at[layer, cb, :, ch, :], kbuf.at[cslot], sem.at[0, cslot]),
                pltpu.make_async_copy(vc_hbm.at[layer, cb, :, ch, :], vbuf.at[cslot], sem.at[1, cslot]))

    def start(cstep):
        @pl.when(cstep < n_steps)
        def _():
            for c in copies(cstep):
                c.start()

    @pl.when(step == 0)
    def _():
        for ahead in range(SAMPLE_SLOTS - 1):
            start(step + ahead)

    start(step + SAMPLE_SLOTS - 1)

    qs = (q_ref[...] * (DIFF_D ** -0.5)).astype(BF16)
    kn = kn_ref[...]
    bias_c = bias_ref[0, 0, :, :past]
    bias_n = bias_ref[0, 0, :, past:]
    for c in copies(step):
        c.wait()
    ps = []
    for half in range(2):
        cols = slice(half * DIFF_D, (half + 1) * DIFF_D)
        sc = _dot_nt(qs[:, cols], kbuf[slot, :, cols].astype(BF16)) + bias_c
        sn = _dot_nt(qs[:, cols], kn[:, cols]) + bias_n
        m = jnp.maximum(jnp.max(sc, axis=-1, keepdims=True), jnp.max(sn, axis=-1, keepdims=True))
        pc = jnp.exp(sc - m)
        pn = jnp.exp(sn - m)
        inv = 1.0 / (jnp.sum(pc, axis=-1, keepdims=True) + jnp.sum(pn, axis=-1, keepdims=True))
        ps.append((pc * inv, pn * inv))
    lam, lam_init = _lambda(lam_ref, layer)
    wc = (ps[0][0] - lam * ps[1][0]).astype(BF16)
    wn = (ps[0][1] - lam * ps[1][1]).astype(BF16)
    o = _dot(wc, vbuf[slot].astype(BF16)) + _dot(wn, vn_ref[...])
    o_ref[...] = ((o * _rms_scale(o)) * nw_ref[...] * (1.0 - lam_init)).astype(BF16)


def _attn_sample(main, kvb, cache_k, cache_v, bias, lam, nw, mix, *, layer, row_off, batch, seq):
    past = cache_k.shape[2]
    assert row_off % seq == 0
    rb0 = row_off // seq
    q_col0 = DQ_OFF // (2 * DIFF_D)
    v_col0 = DIFF_QK_W // DIFF_DV
    o_col0 = GLA_V_W // DIFF_DV
    return pl.pallas_call(
        functools.partial(_attn_sample_body, layer=layer, past=past),
        grid=(batch, DIFF_HEADS),
        in_specs=[
            pl.BlockSpec((seq, 2 * DIFF_D), lambda b, h: (rb0 + b, q_col0 + h)),
            pl.BlockSpec((seq, 2 * DIFF_D), lambda b, h: (rb0 + b, h)),
            pl.BlockSpec((seq, DIFF_DV), lambda b, h: (rb0 + b, v_col0 + h)),
            pl.BlockSpec(memory_space=pl.ANY),
            pl.BlockSpec(memory_space=pl.ANY),
            pl.BlockSpec((1, 1, seq, past + seq), lambda b, h: (h, 0, 0, 0)),
            pl.BlockSpec((4, DIFF_D), lambda b, h: (0, 0)),
            pl.BlockSpec((1, DIFF_DV), lambda b, h: (0, 0)),
            pl.BlockSpec(memory_space=pl.ANY),
        ],
        out_specs=pl.BlockSpec((seq, DIFF_DV), lambda b, h: (rb0 + b, o_col0 + h)),
        out_shape=jax.ShapeDtypeStruct(mix.shape, mix.dtype),
        scratch_shapes=[
            pltpu.VMEM((SAMPLE_SLOTS, past, 2 * DIFF_D), F32),
            pltpu.VMEM((SAMPLE_SLOTS, past, DIFF_DV), F32),
            pltpu.SemaphoreType.DMA((2, SAMPLE_SLOTS)),
        ],
        input_output_aliases={8: 0},
        compiler_params=_params("arbitrary", "arbitrary"),
        name="attn_sample",
    )(main, kvb, kvb, cache_k, cache_v, bias, lam, nw, mix)


def _outproj_body(o_ref, w_ref, x_ref, y_ref):
    y_ref[...] = x_ref[...] + _dot(o_ref[...], w_ref[...])


def _outproj(mix, w, x):
    n, d = x.shape
    tm = _row_tile(n)
    return pl.pallas_call(
        _outproj_body,
        grid=(n // tm,),
        in_specs=[
            pl.BlockSpec((tm, MIX_WIDTH), lambda i: (i, 0)),
            pl.BlockSpec((MIX_WIDTH, d), lambda i: (0, 0)),
            pl.BlockSpec((tm, d), lambda i: (i, 0)),
        ],
        out_specs=pl.BlockSpec((tm, d), lambda i: (i, 0)),
        out_shape=jax.ShapeDtypeStruct((n, d), F32),
        compiler_params=_params("parallel"),
        name="outproj",
    )(mix, w, x)


def kernel(x_prompt, x_sample, cache_k, cache_v, state_gla, ffn1_norm, ffn1_w_gate, ffn1_w_up, ffn1_w_down,
           mix_norm, w_in, gla_w_gk, gla_b_gk, gla_norm, diff_lambda, diff_norm, w_out,
           ffn2_norm, ffn2_w_gate, ffn2_w_up, ffn2_w_down, rel_bias, final_norm):
    pb, ps, d = x_prompt.shape
    sb, ss, _ = x_sample.shape
    depth = w_in.shape[0]
    past = cache_k.shape[2]
    assert pb == 1
    n_p, n_s = pb * ps, sb * ss
    xs = [x_prompt.reshape(n_p, d), x_sample.reshape(n_s, d)]

    blk = _row_tile(n_p + n_s)
    ffn1_w = (ffn1_w_gate, ffn1_w_up, ffn1_w_down)
    ffn2_w = (ffn2_w_gate, ffn2_w_up, ffn2_w_down)
    bias_p, first = _bias_table(rel_bias, ffn1_w, [0] * len(ffn1_w), tiles=3, rows=blk, cols=blk, q_off=0, k_off=0,
                                k_step=-blk, key_major=True, scale=LOG2E)
    bias_s, _ = _bias_table(rel_bias, tiles=1, rows=ss, cols=past + ss, q_off=past, k_off=0, k_step=0)
    zero_state = jnp.zeros((1, pb, GLA_HEADS, GLA_DK, GLA_DV), F32)
    row = lambda a: a.reshape(1, -1)

    w_main, w_gz = _cast_w_in(jnp.swapaxes(w_in, 1, 2))
    f1 = {0: list(first)}
    later = [(w, l) for l in range(depth) for w in ((ffn1_w if l else ()) + ffn2_w + (w_out,))]
    f2, w_o = {}, {}

    kv_out = [jnp.zeros((depth, pb, ps, DIFF_HEADS, DIFF_DV), F32) for _ in range(2)]
    kv_out += [jnp.zeros((depth, sb, ss, DIFF_HEADS, DIFF_DV), F32) for _ in range(2)]
    p_states, s_states = [], []
    mix = jnp.zeros((n_p + n_s, MIX_WIDTH), BF16)
    for l in range(depth):
        w_gk = jnp.pad(gla_w_gk[l], ((0, LANE - GLA_GK_RANK), (0, 0))).astype(BF16)

        x, = _ffn(xs, row(ffn1_norm[l]), *f1[l], row(final_norm), False, [n_p + n_s])
        (main, kvb, vt, g), kv_out = _inproj(x, row(mix_norm[l]), w_main, w_gz, w_gk, row(gla_b_gk[l]), kv_out,
                                             layer=l, split=n_p // blk, seq_s=ss)

        mix, sp = _gla(main, g, zero_state, row(gla_norm[l]), mix,
                       layer=0, row_off=0, batch=pb, seq=ps, chunk=CHUNK, rows=min(GLA_ROWS, ps))
        mix, s_s = _gla(main, g, state_gla, row(gla_norm[l]), mix,
                        layer=l, row_off=n_p, batch=sb, seq=ss, chunk=ss, rows=ss)
        mix, cast = _attn_prompt(main, kvb, vt, bias_p, diff_lambda[l], diff_norm[l].reshape(-1, 1), mix,
                                 [w for w, _ in later] if l == 0 else [], [k for _, k in later] if l == 0 else [],
                                 layer=l, seq=ps)
        cast = list(cast)
        for k in range(depth if l == 0 else 0):
            if k:
                f1[k] = [cast.pop(0) for _ in ffn1_w]
            f2[k] = [cast.pop(0) for _ in ffn2_w]
            w_o[k] = cast.pop(0)
        mix = _attn_sample(main, kvb, cache_k, cache_v, bias_s, diff_lambda[l], row(diff_norm[l]), mix,
                           layer=l, row_off=n_p, batch=sb, seq=ss)

        x = _outproj(mix, w_o[l], x)
        last = l == depth - 1
        xs = _ffn([x], row(ffn2_norm[l]), *f2[l], row(final_norm), last, [n_p, n_s] if last else [n_p + n_s])
        p_states.append(sp)
        s_states.append(s_s)

    prompt_k, prompt_v, sample_k, sample_v = kv_out
    return (xs[0].reshape(pb, ps, d), xs[1].reshape(sb, ss, d), prompt_k, prompt_v, jnp.stack(p_states),
            sample_k, sample_v, jnp.stack(s_states))
```

```python
import functools
import math

import jax
import jax.numpy as jnp
from jax import lax
from jax.experimental import pallas as pl
from jax.experimental.pallas import tpu as pltpu

F32 = jnp.float32
BF16 = jnp.bfloat16

EPS = 1e-6
CHUNK = 64
GLA_HEADS = 4
GLA_DK = 128
GLA_DV = 256
GLA_GK_RANK = 16
GLA_GATE_NORMALIZER = 16.0
DIFF_HEADS = 4
DIFF_D = 128
DIFF_DV = 2 * DIFF_D
N_BUCKETS = 32
MAX_DISTANCE = 128

GLA_K_W = GLA_HEADS * GLA_DK
GLA_V_W = GLA_HEADS * GLA_DV
DIFF_QK_W = DIFF_HEADS * 2 * DIFF_D
DIFF_V_W = DIFF_HEADS * DIFF_DV
MIX_WIDTH = GLA_V_W + DIFF_V_W
GZ_OFF = 2 * GLA_K_W + 2 * GLA_V_W
MAIN_W = GZ_OFF + DIFF_QK_W
DQ_OFF = GZ_OFF

LANE = 128
BF16_ROWS = 16
LOG2E = math.log2(math.e)
SUB = 64
VMEM_LIMIT = 56 * 1024 * 1024

ROW_TILE = 512
FFN_ROWS = 1024
FF_TILE = 512
IN_TILE = 1024
GLA_ROWS = 512
SAMPLE_SLOTS = 3


def _dot(a, b):
    return jnp.dot(a, b, preferred_element_type=F32)


def _dot_nt(a, b):
    return lax.dot_general(a, b, (((1,), (1,)), ((), ())), preferred_element_type=F32)


def _dot_tn(a, b):
    return lax.dot_general(a, b, (((0,), (0,)), ((), ())), preferred_element_type=F32)


def _params(*sem):
    return pltpu.CompilerParams(dimension_semantics=sem, vmem_limit_bytes=VMEM_LIMIT)


def _row_tile(n):
    t = ROW_TILE
    while n % t:
        t //= 2
    return t


def _rms_scale(x):
    return lax.rsqrt(jnp.mean(x * x, axis=-1, keepdims=True) + EPS)


CAST_ROWS = 256


def _cast_rows(rows, steps):
    return -(-rows // (BF16_ROWS * steps)) * BF16_ROWS


class _SideCasts:
    def __init__(self, weights, layers, steps):
        self.layers = layers
        self.steps = steps
        self.shapes = [w.shape[1:] for w in weights]
        self.rows = [_cast_rows(s[0], steps) for s in self.shapes]
        assert all(s[0] >= r and s[0] % BF16_ROWS == 0 for s, r in zip(self.shapes, self.rows))

    def __len__(self):
        return len(self.shapes)

    def out_shapes(self):
        return [jax.ShapeDtypeStruct(s, BF16) for s in self.shapes]

    def scratch_shapes(self):
        n = len(self)
        if n == 0:
            return []
        return ([pltpu.VMEM((2, r, s[1]), F32) for s, r in zip(self.shapes, self.rows)]
                + [pltpu.VMEM((r, s[1]), BF16) for s, r in zip(self.shapes, self.rows)]
                + [pltpu.SemaphoreType.DMA((n, 2)), pltpu.SemaphoreType.DMA((n,))])

    def run(self, step, srcs, dsts, scratch):
        n = len(self)
        if n == 0:
            return
        in_bufs, out_bufs, in_sem, out_sem = scratch[:n], scratch[n:2 * n], scratch[2 * n], scratch[2 * n + 1]

        def start_row(k, t):
            r, total = self.rows[k], self.shapes[k][0]
            return pl.multiple_of(jnp.minimum(t * r, total - r), BF16_ROWS)

        def fetch(k, t):
            return pltpu.make_async_copy(srcs[k].at[self.layers[k], pl.ds(start_row(k, t), self.rows[k]), :],
                                         in_bufs[k].at[t % 2], in_sem.at[k, t % 2])

        def store(k, t):
            return pltpu.make_async_copy(out_bufs[k], dsts[k].at[pl.ds(start_row(k, t), self.rows[k]), :],
                                         out_sem.at[k])

        @pl.when(step == 0)
        def _():
            for k in range(n):
                fetch(k, step).start()

        @pl.when(step + 1 < self.steps)
        def _():
            for k in range(n):
                fetch(k, step + 1).start()

        for k in range(n):
            fetch(k, step).wait()

            @pl.when(step > 0)
            def _():
                store(k, step - 1).wait()

            out_bufs[k][...] = in_bufs[k][step % 2].astype(BF16)
            store(k, step).start()

        @pl.when(step == self.steps - 1)
        def _():
            for k in range(n):
                store(k, step).wait()


GZ_BLK = GZ_OFF // CAST_ROWS
assert GZ_OFF % CAST_ROWS == 0 and GLA_GK_RANK % BF16_ROWS == 0 and GLA_GK_RANK < CAST_ROWS


def _cast_w_in_body(a_ref, b_ref, main_ref, gz_ref):
    k = pl.program_id(1)

    @pl.when(k < GZ_BLK)
    def _():
        main_ref[0] = a_ref[0].astype(BF16)

    @pl.when(k >= GZ_BLK)
    def _():
        main_ref[0, :CAST_ROWS - GLA_GK_RANK] = a_ref[0, GLA_GK_RANK:].astype(BF16)
        main_ref[0, CAST_ROWS - GLA_GK_RANK:] = b_ref[0, :GLA_GK_RANK].astype(BF16)

    @pl.when(k == GZ_BLK)
    def _():
        gz_ref[0, :GLA_GK_RANK] = a_ref[0, :GLA_GK_RANK].astype(BF16)
        gz_ref[0, GLA_GK_RANK:] = jnp.zeros((LANE - GLA_GK_RANK, a_ref.shape[2]), BF16)


def _cast_w_in(w_in_t):
    depth, c, r = w_in_t.shape
    wide = c - GLA_GK_RANK
    last = pl.cdiv(c, CAST_ROWS) - 1
    return pl.pallas_call(
        _cast_w_in_body,
        grid=(depth, wide // CAST_ROWS),
        in_specs=[pl.BlockSpec((1, CAST_ROWS, r), lambda l, k: (l, k, 0)),
                  pl.BlockSpec((1, CAST_ROWS, r), lambda l, k: (l, jnp.clip(k + 1, GZ_BLK + 1, last), 0))],
        out_specs=[pl.BlockSpec((1, CAST_ROWS, r), lambda l, k: (l, k, 0)),
                   pl.BlockSpec((1, LANE, r), lambda l, k: (l, 0, 0))],
        out_shape=[jax.ShapeDtypeStruct((depth, wide, r), BF16),
                   jax.ShapeDtypeStruct((depth, LANE, r), BF16)],
        compiler_params=_params("parallel", "arbitrary"),
        name="cast_w_in",
    )(w_in_t, w_in_t)


def _ffn_body(*refs, final, n_in, n_out, split):
    x_refs = refs[:n_in]
    nw_ref, wg_ref, wu_ref, wd_ref, fw_ref = refs[n_in:n_in + 5]
    o_refs = refs[n_in + 5:n_in + 5 + n_out]
    scratch = refs[n_in + 5 + n_out:]
    xn_ref = scratch[0]
    i = pl.program_id(0)
    j = pl.program_id(1)
    n_i = pl.num_programs(0)
    tm = xn_ref.shape[0]

    acc_ref = o_refs[0] if n_out == 1 else scratch[1]

    if n_in == 2:
        x_buf, x_sem = scratch[-2:]

        def fetch(t, act):
            @pl.when(t < split)
            def _():
                act(pltpu.make_async_copy(x_refs[0].at[pl.ds(t * tm, tm), :], x_buf.at[t % 2], x_sem.at[t % 2]))

            @pl.when(t >= split)
            def _():
                act(pltpu.make_async_copy(x_refs[1].at[pl.ds((t - split) * tm, tm), :], x_buf.at[t % 2],
                                          x_sem.at[t % 2]))

        @pl.when(j == 0)
        def _():
            @pl.when(i == 0)
            def _():
                fetch(i, lambda c: c.start())

            @pl.when(i + 1 < n_i)
            def _():
                fetch(i + 1, lambda c: c.start())

            fetch(i, lambda c: c.wait())

    def load_x():
        return x_refs[0][...] if n_in == 1 else x_buf[i % 2]

    @pl.when(j == 0)
    def _():
        x = load_x()
        xn_ref[...] = ((x * _rms_scale(x)) * nw_ref[...]).astype(BF16)
        acc_ref[...] = jnp.zeros_like(acc_ref)

    xn = xn_ref[...]
    h = _dot(xn, wg_ref[...])
    u = _dot(xn, wu_ref[...])
    a = (h * jax.nn.sigmoid(h) * u).astype(BF16)
    acc_ref[...] += _dot(a, wd_ref[...])

    @pl.when(j == pl.num_programs(1) - 1)
    def _():
        y = load_x() + 0.5 * acc_ref[...]
        if final:
            y = (y * _rms_scale(y)) * fw_ref[...]
        if n_out == 1:
            o_refs[0][...] = y
        else:
            @pl.when(i < split)
            def _():
                o_refs[0][...] = y

            @pl.when(i >= split)
            def _():
                o_refs[1][...] = y


def _ffn(xs, nw, wg, wu, wd, fw, final, out_rows):
    d = xs[0].shape[1]
    n = sum(x.shape[0] for x in xs)
    assert n == sum(out_rows)
    f = wg.shape[1]
    tm = FFN_ROWS if len(out_rows) == 1 and n % FFN_ROWS == 0 else _row_tile(n)
    first = xs[0].shape[0] if len(xs) == 2 else out_rows[0]
    if first % tm:
        tm = _row_tile(n)
    assert first % tm == 0
    split = first // tm
    anyspace = pl.BlockSpec(memory_space=pl.ANY)
    tile = pl.BlockSpec((tm, d), lambda i, j: (i, 0))
    outs = pl.pallas_call(
        functools.partial(_ffn_body, final=final, n_in=len(xs), n_out=len(out_rows), split=split),
        grid=(n // tm, f // FF_TILE),
        in_specs=([anyspace] * 2 if len(xs) == 2 else [tile]) + [
            pl.BlockSpec((1, d), lambda i, j: (0, 0)),
            pl.BlockSpec((d, FF_TILE), lambda i, j: (0, j)),
            pl.BlockSpec((d, FF_TILE), lambda i, j: (0, j)),
            pl.BlockSpec((FF_TILE, d), lambda i, j: (j, 0)),
            pl.BlockSpec((1, d), lambda i, j: (0, 0)),
        ],
        out_specs=([pl.BlockSpec((tm, d), lambda i, j: (jnp.minimum(i, split - 1), 0)),
                    pl.BlockSpec((tm, d), lambda i, j: (jnp.maximum(i - split, 0), 0))]
                   if len(out_rows) == 2 else [tile]),
        out_shape=[jax.ShapeDtypeStruct((r, d), F32) for r in out_rows],
        scratch_shapes=([pltpu.VMEM((tm, d), BF16)]
                        + ([pltpu.VMEM((tm, d), F32)] if len(out_rows) == 2 else [])
                        + ([pltpu.VMEM((2, tm, d), F32), pltpu.SemaphoreType.DMA((2,))] if len(xs) == 2 else [])),
        compiler_params=_params("parallel" if len(xs) == len(out_rows) == 1 else "arbitrary", "arbitrary"),
        name="ffn",
    )(*xs, nw, wg, wu, wd, fw)
    return outs


N_MAIN_BLK = MAIN_W // IN_TILE
N_K_BLK = DIFF_QK_W // IN_TILE
N_V_BLK = DIFF_V_W // IN_TILE


assert N_K_BLK == 1 and N_V_BLK == 1


def _inproj_body(x_ref, nw_ref, w_ref, wgz_ref, wgk_ref, bgk_ref, *rest, layer, split, seq_s):
    main_ref, kvb_ref, vt_ref, g_ref, pk_ref, pv_ref, sk_ref, sv_ref, xn_ref, y_ref, sem = rest[-11:]
    i = pl.program_id(0)
    j = pl.program_id(1)
    n_i = pl.num_programs(0)
    tm = x_ref.shape[0]

    @pl.when(j == 0)
    def _():
        x = x_ref[...]
        xn = ((x * _rms_scale(x)) * nw_ref[...]).astype(BF16)
        xn_ref[...] = xn
        gz = _dot_nt(xn, wgz_ref[...])
        z = _dot(gz.astype(BF16), wgk_ref[...]) + bgk_ref[...]
        g_ref[...] = (jnp.minimum(z, 0.0) - jnp.log1p(jnp.exp(-jnp.abs(z)))) * (1.0 / GLA_GATE_NORMALIZER)

    y = _dot_nt(xn_ref[...], w_ref[...])

    @pl.when(j < N_MAIN_BLK)
    def _():
        main_ref[...] = y

    def row_copies(slot, tile, dst_p, dst_s, act):
        heads = [slice(h * DIFF_DV, (h + 1) * DIFF_DV) for h in range(DIFF_HEADS)]

        @pl.when(tile < split)
        def _():
            for h, cols in enumerate(heads):
                act(pltpu.make_async_copy(y_ref.at[slot, :, cols],
                                          dst_p.at[layer, 0, pl.ds(tile * tm, tm), h, :], sem.at[slot]))

        @pl.when(tile >= split)
        def _():
            for b in range(tm // seq_s):
                for h, cols in enumerate(heads):
                    act(pltpu.make_async_copy(y_ref.at[slot, b * seq_s:(b + 1) * seq_s, cols],
                                              dst_s.at[layer, (tile - split) * (tm // seq_s) + b, :, h, :],
                                              sem.at[slot]))

    def kv_step(slot, dst_p, dst_s):
        @pl.when(i > 0)
        def _():
            row_copies(slot, i - 1, dst_p, dst_s, lambda c: c.wait())

        y_ref[slot] = y
        kvb_ref[...] = y.astype(BF16)
        if slot == 1:
            vt_ref[0] = y.T.astype(BF16)
        row_copies(slot, i, dst_p, dst_s, lambda c: c.start())

        @pl.when(i == n_i - 1)
        def _():
            row_copies(slot, i, dst_p, dst_s, lambda c: c.wait())

    @pl.when(j == N_MAIN_BLK)
    def _():
        kv_step(0, pk_ref, sk_ref)

    @pl.when(j == N_MAIN_BLK + 1)
    def _():
        kv_step(1, pv_ref, sv_ref)


def _inproj(x, nw, w_main, w_gz, w_gk, b_gk, kv_out, *, layer, split, seq_s):
    n, d = x.shape
    tm = _row_tile(n)
    assert tm % seq_s == 0 and kv_out[0].shape[1] == 1
    nj = N_MAIN_BLK + 2
    k0 = N_MAIN_BLK
    anyspace = pl.BlockSpec(memory_space=pl.ANY)
    outs = pl.pallas_call(
        functools.partial(_inproj_body, layer=layer, split=split, seq_s=seq_s),
        grid=(n // tm, nj),
        in_specs=[
            pl.BlockSpec((tm, d), lambda i, j: (i, 0)),
            pl.BlockSpec((1, d), lambda i, j: (0, 0)),
            pl.BlockSpec((None, IN_TILE, d), lambda i, j: (layer, j, 0)),
            pl.BlockSpec((None, LANE, d), lambda i, j: (layer, 0, 0)),
            pl.BlockSpec((LANE, GLA_K_W), lambda i, j: (0, 0)),
            pl.BlockSpec((1, GLA_K_W), lambda i, j: (0, 0)),
        ] + [anyspace] * len(kv_out),
        out_specs=[
            pl.BlockSpec((tm, IN_TILE), lambda i, j: (i, jnp.minimum(j, k0 - 1))),
            pl.BlockSpec((tm, IN_TILE), lambda i, j: (i, jnp.clip(j - k0, 0, 1))),
            pl.BlockSpec((1, IN_TILE, tm), lambda i, j: (i, 0, 0)),
            pl.BlockSpec((tm, GLA_K_W), lambda i, j: (i, 0)),
            anyspace, anyspace, anyspace, anyspace,
        ],
        out_shape=[
            jax.ShapeDtypeStruct((n, MAIN_W), F32),
            jax.ShapeDtypeStruct((n, DIFF_QK_W + DIFF_V_W), BF16),
            jax.ShapeDtypeStruct((n // tm, DIFF_V_W, tm), BF16),
            jax.ShapeDtypeStruct((n, GLA_K_W), F32),
        ] + [jax.ShapeDtypeStruct(a.shape, a.dtype) for a in kv_out],
        scratch_shapes=[pltpu.VMEM((tm, d), BF16), pltpu.VMEM((2, tm, IN_TILE), F32),
                        pltpu.SemaphoreType.DMA((2,))],
        input_output_aliases={6 + k: 4 + k for k in range(len(kv_out))},
        compiler_params=_params("arbitrary", "arbitrary"),
        name="inproj",
    )(x, nw, w_main, w_gz, w_gk, b_gk, *kv_out)
    return outs[:4], outs[4:]


def _split3(a):
    hi = a.astype(BF16)
    r = a - hi.astype(F32)
    mid = r.astype(BF16)
    lo = (r - mid.astype(F32)).astype(BF16)
    return hi, mid, lo


def _gla_body(q_ref, k_ref, v_ref, r_ref, g_ref, s0_ref, nw_ref, mix_in_ref, o_ref, s_ref, st_ref, *, chunk):
    del mix_in_ref
    t = pl.program_id(1)
    rows = q_ref.shape[0]
    shift = chunk.bit_length() - 1
    assert chunk == 1 << shift

    @pl.when(t == 0)
    def _():
        for h in range(GLA_HEADS):
            st_ref[h] = s0_ref[0, h].T

    ri = lax.broadcasted_iota(jnp.int32, (rows, rows), 0)
    ci = lax.broadcasted_iota(jnp.int32, (rows, rows), 1)
    causal = jnp.logical_and((ri >> shift) == (ci >> shift), ci <= ri)
    tri = causal.astype(BF16)

    g_hi, g_mid, g_lo = _split3(g_ref[...])
    b_all = _dot(tri, g_hi) + _dot(tri, g_mid) + _dot(tri, g_lo)

    for h in range(GLA_HEADS):
        kc = slice(h * GLA_DK, (h + 1) * GLA_DK)
        vc = slice(h * GLA_DV, (h + 1) * GLA_DV)
        b = b_all[:, kc]
        k = k_ref[:, kc]
        v = v_ref[:, vc].astype(BF16)
        qe = (q_ref[:, kc] * (GLA_DK ** -0.5) * jnp.exp(b)).astype(BF16)
        ke = (k * jnp.exp(-b)).astype(BF16)
        a = jnp.where(causal, _dot_nt(qe, ke), 0.0).astype(BF16)
        o_in = _dot(a, v)
        for c in range(rows // chunk):
            sl = slice(c * chunk, (c + 1) * chunk)
            b_last = b[(c + 1) * chunk - 1:(c + 1) * chunk, :]
            kd = (k[sl] * jnp.exp(b_last - b[sl])).astype(BF16)
            st = st_ref[h]
            o = _dot_nt(qe[sl], st.astype(BF16)) + o_in[sl]
            st_ref[h] = st * jnp.exp(b_last) + _dot_tn(v[sl], kd)
            o = (o * _rms_scale(o)) * nw_ref[...]
            r = r_ref[sl, vc]
            o_ref[sl, vc] = (o * (r * jax.nn.sigmoid(r))).astype(BF16)

    @pl.when(t == pl.num_programs(1) - 1)
    def _():
        for h in range(GLA_HEADS):
            s_ref[0, h] = st_ref[h].T


def _gla(main, g, s0, nw, mix, *, layer, row_off, batch, seq, chunk, rows):
    assert seq % rows == 0 and rows % chunk == 0 and row_off % rows == 0
    nt = seq // rows
    rb0 = row_off // rows

    def rowblk(b, t):
        return rb0 + b * nt + t

    state_spec = pl.BlockSpec((1, GLA_HEADS, GLA_DK, GLA_DV), lambda b, t: (b, 0, 0, 0))
    state_in = pl.BlockSpec((None, 1, GLA_HEADS, GLA_DK, GLA_DV), lambda b, t: (layer, b, 0, 0, 0))
    out, s = pl.pallas_call(
        functools.partial(_gla_body, chunk=chunk),
        grid=(batch, nt),
        in_specs=[
            pl.BlockSpec((rows, GLA_K_W), lambda b, t: (rowblk(b, t), 0)),
            pl.BlockSpec((rows, GLA_K_W), lambda b, t: (rowblk(b, t), 1)),
            pl.BlockSpec((rows, GLA_V_W), lambda b, t: (rowblk(b, t), 2 * GLA_K_W // GLA_V_W)),
            pl.BlockSpec((rows, GLA_V_W), lambda b, t: (rowblk(b, t), 2 * GLA_K_W // GLA_V_W + 1)),
            pl.BlockSpec((rows, GLA_K_W), lambda b, t: (rowblk(b, t), 0)),
            state_in,
            pl.BlockSpec((1, GLA_DV), lambda b, t: (0, 0)),
            pl.BlockSpec(memory_space=pl.ANY),
        ],
        out_specs=[
            pl.BlockSpec((rows, GLA_V_W), lambda b, t: (rowblk(b, t), 0)),
            state_spec,
        ],
        out_shape=[
            jax.ShapeDtypeStruct(mix.shape, mix.dtype),
            jax.ShapeDtypeStruct((batch, GLA_HEADS, GLA_DK, GLA_DV), F32),
        ],
        scratch_shapes=[pltpu.VMEM((GLA_HEADS, GLA_DV, GLA_DK), F32)],
        input_output_aliases={7: 0},
        compiler_params=_params("parallel", "arbitrary"),
        name="gla",
    )(main, main, main, main, g, s0, nw, mix)
    return out, s


def _bucket_thresholds():
    half = N_BUCKETS // 2
    m = half // 2
    e = half - m
    thr = []
    for kk in range(1, e):
        n = m
        while n ** e * m ** kk < m ** e * MAX_DISTANCE ** kk:
            n += 1
        thr.append(n)
    return tuple(thr)


_BUCKET_THR = _bucket_thresholds()


def _t5_bucket(rel):
    half = N_BUCKETS // 2
    max_exact = half // 2
    n = jnp.abs(rel)
    large = jnp.full(rel.shape, max_exact, jnp.int32)
    for thr in _BUCKET_THR:
        large = large + (n >= thr).astype(jnp.int32)
    return jnp.where(rel > 0, half, 0) + jnp.where(n < max_exact, n, large)


def _bias_body(rb_ref, *rest, q_off, k_off, k_step, key_major, scale, casts):
    n = len(casts)
    o_ref = rest[n]
    h = pl.program_id(0)
    t = pl.program_id(1)
    casts.run(h * pl.num_programs(1) + t, rest[:n], rest[n + 1:2 * n + 1], rest[2 * n + 1:])
    shape = o_ref.shape[2:]
    qpos = q_off + lax.broadcasted_iota(jnp.int32, shape, 1 if key_major else 0)
    kpos = k_off + t * k_step + lax.broadcasted_iota(jnp.int32, shape, 0 if key_major else 1)
    bucket = _t5_bucket(kpos - qpos)
    acc = jnp.zeros(shape, F32)
    for bkt in range(N_BUCKETS):
        acc = jnp.where(bucket == bkt, rb_ref[bkt, h], acc)
    visible = (kpos >> 6) <= (qpos >> 6)
    o_ref[0, 0] = jnp.where(visible, acc * scale, -jnp.inf)


assert CHUNK == 64


def _bias_table(rel_bias, cast_weights=(), cast_layers=(), *, tiles, rows, cols, q_off, k_off, k_step,
                key_major=False, scale=1.0):
    casts = _SideCasts(cast_weights, cast_layers, DIFF_HEADS * tiles)
    anyspace = pl.BlockSpec(memory_space=pl.ANY)
    outs = pl.pallas_call(
        functools.partial(_bias_body, q_off=q_off, k_off=k_off, k_step=k_step, key_major=key_major,
                          scale=scale, casts=casts),
        grid=(DIFF_HEADS, tiles),
        in_specs=[pl.BlockSpec(memory_space=pltpu.SMEM)] + [anyspace] * len(casts),
        out_specs=[pl.BlockSpec((1, 1, rows, cols), lambda h, t: (h, t, 0, 0))] + [anyspace] * len(casts),
        out_shape=[jax.ShapeDtypeStruct((DIFF_HEADS, tiles, rows, cols), F32)] + casts.out_shapes(),
        scratch_shapes=casts.scratch_shapes(),
        compiler_params=_params("arbitrary", "arbitrary"),
        name="bias_table",
    )(rel_bias, *cast_weights)
    return outs[0], outs[1:]


def _lambda(lam_ref, layer):
    lam = lam_ref[...]
    a = jnp.sum(lam[0:1, :] * lam[1:2, :], axis=-1, keepdims=True)
    b = jnp.sum(lam[2:3, :] * lam[3:4, :], axis=-1, keepdims=True)
    lam_init = 0.8 - 0.6 * math.exp(-0.3 * layer)
    return jnp.exp(a) - jnp.exp(b) + lam_init, lam_init


def _attn_prompt_body(q_ref, k_ref, vt_ref, bias_ref, lam_ref, nw_ref, mix_in_ref, *rest, layer, blk, casts):
    del mix_in_ref
    n = len(casts)
    cast_srcs, o_ref, cast_dsts = rest[:n], rest[n], rest[n + 1:2 * n + 1]
    acc_ref, s_ref, p_ref = rest[2 * n + 1:2 * n + 4]
    casts.run(pl.program_id(0) * pl.num_programs(1) + pl.program_id(1), cast_srcs, cast_dsts, rest[2 * n + 4:])
    qi = pl.program_id(1)
    qt = (q_ref[...] * (DIFF_D ** -0.5 * LOG2E)).T.astype(BF16)
    qth = (qt[:DIFF_D], qt[DIFF_D:])
    acc_ref[...] = jnp.zeros_like(acc_ref)
    strips = [slice(c * SUB, (c + 1) * SUB) for c in range(blk // SUB)]

    def fold(a):
        return a.reshape(SUB // 8, 8, blk)

    p_ref[...] = jnp.zeros_like(p_ref)

    def step(kj, stats):
        kb = k_ref[pl.ds(pl.multiple_of(kj * blk, blk), blk), :]
        vt_prev = vt_ref[jnp.maximum(kj - 1, 0)]
        t = jnp.minimum(qi - kj, 2)
        for half in range(2):
            s_ref[half] = _dot(kb[:, half * DIFF_D:(half + 1) * DIFF_D], qth[half]) + bias_ref[0, t]
        for half in range(2):
            acc_ref[half] += _dot(vt_prev, p_ref[half])
        out = []
        for half in range(2):
            m, l = stats[half]
            bm = jnp.max(fold(s_ref[half, strips[0], :]), axis=0)
            for c in strips[1:]:
                bm = jnp.maximum(bm, jnp.max(fold(s_ref[half, c, :]), axis=0))
            m_new = jnp.maximum(m, jnp.max(bm, axis=0, keepdims=True))
            alpha = jnp.exp2(m - m_new)
            acc_ref[half] = alpha * acc_ref[half]
            ls = jnp.zeros((8, blk), F32)
            for c in strips:
                p = jnp.exp2(s_ref[half, c, :] - m_new)
                ls = ls + jnp.sum(fold(p), axis=0)
                p_ref[half, c, :] = p.astype(BF16)
            l = alpha * l + jnp.sum(ls, axis=0, keepdims=True)
            out.append((m_new, l))
        return tuple(out)

    init = tuple((jnp.full((1, blk), -jnp.inf, F32), jnp.zeros((1, blk), F32)) for _ in range(2))
    stats = lax.fori_loop(0, qi + 1, step, init)

    lam, lam_init = _lambda(lam_ref, layer)
    vt_last = vt_ref[qi]
    o1 = acc_ref[0] + _dot(vt_last, p_ref[0])
    o2 = acc_ref[1] + _dot(vt_last, p_ref[1])
    ot = o1 / stats[0][1] - lam * (o2 / stats[1][1])
    scale = lax.rsqrt(jnp.mean(ot * ot, axis=0, keepdims=True) + EPS)
    ot = (ot * scale) * nw_ref[...] * (1.0 - lam_init)
    o_ref[...] = ot.T.astype(BF16)


def _attn_prompt(main, kvb, vt, bias, lam, nw_col, mix, cast_weights=(), cast_layers=(), *, layer, seq):
    blk = vt.shape[2]
    assert seq % blk == 0 and blk >= MAX_DISTANCE and blk % CHUNK == 0
    q_col0 = DQ_OFF // (2 * DIFF_D)
    o_col0 = GLA_V_W // DIFF_DV
    casts = _SideCasts(cast_weights, cast_layers, DIFF_HEADS * (seq // blk))
    anyspace = pl.BlockSpec(memory_space=pl.ANY)
    outs = pl.pallas_call(
        functools.partial(_attn_prompt_body, layer=layer, blk=blk, casts=casts),
        grid=(DIFF_HEADS, seq // blk),
        in_specs=[
            pl.BlockSpec((blk, 2 * DIFF_D), lambda h, i: (i, q_col0 + h)),
            pl.BlockSpec((seq, 2 * DIFF_D), lambda h, i: (0, h)),
            pl.BlockSpec((seq // blk, DIFF_DV, blk), lambda h, i: (0, h, 0)),
            pl.BlockSpec((1, 3, blk, blk), lambda h, i: (h, 0, 0, 0)),
            pl.BlockSpec((4, DIFF_D), lambda h, i: (0, 0)),
            pl.BlockSpec((DIFF_DV, 1), lambda h, i: (0, 0)),
            anyspace,
        ] + [anyspace] * len(casts),
        out_specs=[pl.BlockSpec((blk, DIFF_DV), lambda h, i: (i, o_col0 + h))] + [anyspace] * len(casts),
        out_shape=[jax.ShapeDtypeStruct(mix.shape, mix.dtype)] + casts.out_shapes(),
        scratch_shapes=[pltpu.VMEM((2, DIFF_DV, blk), F32), pltpu.VMEM((2, blk, blk), F32),
                        pltpu.VMEM((2, blk, blk), BF16)] + casts.scratch_shapes(),
        input_output_aliases={6: 0},
        compiler_params=_params("arbitrary", "arbitrary"),
        name="attn_prompt",
    )(main, kvb, vt, bias, lam, nw_col, mix, *cast_weights)
    return outs[0], outs[1:]


def _attn_sample_body(q_ref, kn_ref, vn_ref, kc_hbm, vc_hbm, bias_ref, lam_ref, nw_ref, mix_in_ref,
                      o_ref, kbuf, vbuf, sem, *, layer, past):
    del mix_in_ref
    b = pl.program_id(0)
    h = pl.program_id(1)
    n_h = DIFF_HEADS
    step = b * n_h + h
    n_steps = pl.num_programs(0) * n_h
    slot = step % SAMPLE_SLOTS

    def copies(cstep):
        cb, ch, cslot = cstep // n_h, cstep % n_h, cstep % SAMPLE_SLOTS
        return (pltpu.make_async_copy(kc_hbm.at[layer, cb, :, ch, :], kbuf.at[cslot], sem.at[0, cslot]),
                pltpu.make_async_copy(vc_hbm.at[layer, cb, :, ch, :], vbuf.at[cslot], sem.at[1, cslot]))

    def start(cstep):
        @pl.when(cstep < n_steps)
        def _():
            for thread, c in enumerate(copies(cstep)):
                c.start(priority=thread)

    @pl.when(step == 0)
    def _():
        for ahead in range(SAMPLE_SLOTS - 1):
            start(step + ahead)

    start(step + SAMPLE_SLOTS - 1)

    qs = (q_ref[...] * (DIFF_D ** -0.5)).astype(BF16)
    kn = kn_ref[...]
    bias_c = bias_ref[0, 0, :, :past]
    bias_n = bias_ref[0, 0, :, past:]
    for c in copies(step):
        c.wait()
    ps = []
    for half in range(2):
        cols = slice(half * DIFF_D, (half + 1) * DIFF_D)
        sc = _dot_nt(qs[:, cols], kbuf[slot, :, cols].astype(BF16)) + bias_c
        sn = _dot_nt(qs[:, cols], kn[:, cols]) + bias_n
        m = jnp.maximum(jnp.max(sc, axis=-1, keepdims=True), jnp.max(sn, axis=-1, keepdims=True))
        pc = jnp.exp(sc - m)
        pn = jnp.exp(sn - m)
        inv = 1.0 / (jnp.sum(pc, axis=-1, keepdims=True) + jnp.sum(pn, axis=-1, keepdims=True))
        ps.append((pc * inv, pn * inv))
    lam, lam_init = _lambda(lam_ref, layer)
    wc = (ps[0][0] - lam * ps[1][0]).astype(BF16)
    wn = (ps[0][1] - lam * ps[1][1]).astype(BF16)
    o = _dot(wc, vbuf[slot].astype(BF16)) + _dot(wn, vn_ref[...])
    o_ref[...] = ((o * _rms_scale(o)) * nw_ref[...] * (1.0 - lam_init)).astype(BF16)


def _attn_sample(main, kvb, cache_k, cache_v, bias, lam, nw, mix, *, layer, row_off, batch, seq):
    past = cache_k.shape[2]
    assert row_off % seq == 0
    rb0 = row_off // seq
    q_col0 = DQ_OFF // (2 * DIFF_D)
    v_col0 = DIFF_QK_W // DIFF_DV
    o_col0 = GLA_V_W // DIFF_DV
    return pl.pallas_call(
        functools.partial(_attn_sample_body, layer=layer, past=past),
        grid=(batch, DIFF_HEADS),
        in_specs=[
            pl.BlockSpec((seq, 2 * DIFF_D), lambda b, h: (rb0 + b, q_col0 + h)),
            pl.BlockSpec((seq, 2 * DIFF_D), lambda b, h: (rb0 + b, h)),
            pl.BlockSpec((seq, DIFF_DV), lambda b, h: (rb0 + b, v_col0 + h)),
            pl.BlockSpec(memory_space=pl.ANY),
            pl.BlockSpec(memory_space=pl.ANY),
            pl.BlockSpec((1, 1, seq, past + seq), lambda b, h: (h, 0, 0, 0)),
            pl.BlockSpec((4, DIFF_D), lambda b, h: (0, 0)),
            pl.BlockSpec((1, DIFF_DV), lambda b, h: (0, 0)),
            pl.BlockSpec(memory_space=pl.ANY),
        ],
        out_specs=pl.BlockSpec((seq, DIFF_DV), lambda b, h: (rb0 + b, o_col0 + h)),
        out_shape=jax.ShapeDtypeStruct(mix.shape, mix.dtype),
        scratch_shapes=[
            pltpu.VMEM((SAMPLE_SLOTS, past, 2 * DIFF_D), F32),
            pltpu.VMEM((SAMPLE_SLOTS, past, DIFF_DV), F32),
            pltpu.SemaphoreType.DMA((2, SAMPLE_SLOTS)),
        ],
        input_output_aliases={8: 0},
        compiler_params=_params("arbitrary", "arbitrary"),
        name="attn_sample",
    )(main, kvb, kvb, cache_k, cache_v, bias, lam, nw, mix)


def _outproj_body(o_ref, w_ref, x_ref, y_ref):
    y_ref[...] = x_ref[...] + _dot(o_ref[...], w_ref[...])


def _outproj(mix, w, x):
    n, d = x.shape
    tm = _row_tile(n)
    return pl.pallas_call(
        _outproj_body,
        grid=(n // tm,),
        in_specs=[
            pl.BlockSpec((tm, MIX_WIDTH), lambda i: (i, 0)),
            pl.BlockSpec((MIX_WIDTH, d), lambda i: (0, 0)),
            pl.BlockSpec((tm, d), lambda i: (i, 0)),
        ],
        out_specs=pl.BlockSpec((tm, d), lambda i: (i, 0)),
        out_shape=jax.ShapeDtypeStruct((n, d), F32),
        compiler_params=_params("parallel"),
        name="outproj",
    )(mix, w, x)


def kernel(x_prompt, x_sample, cache_k, cache_v, state_gla, ffn1_norm, ffn1_w_gate, ffn1_w_up, ffn1_w_down,
           mix_norm, w_in, gla_w_gk, gla_b_gk, gla_norm, diff_lambda, diff_norm, w_out,
           ffn2_norm, ffn2_w_gate, ffn2_w_up, ffn2_w_down, rel_bias, final_norm):
    pb, ps, d = x_prompt.shape
    sb, ss, _ = x_sample.shape
    depth = w_in.shape[0]
    past = cache_k.shape[2]
    assert pb == 1
    n_p, n_s = pb * ps, sb * ss
    xs = [x_prompt.reshape(n_p, d), x_sample.reshape(n_s, d)]

    blk = _row_tile(n_p + n_s)
    ffn1_w = (ffn1_w_gate, ffn1_w_up, ffn1_w_down)
    ffn2_w = (ffn2_w_gate, ffn2_w_up, ffn2_w_down)
    bias_p, first = _bias_table(rel_bias, ffn1_w, [0] * len(ffn1_w), tiles=3, rows=blk, cols=blk, q_off=0, k_off=0,
                                k_step=-blk, key_major=True, scale=LOG2E)
    bias_s, _ = _bias_table(rel_bias, tiles=1, rows=ss, cols=past + ss, q_off=past, k_off=0, k_step=0)
    zero_state = jnp.zeros((1, pb, GLA_HEADS, GLA_DK, GLA_DV), F32)
    row = lambda a: a.reshape(1, -1)

    w_main, w_gz = _cast_w_in(jnp.swapaxes(w_in, 1, 2))
    f1 = {0: list(first)}
    later = [(w, l) for l in range(depth) for w in ((ffn1_w if l else ()) + ffn2_w + (w_out,))]
    f2, w_o = {}, {}

    kv_out = [jnp.zeros((depth, pb, ps, DIFF_HEADS, DIFF_DV), F32) for _ in range(2)]
    kv_out += [jnp.zeros((depth, sb, ss, DIFF_HEADS, DIFF_DV), F32) for _ in range(2)]
    p_states, s_states = [], []
    mix = jnp.zeros((n_p + n_s, MIX_WIDTH), BF16)
    for l in range(depth):
        w_gk = jnp.pad(gla_w_gk[l], ((0, LANE - GLA_GK_RANK), (0, 0))).astype(BF16)

        x, = _ffn(xs, row(ffn1_norm[l]), *f1[l], row(final_norm), False, [n_p + n_s])
        (main, kvb, vt, g), kv_out = _inproj(x, row(mix_norm[l]), w_main, w_gz, w_gk, row(gla_b_gk[l]), kv_out,
                                             layer=l, split=n_p // blk, seq_s=ss)

        mix, sp = _gla(main, g, zero_state, row(gla_norm[l]), mix,
                       layer=0, row_off=0, batch=pb, seq=ps, chunk=CHUNK, rows=min(GLA_ROWS, ps))
        mix, s_s = _gla(main, g, state_gla, row(gla_norm[l]), mix,
                        layer=l, row_off=n_p, batch=sb, seq=ss, chunk=ss, rows=ss)
        mix, cast = _attn_prompt(main, kvb, vt, bias_p, diff_lambda[l], diff_norm[l].reshape(-1, 1), mix,
                                 [w for w, _ in later] if l == 0 else [], [k for _, k in later] if l == 0 else [],
                                 layer=l, seq=ps)
        cast = list(cast)
        for k in range(depth if l == 0 else 0):
            if k:
                f1[k] = [cast.pop(0) for _ in ffn1_w]
            f2[k] = [cast.pop(0) for _ in ffn2_w]
            w_o[k] = cast.pop(0)
        mix = _attn_sample(main, kvb, cache_k, cache_v, bias_s, diff_lambda[l], row(diff_norm[l]), mix,
                           layer=l, row_off=n_p, batch=sb, seq=ss)

        x = _outproj(mix, w_o[l], x)
        last = l == depth - 1
        xs = _ffn([x], row(ffn2_norm[l]), *f2[l], row(final_norm), last, [n_p, n_s] if last else [n_p + n_s])
        p_states.append(sp)
        s_states.append(s_s)

    prompt_k, prompt_v, sample_k, sample_v = kv_out
    return (xs[0].reshape(pb, ps, d), xs[1].reshape(sb, ss, d), prompt_k, prompt_v, jnp.stack(p_states),
            sample_k, sample_v, jnp.stack(s_states))
```
